```python
import math
import jax
import jax.numpy as jnp
from jax import lax
import numpy as np

D_MODEL = 1024
BATCH = 4
SEQ = 4096
DEPTH = 2

GRID_W = 64
CTX_LEN = 256
HEAD_DIM = 64
N_BRANCH = 4
BRANCH_WIDTH = 512
NA_HEADS = 8
NA_WIN_ROWS = 8
NA_WIN_COLS = 16
SWA_HEADS = 8
SWA_KV_HEADS = 2
SWA_WINDOW = 128
SWA_BLOCK = 128
S5_WIDTH = 512
S5_GROUP = 16
S5_GROUPS = S5_WIDTH // S5_GROUP
S5_STATE = 64
S5_DT_MIN = 0.001
S5_DT_MAX = 0.1
MLA_HEADS = 8
MLA_Q_LORA = 256
MLA_KV_LORA = 128
MLA_NOPE = 64
MLA_ROPE = 32
MLA_V = 64
MLA_BLOCK = 128
D_FF = 2816
MACARON_WEIGHT = 0.5
ROPE_BASE = 10000.0
EPS = 1e-6
N_MOD = 9
IN_SPLITS = (NA_HEADS * HEAD_DIM, NA_HEADS * HEAD_DIM, NA_HEADS * HEAD_DIM,
             SWA_HEADS * HEAD_DIM, SWA_KV_HEADS * HEAD_DIM, SWA_KV_HEADS * HEAD_DIM,
             S5_WIDTH, MLA_Q_LORA, MLA_KV_LORA, MLA_ROPE, N_BRANCH * D_MODEL)
IN_COLS = sum(IN_SPLITS)

kernel_name = 'hybrid_na_swa_s5_mla_diffusion_block'


def rms_norm(x, w):
    xf = x.astype(jnp.float32)
    y = xf * lax.rsqrt(jnp.mean(xf * xf, axis=-1, keepdims=True) + EPS)
    return (y * w.astype(jnp.float32)).astype(x.dtype)


def modulate(x, shift, scale):
    return x * (1.0 + scale) + shift


def swiglu(x, w_gate, w_up, w_down):
    return (jax.nn.silu(x @ w_gate) * (x @ w_up)) @ w_down


def macaron_half_ffn(h, shift, scale, gate, norm_w, w_gate, w_up, w_down):
    n = modulate(rms_norm(h, norm_w), shift, scale)
    return h + MACARON_WEIGHT * gate * swiglu(n, w_gate, w_up, w_down)


def split_heads(t, n_heads):
    return t.reshape(t.shape[:-1] + (n_heads, t.shape[-1] // n_heads))


def split_columns(t):
    parts, off = [], 0
    for width in IN_SPLITS:
        parts.append(t[..., off:off + width])
        off += width
    return parts


def rope_2d(x, rows, cols):
    dim = x.shape[-1]
    half_axis = dim // 2
    n_freq = half_axis // 2
    inv_freq = ROPE_BASE ** (-jnp.arange(n_freq, dtype=jnp.float32) / n_freq)

    def rotate(xa, pos):
        ang = pos.astype(jnp.float32)[:, None] * inv_freq[None, :]
        cos = jnp.cos(ang)[None, :, None, :]
        sin = jnp.sin(ang)[None, :, None, :]
        x1, x2 = xa[..., :n_freq], xa[..., n_freq:]
        return jnp.concatenate([x1 * cos - x2 * sin, x2 * cos + x1 * sin], axis=-1)

    xf = x.astype(jnp.float32)
    out = jnp.concatenate([rotate(xf[..., :half_axis], rows), rotate(xf[..., half_axis:], cols)], axis=-1)
    return out.astype(x.dtype)


def softmax_with_sink(s, sink):
    m = jnp.maximum(jnp.max(s, axis=-1, keepdims=True), sink)
    e = jnp.exp(s - m)
    return e / (jnp.sum(e, axis=-1, keepdims=True) + jnp.exp(sink - m))


def ctx_attention(q, k, v, sink=None):
    B, C, Hq, dq = q.shape
    Hk = k.shape[2]
    G = Hq // Hk
    qg = q.reshape(B, C, Hk, G, dq)
    s = jnp.einsum('bqkgd,bckd->bkgqc', qg, k).astype(jnp.float32) * dq ** -0.5
    if sink is None:
        p = jax.nn.softmax(s, axis=-1)
    else:
        p = softmax_with_sink(s, sink.astype(jnp.float32).reshape(Hk, G)[None, :, :, None, None])
    o = jnp.einsum('bkgqc,bckd->bqkgd', p.astype(v.dtype), v)
    return o.reshape(B, C, Hq * v.shape[-1])


def neighbourhood_attention(q, k, v, kc, vc, rpb):
    B, T, H, d = q.shape
    rows_n = T // GRID_W
    kr = min(NA_WIN_ROWS, rows_n)
    n_loc = kr * NA_WIN_COLS
    scale = d ** -0.5
    qg = q.reshape(B, rows_n, GRID_W, H, d)
    kg = k.reshape(B, rows_n, GRID_W, H, d)
    vg = v.reshape(B, rows_n, GRID_W, H, d)
    col = jnp.arange(GRID_W)
    c0 = jnp.clip(col - NA_WIN_COLS // 2, 0, GRID_W - NA_WIN_COLS)
    col_idx = c0[:, None] + jnp.arange(NA_WIN_COLS)[None, :]
    col_bias_idx = col_idx - col[:, None] + (NA_WIN_COLS - 1)
    rpb32 = rpb.astype(jnp.float32)

    def one_row(args):
        r, q_r = args
        r0 = jnp.clip(r - kr // 2, 0, rows_n - kr)
        k_win = lax.dynamic_slice_in_dim(kg, r0, kr, axis=1)[:, :, col_idx]
        v_win = lax.dynamic_slice_in_dim(vg, r0, kr, axis=1)[:, :, col_idx]
        row_bias_idx = r0 + jnp.arange(kr) - r + (NA_WIN_ROWS - 1)
        bias = rpb32[:, row_bias_idx[None, :, None], col_bias_idx[:, None, :]]
        s_loc = jnp.einsum('bqhd,brqjhd->bhqrj', q_r, k_win).astype(jnp.float32) * scale + bias[None]
        s_ctx = jnp.einsum('bqhd,bchd->bhqc', q_r, kc).astype(jnp.float32) * scale
        p = jax.nn.softmax(jnp.concatenate([s_loc.reshape(B, H, GRID_W, n_loc), s_ctx], axis=-1), axis=-1).astype(v.dtype)
        p_loc = p[..., :n_loc].reshape(B, H, GRID_W, kr, NA_WIN_COLS)
        return (jnp.einsum('bhqrj,brqjhd->bqhd', p_loc, v_win)
                + jnp.einsum('bhqc,bchd->bqhd', p[..., n_loc:], vc))

    out = lax.map(one_row, (jnp.arange(rows_n), jnp.moveaxis(qg, 1, 0)))
    return jnp.moveaxis(out, 0, 1).reshape(B, T, H * d)


def sliding_window_attention(q, k, v, kc, vc, sink):
    B, T, Hq, d = q.shape
    Hk = k.shape[2]
    G = Hq // Hk
    blk = SWA_BLOCK
    nb = T // blk
    scale = d ** -0.5
    qb = q.reshape(B, nb, blk, Hk, G, d)
    pad = ((0, 0), (blk, blk), (0, 0), (0, 0))
    kp = jnp.pad(k, pad).reshape(B, nb + 2, blk, Hk, d)
    vp = jnp.pad(v, pad).reshape(B, nb + 2, blk, Hk, d)
    kband = jnp.concatenate([kp[:, :-2], kp[:, 1:-1], kp[:, 2:]], axis=2)
    vband = jnp.concatenate([vp[:, :-2], vp[:, 1:-1], vp[:, 2:]], axis=2)
    qpos = jnp.arange(nb)[:, None] * blk + jnp.arange(blk)[None, :]
    kpos = (jnp.arange(nb)[:, None] - 1) * blk + jnp.arange(3 * blk)[None, :]
    valid = ((jnp.abs(qpos[:, :, None] - kpos[:, None, :]) <= SWA_WINDOW)
             & (kpos[:, None, :] >= 0) & (kpos[:, None, :] < T))
    s_loc = jnp.einsum('bnqkgd,bnckd->bnkgqc', qb, kband).astype(jnp.float32) * scale
    s_loc = jnp.where(valid[None, :, None, None], s_loc, -jnp.inf)
    s_ctx = jnp.einsum('bnqkgd,bckd->bnkgqc', qb, kc).astype(jnp.float32) * scale
    sink_b = sink.astype(jnp.float32).reshape(Hk, G)[None, None, :, :, None, None]
    p = softmax_with_sink(jnp.concatenate([s_loc, s_ctx], axis=-1), sink_b).astype(v.dtype)
    n_loc = 3 * blk
    o = (jnp.einsum('bnkgqc,bnckd->bnqkgd', p[..., :n_loc], vband)
         + jnp.einsum('bnkgqc,bckd->bnqkgd', p[..., n_loc:], vc))
    return o.reshape(B, T, Hq * d)


def mla_query(cq, p, rows, cols):
    q = split_heads(rms_norm(cq, p['mla_q_norm']) @ p['mla_w_uq'], MLA_HEADS)
    q_nope, q_rope = q[..., :MLA_NOPE], q[..., MLA_NOPE:]
    if rows is not None:
        q_rope = rope_2d(q_rope, rows, cols)
    return jnp.concatenate([q_nope, q_rope], axis=-1)


def mla_keys_values(ckv, k_rope, p, rows, cols):
    kv = split_heads(rms_norm(ckv, p['mla_kv_norm']) @ p['mla_w_ukv'], MLA_HEADS)
    k_nope, v = kv[..., :MLA_NOPE], kv[..., MLA_NOPE:]
    kr = k_rope[..., None, :]
    if rows is not None:
        kr = rope_2d(kr, rows, cols)
    k = jnp.concatenate([k_nope, jnp.broadcast_to(kr, k_nope.shape[:-1] + (MLA_ROPE,))], axis=-1)
    return k, v


def mla_dense_attention(q, k, v, kc, vc):
    B, T, H, dq = q.shape
    dv = v.shape[-1]
    scale = dq ** -0.5
    k_all = jnp.concatenate([k, kc], axis=1)
    v_all = jnp.concatenate([v, vc], axis=1)
    nb = T // MLA_BLOCK
    qb = jnp.moveaxis(q.reshape(B, nb, MLA_BLOCK, H, dq), 1, 0)

    def one_block(q_blk):
        s = jnp.einsum('bqhd,bkhd->bhqk', q_blk, k_all).astype(jnp.float32) * scale
        pr = jax.nn.softmax(s, axis=-1).astype(v.dtype)
        return jnp.einsum('bhqk,bkhd->bqhd', pr, v_all)

    o = lax.map(one_block, qb)
    return jnp.moveaxis(o, 0, 1).reshape(B, T, H * dv)


def linear_recurrence(e1, e2):
    a1, b1 = e1
    a2, b2 = e2
    return a1 * a2, a2 * b1 + b2


def s5_discretise(lam_re, lam_im, log_dt, b_re, b_im, c_re, c_im):
    lam = lax.complex(lam_re.astype(jnp.float32), lam_im.astype(jnp.float32))
    dt = jnp.exp(log_dt.astype(jnp.float32))[:, None]
    lam_bar = jnp.exp(lam * dt)
    b = lax.complex(b_re.astype(jnp.float32), b_im.astype(jnp.float32))
    b_bar = ((lam_bar - 1.0) / lam)[..., None] * b
    c = lax.complex(c_re.astype(jnp.float32), c_im.astype(jnp.float32))
    return lam_bar, b_bar, c


def s5_states(u, lam_bar, b_bar, h0, reverse):
    bu = jnp.einsum('gpc,btgc->btgp', b_bar, u.astype(jnp.complex64))
    if h0 is not None:
        first = -1 if reverse else 0
        bu = bu.at[:, first].add(lam_bar * h0)
    a = jnp.broadcast_to(lam_bar, bu.shape)
    _, h = lax.associative_scan(linear_recurrence, (a, bu), reverse=reverse, axis=1)
    return h


def s5_readout(c, h):
    B, T = h.shape[0], h.shape[1]
    return jnp.einsum('gcp,btgp->btgc', c, h).real.reshape(B, T, S5_WIDTH)


def s5_glu(y, w, b):
    g = jax.nn.gelu(y)
    return g * jax.nn.sigmoid(g @ w + b)


def s5_mixer(u_lat, u_ctx, p, need_ctx):
    B, T, _ = u_lat.shape
    C = u_ctx.shape[1]
    ul = u_lat.astype(jnp.float32).reshape(B, T, S5_GROUPS, S5_GROUP)
    uc = u_ctx.astype(jnp.float32).reshape(B, C, S5_GROUPS, S5_GROUP)
    d_skip = p['s5_d'].astype(jnp.float32)
    y_lat = d_skip * u_lat.astype(jnp.float32)
    y_ctx = d_skip * u_ctx.astype(jnp.float32) if need_ctx else None
    for direction in range(2):
        reverse = direction == 1
        lam_bar, b_bar, c = s5_discretise(
            p['s5_lambda_re'][direction], p['s5_lambda_im'][direction], p['s5_log_dt'][direction],
            p['s5_b_re'][direction], p['s5_b_im'][direction], p['s5_c_re'][direction], p['s5_c_im'][direction])
        hc = s5_states(uc, lam_bar, b_bar, None, reverse)
        h_final = hc[:, 0] if reverse else hc[:, -1]
        hl = s5_states(ul, lam_bar, b_bar, h_final, reverse)
        y_lat = y_lat + s5_readout(c, hl)
        if need_ctx:
            y_ctx = y_ctx + s5_readout(c, hc)
    out_lat = s5_glu(y_lat.astype(u_lat.dtype), p['s5_glu_w'], p['s5_glu_b'])
    out_ctx = s5_glu(y_ctx.astype(u_ctx.dtype), p['s5_glu_w'], p['s5_glu_b']) if need_ctx else None
    return out_lat, out_ctx


def merge_branches(ys, gates, w_branch, w_out):
    g = jax.nn.sigmoid(gates.reshape(gates.shape[:-1] + (N_BRANCH, gates.shape[-1] // N_BRANCH)))
    proj = jnp.einsum('btnw,nwd->btnd', jnp.stack(ys, axis=-2), w_branch)
    return jnp.sum(g * proj, axis=-2) @ w_out


def hybrid_token_mixer(n_lat, n_ctx, p, rows, cols, need_ctx):
    (na_q, na_k, na_v, sw_q, sw_k, sw_v, s5_u, mla_cq, mla_ckv, mla_kr, gates) = split_columns(n_lat @ p['w_in'])
    (na_qc, na_kc, na_vc, sw_qc, sw_kc, sw_vc, s5_uc, mla_cqc, mla_ckvc, mla_krc, gates_c) = split_columns(n_ctx @ p['w_in'])
    na_kc_h, na_vc_h = split_heads(na_kc, NA_HEADS), split_heads(na_vc, NA_HEADS)
    y_na = neighbourhood_attention(split_heads(na_q, NA_HEADS), split_heads(na_k, NA_HEADS),
                                   split_heads(na_v, NA_HEADS), na_kc_h, na_vc_h, p['na_rpb'])
    sw_kc_h, sw_vc_h = split_heads(sw_kc, SWA_KV_HEADS), split_heads(sw_vc, SWA_KV_HEADS)
    y_sw = sliding_window_attention(rope_2d(split_heads(sw_q, SWA_HEADS), rows, cols),
                                    rope_2d(split_heads(sw_k, SWA_KV_HEADS), rows, cols),
                                    split_heads(sw_v, SWA_KV_HEADS), sw_kc_h, sw_vc_h, p['swa_sink'])
    y_s5, y_s5_c = s5_mixer(s5_u, s5_uc, p, need_ctx)
    k_mc, v_mc = mla_keys_values(mla_ckvc, mla_krc, p, None, None)
    k_m, v_m = mla_keys_values(mla_ckv, mla_kr, p, rows, cols)
    y_mla = mla_dense_attention(mla_query(mla_cq, p, rows, cols), k_m, v_m, k_mc, v_mc)
    out_lat = merge_branches([y_na, y_sw, y_s5, y_mla], gates, p['w_branch'], p['w_out'])
    if not need_ctx:
        return out_lat, None
    y_na_c = ctx_attention(split_heads(na_qc, NA_HEADS), na_kc_h, na_vc_h)
    y_sw_c = ctx_attention(split_heads(sw_qc, SWA_HEADS), sw_kc_h, sw_vc_h, p['swa_sink'])
    y_mla_c = ctx_attention(mla_query(mla_cqc, p, None, None), k_mc, v_mc)
    out_ctx = merge_branches([y_na_c, y_sw_c, y_s5_c, y_mla_c], gates_c, p['w_branch'], p['w_out'])
    return out_lat, out_ctx


def setup_inputs(seed: int = 0) -> dict:
    key = jax.random.key(seed)
    ks = iter(jax.random.split(key, 48))

    def nrm(shape, scale=1.0):
        return jax.random.normal(next(ks), shape, jnp.float32) * scale

    def gain(shape):
        return 1.0 + nrm(shape, 0.01)

    D = D_MODEL
    G, P, Cg = S5_GROUPS, S5_STATE, S5_GROUP
    lam_im = jnp.pi * jnp.arange(P, dtype=jnp.float32) + nrm((DEPTH, 2, G, P), 0.01)
    log_dt = jax.random.uniform(next(ks), (DEPTH, 2, G), jnp.float32,
                                math.log(S5_DT_MIN), math.log(S5_DT_MAX))
    return {
        'x': nrm((BATCH, SEQ, D)),
        'c': nrm((BATCH, D)),
        'ctx': nrm((BATCH, CTX_LEN, D)),
        'c_ctx': nrm((D,)),
        'ada_w': nrm((DEPTH, D, N_MOD * D), 0.5 * D ** -0.5),
        'ada_b': nrm((DEPTH, N_MOD * D), 0.02),
        'ffn1_norm': gain((DEPTH, D)),
        'ffn1_w_gate': nrm((DEPTH, D, D_FF), D ** -0.5),
        'ffn1_w_up': nrm((DEPTH, D, D_FF), D ** -0.5),
        'ffn1_w_down': nrm((DEPTH, D_FF, D), D_FF ** -0.5),
        'mix_norm': gain((DEPTH, D)),
        'w_in': nrm((DEPTH, D, IN_COLS), D ** -0.5),
        'na_rpb': nrm((DEPTH, NA_HEADS, 2 * NA_WIN_ROWS - 1, 2 * NA_WIN_COLS - 1), 0.02),
        'swa_sink': nrm((DEPTH, SWA_HEADS), 0.5),
        's5_lambda_re': -0.5 + nrm((DEPTH, 2, G, P), 0.01),
        's5_lambda_im': lam_im,
        's5_log_dt': log_dt,
        's5_b_re': nrm((DEPTH, 2, G, P, Cg), (2 * Cg) ** -0.5),
        's5_b_im': nrm((DEPTH, 2, G, P, Cg), (2 * Cg) ** -0.5),
        's5_c_re': nrm((DEPTH, 2, G, Cg, P), P ** -0.5),
        's5_c_im': nrm((DEPTH, 2, G, Cg, P), P ** -0.5),
        's5_d': nrm((DEPTH, S5_WIDTH)),
        's5_glu_w': nrm((DEPTH, S5_WIDTH, S5_WIDTH), S5_WIDTH ** -0.5),
        's5_glu_b': nrm((DEPTH, S5_WIDTH), 0.02),
        'mla_q_norm': gain((DEPTH, MLA_Q_LORA)),
        'mla_w_uq': nrm((DEPTH, MLA_Q_LORA, MLA_HEADS * (MLA_NOPE + MLA_ROPE)), MLA_Q_LORA ** -0.5),
        'mla_kv_norm': gain((DEPTH, MLA_KV_LORA)),
        'mla_w_ukv': nrm((DEPTH, MLA_KV_LORA, MLA_HEADS * (MLA_NOPE + MLA_V)), MLA_KV_LORA ** -0.5),
        'w_branch': nrm((DEPTH, N_BRANCH, BRANCH_WIDTH, D), BRANCH_WIDTH ** -0.5),
        'w_out': nrm((DEPTH, D, D), D ** -0.5),
        'ffn2_norm': gain((DEPTH, D)),
        'ffn2_w_gate': nrm((DEPTH, D, D_FF), D ** -0.5),
        'ffn2_w_up': nrm((DEPTH, D, D_FF), D ** -0.5),
        'ffn2_w_down': nrm((DEPTH, D_FF, D), D_FF ** -0.5),
        'final_norm': gain((D,)),
    }


def reference(x, c, ctx, c_ctx, ada_w, ada_b, ffn1_norm, ffn1_w_gate, ffn1_w_up, ffn1_w_down,
              mix_norm, w_in, na_rpb, swa_sink, s5_lambda_re, s5_lambda_im, s5_log_dt,
              s5_b_re, s5_b_im, s5_c_re, s5_c_im, s5_d, s5_glu_w, s5_glu_b,
              mla_q_norm, mla_w_uq, mla_kv_norm, mla_w_ukv, w_branch, w_out,
              ffn2_norm, ffn2_w_gate, ffn2_w_up, ffn2_w_down, final_norm):
    B, T, D = x.shape
    pos = jnp.arange(T)
    rows = pos // GRID_W
    cols = pos % GRID_W
    h, hc = x, ctx
    silu_c, silu_cc = jax.nn.silu(c), jax.nn.silu(c_ctx)
    for l in range(DEPTH):
        need_ctx = l < DEPTH - 1
        mod = (silu_c @ ada_w[l] + ada_b[l]).reshape(B, N_MOD, 1, D)
        modc = (silu_cc @ ada_w[l] + ada_b[l]).reshape(N_MOD, D)
        h = macaron_half_ffn(h, mod[:, 0], mod[:, 1], mod[:, 2], ffn1_norm[l],
                             ffn1_w_gate[l], ffn1_w_up[l], ffn1_w_down[l])
        hc = macaron_half_ffn(hc, modc[0], modc[1], modc[2], ffn1_norm[l],
                              ffn1_w_gate[l], ffn1_w_up[l], ffn1_w_down[l])
        p = {'w_in': w_in[l], 'na_rpb': na_rpb[l], 'swa_sink': swa_sink[l],
             's5_lambda_re': s5_lambda_re[l], 's5_lambda_im': s5_lambda_im[l], 's5_log_dt': s5_log_dt[l],
             's5_b_re': s5_b_re[l], 's5_b_im': s5_b_im[l], 's5_c_re': s5_c_re[l], 's5_c_im': s5_c_im[l],
             's5_d': s5_d[l], 's5_glu_w': s5_glu_w[l], 's5_glu_b': s5_glu_b[l],
             'mla_q_norm': mla_q_norm[l], 'mla_w_uq': mla_w_uq[l],
             'mla_kv_norm': mla_kv_norm[l], 'mla_w_ukv': mla_w_ukv[l],
             'w_branch': w_branch[l], 'w_out': w_out[l]}
        n_lat = modulate(rms_norm(h, mix_norm[l]), mod[:, 3], mod[:, 4])
        n_ctx = modulate(rms_norm(hc, mix_norm[l]), modc[3], modc[4])
        y_lat, y_ctx = hybrid_token_mixer(n_lat, n_ctx, p, rows, cols, need_ctx)
        h = h + mod[:, 5] * y_lat
        h = macaron_half_ffn(h, mod[:, 6], mod[:, 7], mod[:, 8], ffn2_norm[l],
                             ffn2_w_gate[l], ffn2_w_up[l], ffn2_w_down[l])
        if need_ctx:
            hc = hc + modc[5] * y_ctx
            hc = macaron_half_ffn(hc, modc[6], modc[7], modc[8], ffn2_norm[l],
                                  ffn2_w_gate[l], ffn2_w_up[l], ffn2_w_down[l])
    return rms_norm(h, final_norm)
```

```python
import functools
import math

import numpy as np
import jax
import jax.numpy as jnp
from jax import lax
from jax.experimental import pallas as pl
from jax.experimental.pallas import tpu as pltpu

F32 = jnp.float32
BF16 = jnp.bfloat16

GRID_W = 64
HEAD_DIM = 64
N_BRANCH = 4
NA_HEADS = 8
NA_WIN_ROWS = 8
NA_WIN_COLS = 16
SWA_HEADS = 8
SWA_KV_HEADS = 2
SWA_WINDOW = 128
S5_GROUP = 16
S5_STATE = 64
MLA_HEADS = 8
MLA_NOPE = 64
MLA_ROPE = 32
MLA_V = 64
MACARON_WEIGHT = 0.5
ROPE_BASE = 10000.0
EPS = 1e-6
N_MOD = 9

LANES = 128
SUBLANES = 8
TM = 512
FF_CHUNK = 256
NA_QROWS = 4
NA_KROWS = NA_QROWS + NA_WIN_ROWS - 1
SWA_BLK = 128
MLA_TQ = 256
S5_STEPS = 64
NEG = -1e30
VMEM_LIMIT = 56 * 1024 * 1024


def _cparams(sem):
    return pltpu.CompilerParams(dimension_semantics=sem, vmem_limit_bytes=VMEM_LIMIT)


def _const_spec(shape):
    nd = len(shape)
    return pl.BlockSpec(shape, lambda *_: (0,) * nd, pipeline_mode=pl.Buffered(1))


def _dot(a, b):
    return jnp.dot(a, b, preferred_element_type=F32)


def _dot_t(a, b):
    return lax.dot_general(a, b, (((1,), (1,)), ((), ())), preferred_element_type=F32)


def _rms(x, w):
    return x * lax.rsqrt(jnp.mean(x * x, axis=-1, keepdims=True) + EPS) * w


def _rms_mod(x, w, shift, scale):
    return _rms(x, w) * (1.0 + scale) + shift


def _mod_row_map(n_lat_tiles, tiles_per_batch, n_batch):
    def f(i):
        return jnp.where(i < n_lat_tiles, i // tiles_per_batch, n_batch)
    return f


def _ada_kernel(c_ref, w_ref, b_ref, o_ref):
    c = c_ref[...]
    s = c * jax.nn.sigmoid(c)
    o_ref[0] = jnp.dot(s, w_ref[0], preferred_element_type=F32,
                       precision=lax.Precision.HIGHEST) + b_ref[0]


def _ada_mod(cc, ada_w, ada_b):
    depth, d, nd = ada_w.shape
    tn = 1024
    out = pl.pallas_call(
        _ada_kernel,
        out_shape=jax.ShapeDtypeStruct((depth, SUBLANES, nd), F32),
        grid=(depth, nd // tn),
        in_specs=[pl.BlockSpec((SUBLANES, d), lambda l, j: (0, 0)),
                  pl.BlockSpec((1, d, tn), lambda l, j: (l, 0, j)),
                  pl.BlockSpec((1, 1, tn), lambda l, j: (l, 0, j))],
        out_specs=pl.BlockSpec((1, SUBLANES, tn), lambda l, j: (l, 0, j)),
        compiler_params=_cparams(("arbitrary", "arbitrary")),
        name="ada_mod",
    )(cc, ada_w, ada_b.reshape(depth, 1, nd))
    return out.reshape(depth, SUBLANES, N_MOD, d)


def _ffn_kernel(x_ref, mod_ref, nw_ref, wg_ref, wu_ref, wd_ref, fw_ref, o_ref, acc_ref, *, base, final):
    x = x_ref[...]
    m = mod_ref[0]
    n = _rms_mod(x, nw_ref[...], m[base:base + 1], m[base + 1:base + 2]).astype(BF16)
    acc_ref[...] = jnp.zeros_like(acc_ref)

    def body(c, carry):
        g = _dot(n, wg_ref[c])
        u = _dot(n, wu_ref[c])
        a = (g * jax.nn.sigmoid(g) * u).astype(BF16)
        acc_ref[...] += _dot(a, wd_ref[c])
        return carry

    lax.fori_loop(0, wg_ref.shape[0], body, 0)
    out = x + MACARON_WEIGHT * m[base + 2:base + 3] * acc_ref[...]
    if final:
        out = _rms(out, fw_ref[...])
    o_ref[...] = out


def _ffn(h, n_rows, mod, norm_w, wg, wu, wd, final_w, *, base, final, n_lat_tiles, tiles_per_batch, n_batch):
    d = h.shape[1]
    nch, _, fc = wg.shape
    kern = functools.partial(_ffn_kernel, base=base, final=final)
    return pl.pallas_call(
        kern,
        out_shape=jax.ShapeDtypeStruct((n_rows, d), F32),
        grid=(n_rows // TM,),
        in_specs=[pl.BlockSpec((TM, d), lambda i: (i, 0)),
                  pl.BlockSpec((1, N_MOD, d), lambda i: (_mod_row_map(n_lat_tiles, tiles_per_batch, n_batch)(i), 0, 0)),
                  _const_spec((1, d)),
                  _const_spec((nch, d, fc)), _const_spec((nch, d, fc)), _const_spec((nch, fc, d)),
                  _const_spec((1, d))],
        out_specs=pl.BlockSpec((TM, d), lambda i: (i, 0)),
        scratch_shapes=[pltpu.VMEM((TM, d), F32)],
        compiler_params=_cparams(("arbitrary",)),
        name="ffn_final" if final else "ffn",
    )(h, mod, norm_w, wg, wu, wd, final_w)


def _inproj_kernel(x_ref, mod_ref, nw_ref, wa_ref, wr_ref, wrs_ref, wv_ref, wf_ref, wk_ref, wks_ref,
                   cs_ref, sn_ref, ck_ref, sk_ref, ob_ref, of_ref):
    x = x_ref[...]
    m = mod_ref[0]
    n = _rms_mod(x, nw_ref[...], m[3:4], m[4:5]).astype(BF16)
    na = wa_ref.shape[1]
    nr = wr_ref.shape[1]
    nv = wv_ref.shape[1]
    nf = wf_ref.shape[1]
    cw = 2 * LANES
    for c in range(na // cw):
        ob_ref[:, c * cw:(c + 1) * cw] = _dot(n, wa_ref[:, c * cw:(c + 1) * cw]).astype(BF16)
    cs = cs_ref[...]
    sn = sn_ref[...]
    for c in range(nr // cw):
        y = _dot(n, wr_ref[:, c * cw:(c + 1) * cw])
        ys = _dot(n, wrs_ref[:, c * cw:(c + 1) * cw])
        ob_ref[:, na + c * cw:na + (c + 1) * cw] = (y * cs + ys * sn).astype(BF16)
    for c in range(nv // cw):
        ob_ref[:, na + nr + c * cw:na + nr + (c + 1) * cw] = _dot(n, wv_ref[:, c * cw:(c + 1) * cw]).astype(BF16)
    for c in range(nf // LANES):
        of_ref[:, c * LANES:(c + 1) * LANES] = _dot(n, wf_ref[:, c * LANES:(c + 1) * LANES])
    of_ref[:, nf:nf + LANES] = _dot(n, wk_ref[...]) * ck_ref[...] + _dot(n, wks_ref[...]) * sk_ref[...]


def _inproj(h, mod, norm_w, w, tabs, *, n_lat_tiles, tiles_per_batch, n_batch):
    n_rows, d = h.shape
    wa, wr, wrs, wv, wf, wk, wks = w
    cs, sn, ck, sk = tabs
    nb = wa.shape[1] + wr.shape[1] + wv.shape[1]
    nf = wf.shape[1] + LANES
    tab_map = lambda i: (jnp.where(i < n_lat_tiles, i % tiles_per_batch, tiles_per_batch), 0)
    return pl.pallas_call(
        _inproj_kernel,
        out_shape=(jax.ShapeDtypeStruct((n_rows, nb), BF16), jax.ShapeDtypeStruct((n_rows, nf), F32)),
        grid=(n_rows // TM,),
        in_specs=[pl.BlockSpec((TM, d), lambda i: (i, 0)),
                  pl.BlockSpec((1, N_MOD, d), lambda i: (_mod_row_map(n_lat_tiles, tiles_per_batch, n_batch)(i), 0, 0)),
                  _const_spec((1, d))]
                 + [_const_spec(a.shape) for a in w]
                 + [pl.BlockSpec((TM, 2 * LANES), tab_map), pl.BlockSpec((TM, 2 * LANES), tab_map),
                    pl.BlockSpec((TM, LANES), tab_map), pl.BlockSpec((TM, LANES), tab_map)],
        out_specs=(pl.BlockSpec((TM, nb), lambda i: (i, 0)), pl.BlockSpec((TM, nf), lambda i: (i, 0))),
        compiler_params=_cparams(("arbitrary",)),
        name="inproj",
    )(h, mod, norm_w, *w, cs, sn, ck, sk)


def _lane_half_masks(rows):
    lane = lax.broadcasted_iota(jnp.int32, (rows, LANES), 1)
    return lane < HEAD_DIM


def _softmax2(s1, s2, sink=None):
    m = jnp.maximum(jnp.max(s1, axis=-1, keepdims=True), jnp.max(s2, axis=-1, keepdims=True))
    if sink is not None:
        m = jnp.maximum(m, sink)
    p1 = jnp.exp(s1 - m)
    p2 = jnp.exp(s2 - m)
    l = jnp.sum(p1, axis=-1, keepdims=True) + jnp.sum(p2, axis=-1, keepdims=True)
    if sink is not None:
        l = l + jnp.exp(sink - m)
    return p1, p2, l


def _na_kernel(q_ref, k_ref, v_ref, kc_ref, vc_ref, bias_ref, o_ref, *, n_blk, rows_n):
    blk = pl.program_id(1)
    nq = NA_QROWS * GRID_W
    nk = NA_KROWS * GRID_W
    ws = jnp.clip(NA_QROWS * blk - NA_WIN_ROWS // 2, 0, rows_n - NA_KROWS)
    r0 = pl.multiple_of(ws * GRID_W, GRID_W)
    lo = _lane_half_masks(nq)
    for j in range(NA_HEADS // 2):
        sl = slice(j * LANES, (j + 1) * LANES)
        qp = q_ref[:, sl]
        kw = k_ref[pl.ds(r0, nk), sl]
        vw = v_ref[pl.ds(r0, nk), sl]
        kc = kc_ref[:, sl]
        vc = vc_ref[:, sl]
        outs = []
        for half in range(2):
            qm = jnp.where(lo if half == 0 else jnp.logical_not(lo), qp, jnp.zeros_like(qp))
            s1 = _dot_t(qm, kw) + bias_ref[0, 2 * j + half]
            s2 = _dot_t(qm, kc)
            p1, p2, l = _softmax2(s1, s2)
            o = _dot(p1.astype(BF16), vw) + _dot(p2.astype(BF16), vc)
            outs.append(o / l)
        o_ref[:, sl] = jnp.where(lo, outs[0], outs[1]).astype(BF16)


def _na_attention(pb, bias, *, n_batch, seq, ctx, need_ctx):
    rows_n = seq // GRID_W
    n_blk = rows_n // NA_QROWS
    nq = NA_QROWS * GRID_W
    hw = NA_HEADS * HEAD_DIM
    n_q = n_blk + (1 if need_ctx else 0)
    assert ctx == nq
    lat_blocks = n_batch * seq // nq
    qmap = lambda b, i: (jnp.where(i < n_blk, b * n_blk + i, lat_blocks + b), 0)

    def bias_map(b, i):
        t = jnp.where(i == 0, 0, jnp.where(i == 1, 1, jnp.where(i == n_blk - 1, 3, jnp.where(i == n_blk, 4, 2))))
        return (t, 0, 0, 0)

    kern = functools.partial(_na_kernel, n_blk=n_blk, rows_n=rows_n)
    return pl.pallas_call(
        kern,
        out_shape=jax.ShapeDtypeStruct((pb.shape[0], hw), BF16),
        grid=(n_batch, n_q),
        in_specs=[pl.BlockSpec((nq, hw), qmap),
                  pl.BlockSpec((seq, hw), lambda b, i: (b, 1)),
                  pl.BlockSpec((seq, hw), lambda b, i: (b, 2)),
                  pl.BlockSpec((ctx, hw), lambda b, i: (n_batch * seq // ctx + b, 1)),
                  pl.BlockSpec((ctx, hw), lambda b, i: (n_batch * seq // ctx + b, 2)),
                  pl.BlockSpec((1, NA_HEADS, nq, NA_KROWS * GRID_W), bias_map)],
        out_specs=pl.BlockSpec((nq, hw), qmap),
        compiler_params=_cparams(("arbitrary", "arbitrary")),
        name="na_attn",
    )(pb, pb, pb, pb, pb, bias)


def _na_bias_table(rpb, rows_n):
    n_blk = rows_n // NA_QROWS
    kr_n = min(NA_WIN_ROWS, rows_n)
    tabs = []
    for blk in (0, 1, 2, n_blk - 1):
        ws = int(np.clip(NA_QROWS * blk - NA_WIN_ROWS // 2, 0, rows_n - NA_KROWS))
        r = NA_QROWS * blk + np.arange(NA_QROWS)
        r0 = np.clip(r - kr_n // 2, 0, rows_n - kr_n)
        kr = ws + np.arange(NA_KROWS)
        col = np.arange(GRID_W)
        c0 = np.clip(col - NA_WIN_COLS // 2, 0, GRID_W - NA_WIN_COLS)
        row_ok = (kr[None, :] >= r0[:, None]) & (kr[None, :] < r0[:, None] + kr_n)
        col_ok = (col[None, :] >= c0[:, None]) & (col[None, :] < c0[:, None] + NA_WIN_COLS)
        dr = np.clip(kr[None, :] - r[:, None] + NA_WIN_ROWS - 1, 0, 2 * NA_WIN_ROWS - 2)
        dc = np.clip(col[None, :] - col[:, None] + NA_WIN_COLS - 1, 0, 2 * NA_WIN_COLS - 2)
        ok = row_ok[:, None, :, None] & col_ok[None, :, None, :]
        vals = rpb[:, dr[:, None, :, None], dc[None, :, None, :]]
        t = jnp.where(ok[None], vals, NEG)
        tabs.append(t.reshape(rpb.shape[0], NA_QROWS * GRID_W, NA_KROWS * GRID_W))
    tabs.append(jnp.full_like(tabs[0], NEG))
    return jnp.stack(tabs)


def _swa_kernel(sink_ref, q_ref, k_ref, v_ref, qc_ref, kc_ref, vc_ref, o_ref, oc_ref, *, seq, need_ctx):
    n_blk = seq // SWA_BLK
    band = 3 * SWA_BLK
    group = SWA_HEADS // SWA_KV_HEADS
    lo = _lane_half_masks(SWA_BLK)
    qi = lax.broadcasted_iota(jnp.int32, (SWA_BLK, band), 0)
    ki = lax.broadcasted_iota(jnp.int32, (SWA_BLK, band), 1)

    def head_attn(qp, half, h, kb, vb, kc, vc, mask_bias, lo_m):
        qm = jnp.where(lo_m if half == 0 else jnp.logical_not(lo_m), qp, jnp.zeros_like(qp))
        s2 = _dot_t(qm, kc)
        sink = sink_ref[h]
        if kb is None:
            m = jnp.maximum(jnp.max(s2, axis=-1, keepdims=True), sink)
            p2 = jnp.exp(s2 - m)
            l = jnp.sum(p2, axis=-1, keepdims=True) + jnp.exp(sink - m)
            return _dot(p2.astype(BF16), vc) / l
        s1 = _dot_t(qm, kb) + mask_bias
        p1, p2, l = _softmax2(s1, s2, sink)
        return (_dot(p1.astype(BF16), vb) + _dot(p2.astype(BF16), vc)) / l

    def blk_body(n, carry):
        start = jnp.clip((n - 1) * SWA_BLK, 0, seq - band)
        start = pl.multiple_of(start, SWA_BLK)
        q0 = pl.multiple_of(n * SWA_BLK, SWA_BLK)
        delta = (start + ki) - (q0 + qi)
        mask_bias = jnp.where(jnp.abs(delta) <= SWA_WINDOW, 0.0, NEG).astype(F32)
        for j in range(SWA_HEADS // 2):
            sl = slice(j * LANES, (j + 1) * LANES)
            kv = (2 * j) // group
            ksl = slice(kv * LANES, (kv + 1) * LANES)
            qp = q_ref[pl.ds(q0, SWA_BLK), sl]
            kb = k_ref[pl.ds(start, band), ksl]
            vb = v_ref[pl.ds(start, band), ksl]
            kc = kc_ref[:, ksl]
            vc = vc_ref[:, ksl]
            o0 = head_attn(qp, 0, 2 * j, kb, vb, kc, vc, mask_bias, lo)
            o1 = head_attn(qp, 1, 2 * j + 1, kb, vb, kc, vc, mask_bias, lo)
            o_ref[pl.ds(q0, SWA_BLK), sl] = jnp.where(lo, o0, o1).astype(BF16)
        return carry

    lax.fori_loop(0, n_blk, blk_body, 0)

    if need_ctx:
        nc = qc_ref.shape[0]
        lo_c = _lane_half_masks(nc)
        for j in range(SWA_HEADS // 2):
            sl = slice(j * LANES, (j + 1) * LANES)
            kv = (2 * j) // group
            ksl = slice(kv * LANES, (kv + 1) * LANES)
            qp = qc_ref[:, sl]
            kc = kc_ref[:, ksl]
            vc = vc_ref[:, ksl]
            o0 = head_attn(qp, 0, 2 * j, None, None, kc, vc, None, lo_c)
            o1 = head_attn(qp, 1, 2 * j + 1, None, None, kc, vc, None, lo_c)
            oc_ref[:, sl] = jnp.where(lo_c, o0, o1).astype(BF16)
    else:
        oc_ref[...] = jnp.zeros_like(oc_ref)


def _swa_attention(pb, sink, *, n_batch, seq, ctx, need_ctx, col0):
    hw = SWA_HEADS * HEAD_DIM
    kw = SWA_KV_HEADS * LANES
    qcol = col0 // hw
    kcol = (col0 + hw) // kw
    vcol = kcol + 1
    cblk = n_batch * seq // ctx
    kern = functools.partial(_swa_kernel, seq=seq, need_ctx=need_ctx)
    return pl.pallas_call(
        kern,
        out_shape=(jax.ShapeDtypeStruct((n_batch * seq, hw), BF16),
                   jax.ShapeDtypeStruct((n_batch * ctx, hw), BF16)),
        grid=(n_batch,),
        in_specs=[pl.BlockSpec(memory_space=pltpu.SMEM),
                  pl.BlockSpec((seq, hw), lambda b: (b, qcol)),
                  pl.BlockSpec((seq, kw), lambda b: (b, kcol)),
                  pl.BlockSpec((seq, kw), lambda b: (b, vcol)),
                  pl.BlockSpec((ctx, hw), lambda b: (cblk + b, qcol)),
                  pl.BlockSpec((ctx, kw), lambda b: (cblk + b, kcol)),
                  pl.BlockSpec((ctx, kw), lambda b: (cblk + b, vcol))],
        out_specs=(pl.BlockSpec((seq, hw), lambda b: (b, 0)),
                   pl.BlockSpec((ctx, hw), lambda b: (b, 0))),
        compiler_params=_cparams(("arbitrary",)),
        name="swa_attn",
    )(sink, pb, pb, pb, pb, pb, pb)


def _mla_prep_kernel(pf_ref, qn_ref, kn_ref, wq_ref, wqs_ref, wk_ref, wv_ref, cq_ref, sq_ref,
                     q_ref, k_ref, v_ref, *, off_cq, off_ckv, off_kr, q_lora, kv_lora):
    cq = pf_ref[:, off_cq:off_cq + q_lora]
    ckv = pf_ref[:, off_ckv:off_ckv + kv_lora]
    kr = pf_ref[:, off_kr:off_kr + LANES]
    cqn = _rms(cq, qn_ref[...]).astype(BF16)
    ckvn = _rms(ckv, kn_ref[...]).astype(BF16)
    cs = cq_ref[...]
    sn = sq_ref[...]
    for h in range(MLA_HEADS):
        sl = slice(h * LANES, (h + 1) * LANES)
        q = _dot(cqn, wq_ref[:, sl]) * cs + _dot(cqn, wqs_ref[:, sl]) * sn
        q_ref[:, sl] = q.astype(BF16)
        k_ref[:, sl] = (_dot(ckvn, wk_ref[:, sl]) + kr).astype(BF16)
    v_ref[...] = _dot(ckvn, wv_ref[...]).astype(BF16)


def _mla_prep(pf, qn, kn, wq, wqs, wk, wv, cq_tab, sq_tab, *, n_lat_tiles, tiles_per_batch, offs):
    n_rows = pf.shape[0]
    off_cq, off_ckv, off_kr = offs
    tab_map = lambda i: (jnp.where(i < n_lat_tiles, i % tiles_per_batch, tiles_per_batch), 0)
    kern = functools.partial(_mla_prep_kernel, off_cq=off_cq, off_ckv=off_ckv, off_kr=off_kr,
                             q_lora=wq.shape[0], kv_lora=wk.shape[0])
    hq = MLA_HEADS * LANES
    hv = MLA_HEADS * MLA_V
    return pl.pallas_call(
        kern,
        out_shape=(jax.ShapeDtypeStruct((n_rows, hq), BF16), jax.ShapeDtypeStruct((n_rows, hq), BF16),
                   jax.ShapeDtypeStruct((n_rows, hv), BF16)),
        grid=(n_rows // TM,),
        in_specs=[pl.BlockSpec((TM, pf.shape[1]), lambda i: (i, 0)),
                  _const_spec(qn.shape), _const_spec(kn.shape),
                  _const_spec(wq.shape), _const_spec(wqs.shape), _const_spec(wk.shape), _const_spec(wv.shape),
                  pl.BlockSpec((TM, LANES), tab_map), pl.BlockSpec((TM, LANES), tab_map)],
        out_specs=(pl.BlockSpec((TM, hq), lambda i: (i, 0)), pl.BlockSpec((TM, hq), lambda i: (i, 0)),
                   pl.BlockSpec((TM, hv), lambda i: (i, 0))),
        compiler_params=_cparams(("arbitrary",)),
        name="mla_prep",
    )(pf, qn, kn, wq, wqs, wk, wv, cq_tab, sq_tab)


def _mla_kernel(q_ref, k_ref, v_ref, kc_ref, vc_ref, o_ref, *, n_qt, scale):
    i = pl.program_id(2)
    lo = _lane_half_masks(q_ref.shape[0])

    def run(with_latent):
        outs = []
        for half in range(2):
            sl = slice(half * LANES, (half + 1) * LANES)
            q = q_ref[:, sl]
            s2 = _dot_t(q, kc_ref[:, sl])
            if with_latent:
                s1 = _dot_t(q, k_ref[:, sl])
                m = jnp.maximum(jnp.max(s1, axis=-1, keepdims=True), jnp.max(s2, axis=-1, keepdims=True))
                p1 = jnp.exp((s1 - m) * scale)
                p2 = jnp.exp((s2 - m) * scale)
                l = jnp.sum(p1, axis=-1, keepdims=True) + jnp.sum(p2, axis=-1, keepdims=True)
                o = _dot(p1.astype(BF16), v_ref[...]) + _dot(p2.astype(BF16), vc_ref[...])
            else:
                m = jnp.max(s2, axis=-1, keepdims=True)
                p2 = jnp.exp((s2 - m) * scale)
                l = jnp.sum(p2, axis=-1, keepdims=True)
                o = _dot(p2.astype(BF16), vc_ref[...])
            outs.append(o / l)
        o_ref[...] = jnp.where(lo, outs[0], outs[1]).astype(BF16)

    @pl.when(i < n_qt)
    def _():
        run(True)

    @pl.when(i >= n_qt)
    def _():
        run(False)


def _mla_attention(qm, km, vm, *, n_batch, seq, ctx, need_ctx):
    n_qt = seq // MLA_TQ
    n_q = n_qt + (1 if need_ctx else 0)
    assert ctx == MLA_TQ
    lat_tiles = n_batch * n_qt
    cblk = n_batch * seq // ctx
    qmap = lambda b, p, i: (jnp.where(i < n_qt, b * n_qt + i, lat_tiles + b), p)
    scale = float((MLA_NOPE + MLA_ROPE) ** -0.5)
    kern = functools.partial(_mla_kernel, n_qt=n_qt, scale=scale)
    return pl.pallas_call(
        kern,
        out_shape=jax.ShapeDtypeStruct((qm.shape[0], MLA_HEADS * MLA_V), BF16),
        grid=(n_batch, MLA_HEADS // 2, n_q),
        in_specs=[pl.BlockSpec((MLA_TQ, 2 * LANES), qmap),
                  pl.BlockSpec((seq, 2 * LANES), lambda b, p, i: (b, p)),
                  pl.BlockSpec((seq, LANES), lambda b, p, i: (b, p)),
                  pl.BlockSpec((ctx, 2 * LANES), lambda b, p, i: (cblk + b, p)),
                  pl.BlockSpec((ctx, LANES), lambda b, p, i: (cblk + b, p))],
        out_specs=pl.BlockSpec((MLA_TQ, LANES), qmap),
        compiler_params=_cparams(("arbitrary", "arbitrary", "arbitrary")),
        name="mla_attn",
    )(qm, km, vm, km, vm)


def _s5_scan_kernel(u_ref, bre_ref, bim_ref, cre_ref, cim_ref, lre_ref, lim_ref, o_ref,
                    hre_ref, him_ref, bur_ref, bui_ref, sre_ref, sim_ref):
    c = pl.program_id(1)
    rows = u_ref.shape[0]
    n_seq = SUBLANES
    is_fwd = (lax.broadcasted_iota(jnp.int32, (rows, 1), 0) % n_seq) < (n_seq // 2)

    @pl.when(c == 0)
    def _():
        hre_ref[...] = jnp.zeros_like(hre_ref)
        him_ref[...] = jnp.zeros_like(him_ref)

    u = u_ref[...].astype(BF16)
    bur_ref[...] = jnp.where(is_fwd, _dot(u, bre_ref[0, 0]), _dot(u, bre_ref[1, 0]))
    bui_ref[...] = jnp.where(is_fwd, _dot(u, bim_ref[0, 0]), _dot(u, bim_ref[1, 0]))
    lre = lre_ref[0]
    lim = lim_ref[0]

    def step(t, carry):
        hr, hi = carry
        r0 = pl.multiple_of(t * n_seq, n_seq)
        nr = lre * hr - lim * hi + bur_ref[pl.ds(r0, n_seq), :]
        ni = lre * hi + lim * hr + bui_ref[pl.ds(r0, n_seq), :]
        sre_ref[pl.ds(r0, n_seq), :] = nr
        sim_ref[pl.ds(r0, n_seq), :] = ni
        return nr, ni

    hr, hi = lax.fori_loop(0, rows // n_seq, step, (hre_ref[...], him_ref[...]), unroll=8)
    hre_ref[...] = hr
    him_ref[...] = hi
    sre = sre_ref[...].astype(BF16)
    sim = sim_ref[...].astype(BF16)
    yf = _dot(sre, cre_ref[0, 0]) + _dot(sim, cim_ref[0, 0])
    yr = _dot(sre, cre_ref[1, 0]) + _dot(sim, cim_ref[1, 0])
    o_ref[...] = jnp.where(is_fwd, yf, yr)


def _s5_scan(u_seq, bre, bim, cre, cim, lre, lim):
    rows_total, width = u_seq.shape
    nblk = width // LANES
    rows = S5_STEPS * SUBLANES
    sw = bre.shape[-1]
    return pl.pallas_call(
        _s5_scan_kernel,
        out_shape=jax.ShapeDtypeStruct((rows_total, width), F32),
        grid=(nblk, rows_total // rows),
        in_specs=[pl.BlockSpec((rows, LANES), lambda k, c: (c, k)),
                  pl.BlockSpec((2, 1, LANES, sw), lambda k, c: (0, k, 0, 0)),
                  pl.BlockSpec((2, 1, LANES, sw), lambda k, c: (0, k, 0, 0)),
                  pl.BlockSpec((2, 1, sw, LANES), lambda k, c: (0, k, 0, 0)),
                  pl.BlockSpec((2, 1, sw, LANES), lambda k, c: (0, k, 0, 0)),
                  pl.BlockSpec((1, SUBLANES, sw), lambda k, c: (k, 0, 0)),
                  pl.BlockSpec((1, SUBLANES, sw), lambda k, c: (k, 0, 0))],
        out_specs=pl.BlockSpec((rows, LANES), lambda k, c: (c, k)),
        scratch_shapes=[pltpu.VMEM((SUBLANES, sw), F32), pltpu.VMEM((SUBLANES, sw), F32),
                        pltpu.VMEM((rows, sw), F32), pltpu.VMEM((rows, sw), F32),
                        pltpu.VMEM((rows, sw), F32), pltpu.VMEM((rows, sw), F32)],
        compiler_params=_cparams(("arbitrary", "arbitrary")),
        name="s5_scan",
    )(u_seq, bre, bim, cre, cim, lre, lim)


def _s5_post_kernel(pf_ref, yf_ref, yr_ref, d_ref, w_ref, b_ref, o_ref, *, width):
    u = pf_ref[:, 0:width]
    y = d_ref[...] * u + yf_ref[...] + yr_ref[...]
    k0 = math.sqrt(2.0 / math.pi)
    g = 0.5 * y * (1.0 + jnp.tanh(k0 * (y + 0.044715 * (y * y * y))))
    z = _dot(g.astype(BF16), w_ref[...]) + b_ref[...]
    o_ref[...] = (g * jax.nn.sigmoid(z)).astype(BF16)


def _s5_post(pf, yf, yr, d_skip, glu_w, glu_b):
    n_rows = pf.shape[0]
    width = yf.shape[1]
    kern = functools.partial(_s5_post_kernel, width=width)
    return pl.pallas_call(
        kern,
        out_shape=jax.ShapeDtypeStruct((n_rows, width), BF16),
        grid=(n_rows // TM,),
        in_specs=[pl.BlockSpec((TM, pf.shape[1]), lambda i: (i, 0)),
                  pl.BlockSpec((TM, width), lambda i: (i, 0)),
                  pl.BlockSpec((TM, width), lambda i: (i, 0)),
                  _const_spec((1, width)), _const_spec(glu_w.shape), _const_spec((1, width))],
        out_specs=pl.BlockSpec((TM, width), lambda i: (i, 0)),
        compiler_params=_cparams(("arbitrary",)),
        name="s5_post",
    )(pf, yf, yr, d_skip, glu_w, glu_b)


def _merge_kernel(x_ref, mod_ref, nw_ref, y0_ref, y1_ref, y2_ref, y3_ref, wg_ref, wb_ref, wo_ref, o_ref, acc_ref):
    x = x_ref[...]
    m = mod_ref[0]
    n = _rms_mod(x, nw_ref[...], m[3:4], m[4:5]).astype(BF16)
    ys = (y0_ref, y1_ref, y2_ref, y3_ref)
    for b in range(N_BRANCH):
        gate = jax.nn.sigmoid(_dot(n, wg_ref[b]))
        contrib = gate * _dot(ys[b][...], wb_ref[b])
        if b == 0:
            acc_ref[...] = contrib
        else:
            acc_ref[...] += contrib
    o_ref[...] = x + m[5:6] * _dot(acc_ref[...].astype(BF16), wo_ref[...])


def _merge(h, n_rows, mod, norm_w, ys, wg, wb, wo, *, n_lat_tiles, tiles_per_batch, n_batch):
    d = h.shape[1]
    bw = wb.shape[1]
    return pl.pallas_call(
        _merge_kernel,
        out_shape=jax.ShapeDtypeStruct((n_rows, d), F32),
        grid=(n_rows // TM,),
        in_specs=[pl.BlockSpec((TM, d), lambda i: (i, 0)),
                  pl.BlockSpec((1, N_MOD, d), lambda i: (_mod_row_map(n_lat_tiles, tiles_per_batch, n_batch)(i), 0, 0)),
                  _const_spec((1, d))]
                 + [pl.BlockSpec((TM, bw), lambda i: (i, 0))] * N_BRANCH
                 + [_const_spec(wg.shape), _const_spec(wb.shape), _const_spec(wo.shape)],
        out_specs=pl.BlockSpec((TM, d), lambda i: (i, 0)),
        scratch_shapes=[pltpu.VMEM((TM, d), F32)],
        compiler_params=_cparams(("arbitrary",)),
        name="merge",
    )(h, mod, norm_w, *ys, wg, wb, wo)


def _swap_perm(dim):
    nf = dim // 4
    j = np.arange(dim)
    return np.where((j // nf) % 2 == 0, j + nf, j - nf)


def _rope_tables(seq, dim, lane_off, width, period):
    nf = dim // 4
    pos = jnp.arange(seq)
    rows = (pos // GRID_W).astype(F32)
    cols = (pos % GRID_W).astype(F32)
    inv_freq = ROPE_BASE ** (-jnp.arange(nf, dtype=F32) / nf)
    ang_r = rows[:, None] * inv_freq[None, :]
    ang_c = cols[:, None] * inv_freq[None, :]
    cos = jnp.concatenate([jnp.cos(ang_r)] * 2 + [jnp.cos(ang_c)] * 2, axis=1)
    sin = jnp.concatenate([-jnp.sin(ang_r), jnp.sin(ang_r), -jnp.sin(ang_c), jnp.sin(ang_c)], axis=1)
    c_per = jnp.ones((seq, period), F32).at[:, lane_off:lane_off + dim].set(cos)
    s_per = jnp.zeros((seq, period), F32).at[:, lane_off:lane_off + dim].set(sin)
    reps = width // period
    c_tab = jnp.concatenate([jnp.tile(c_per, (1, reps)), jnp.ones((TM, width), F32)], axis=0)
    s_tab = jnp.concatenate([jnp.tile(s_per, (1, reps)), jnp.zeros((TM, width), F32)], axis=0)
    return c_tab, s_tab


def _inproj_weights(w_in):
    d = w_in.shape[0]
    hw = NA_HEADS * HEAD_DIM
    o_sq = 3 * hw
    o_sk = o_sq + SWA_HEADS * HEAD_DIM
    o_sv = o_sk + SWA_KV_HEADS * HEAD_DIM
    o_s5 = o_sv + SWA_KV_HEADS * HEAD_DIM
    s5w = 512
    o_cq = o_s5 + s5w
    o_ckv = o_cq + 256
    o_kr = o_ckv + 128
    o_g = o_kr + MLA_ROPE
    qscale = HEAD_DIM ** -0.5
    na = jnp.concatenate([w_in[:, :hw] * qscale, w_in[:, hw:3 * hw]], axis=1)
    sq = w_in[:, o_sq:o_sk] * qscale
    sk = w_in[:, o_sk:o_sv]
    sv = w_in[:, o_sv:o_s5]
    dup = np.concatenate([np.arange(HEAD_DIM) + kv * HEAD_DIM for kv in range(SWA_KV_HEADS) for _ in range(2)])
    sk_dup = sk[:, dup]
    sv_dup = sv[:, dup]
    p64 = _swap_perm(HEAD_DIM)
    perm_q = np.concatenate([h * HEAD_DIM + p64 for h in range(SWA_HEADS)])
    perm_k = np.concatenate([h * HEAD_DIM + p64 for h in range(2 * SWA_KV_HEADS)])
    wr = jnp.concatenate([sq, sk_dup], axis=1)
    wrs = jnp.concatenate([sq[:, perm_q], sk_dup[:, perm_k]], axis=1)
    wf = w_in[:, o_s5:o_kr]
    kr = w_in[:, o_kr:o_g]
    p32 = _swap_perm(MLA_ROPE)
    wk = jnp.zeros((d, LANES), F32).at[:, MLA_NOPE:MLA_NOPE + MLA_ROPE].set(kr)
    wks = jnp.zeros((d, LANES), F32).at[:, MLA_NOPE:MLA_NOPE + MLA_ROPE].set(kr[:, p32])
    wg = w_in[:, o_g:].reshape(d, N_BRANCH, d).transpose(1, 0, 2)
    proj = tuple(a.astype(BF16) for a in (na, wr, wrs, sv_dup, wf, wk, wks))
    return proj, wg.astype(BF16)


def _mla_weights(w_uq, w_ukv):
    ql = w_uq.shape[0]
    kvl = w_ukv.shape[0]
    dq = MLA_NOPE + MLA_ROPE
    wq3 = w_uq.reshape(ql, MLA_HEADS, dq)
    pad = jnp.zeros((ql, MLA_HEADS, LANES - dq), F32)
    wq = jnp.concatenate([wq3, pad], axis=2).reshape(ql, MLA_HEADS * LANES)
    p32 = _swap_perm(MLA_ROPE)
    wqs3 = jnp.concatenate([jnp.zeros((ql, MLA_HEADS, MLA_NOPE), F32), wq3[:, :, MLA_NOPE + p32], pad], axis=2)
    wqs = wqs3.reshape(ql, MLA_HEADS * LANES)
    wkv3 = w_ukv.reshape(kvl, MLA_HEADS, MLA_NOPE + MLA_V)
    wk = jnp.concatenate([wkv3[:, :, :MLA_NOPE], jnp.zeros((kvl, MLA_HEADS, LANES - MLA_NOPE), F32)], axis=2)
    wk = wk.reshape(kvl, MLA_HEADS * LANES)
    wv = wkv3[:, :, MLA_NOPE:].reshape(kvl, MLA_HEADS * MLA_V)
    return tuple(a.astype(BF16) for a in (wq, wqs, wk, wv))


def _s5_params(lam_re, lam_im, log_dt, b_re, b_im, c_re, c_im):
    lam = lax.complex(lam_re.astype(F32), lam_im.astype(F32))
    dt = jnp.exp(log_dt.astype(F32))[..., None]
    lam_bar = jnp.exp(lam * dt)
    b = lax.complex(b_re.astype(F32), b_im.astype(F32))
    b_bar = ((lam_bar - 1.0) / lam)[..., None] * b
    n_dir, g, p, cg = b_re.shape
    gpb = LANES // cg
    nblk = g // gpb
    eye = jnp.eye(gpb, dtype=F32)

    def in_map(x):
        x5 = x.reshape(n_dir, nblk, gpb, p, cg)
        return jnp.einsum('dkgpc,gh->dkgchp', x5, eye).reshape(n_dir, nblk, gpb * cg, gpb * p)

    def out_map(x):
        x5 = x.reshape(n_dir, nblk, gpb, cg, p)
        return jnp.einsum('dkgcp,gh->dkgphc', x5, eye).reshape(n_dir, nblk, gpb * p, gpb * cg)

    bre = in_map(jnp.real(b_bar)).astype(BF16)
    bim = in_map(jnp.imag(b_bar)).astype(BF16)
    cre = out_map(c_re.astype(F32)).astype(BF16)
    cim = out_map(-c_im.astype(F32)).astype(BF16)
    half = SUBLANES // 2

    def lam_rows(x):
        x3 = x.reshape(n_dir, nblk, gpb * p)
        return jnp.concatenate([jnp.broadcast_to(x3[0][:, None, :], (nblk, half, gpb * p)),
                                jnp.broadcast_to(x3[1][:, None, :], (nblk, half, gpb * p))], axis=1)

    return bre, bim, cre, cim, lam_rows(jnp.real(lam_bar)), lam_rows(jnp.imag(lam_bar))


def _ffn_weights(wg, wu, wd):
    d, ff = wg.shape
    nch = ff // FF_CHUNK
    wg3 = wg.reshape(d, nch, FF_CHUNK).transpose(1, 0, 2).astype(BF16)
    wu3 = wu.reshape(d, nch, FF_CHUNK).transpose(1, 0, 2).astype(BF16)
    wd3 = wd.reshape(nch, FF_CHUNK, d).astype(BF16)
    return wg3, wu3, wd3


def _s5_to_seq(u, n_batch, seq, ctx):
    w = u.shape[1]
    ul = u[:n_batch * seq].reshape(n_batch, seq, w)
    uc = u[n_batch * seq:].reshape(n_batch, ctx, w)
    fwd = jnp.concatenate([uc, ul], axis=1)
    rev = jnp.concatenate([uc[:, ::-1], ul[:, ::-1]], axis=1)
    allseq = jnp.concatenate([fwd, rev], axis=0)
    return allseq.transpose(1, 0, 2).reshape((seq + ctx) * 2 * n_batch, w)


def _s5_from_seq(y, n_batch, seq, ctx):
    w = y.shape[1]
    y3 = y.reshape(seq + ctx, 2 * n_batch, w).transpose(1, 0, 2)
    yf, yr = y3[:n_batch], y3[n_batch:]

    def tok(a, flip):
        c, l = a[:, :ctx], a[:, ctx:]
        if flip:
            c, l = c[:, ::-1], l[:, ::-1]
        return jnp.concatenate([l.reshape(n_batch * seq, w), c.reshape(n_batch * ctx, w)], axis=0)

    return tok(yf, False), tok(yr, True)


def kernel(x, c, ctx, c_ctx, ada_w, ada_b, ffn1_norm, ffn1_w_gate, ffn1_w_up, ffn1_w_down, mix_norm, w_in, na_rpb, swa_sink, s5_lambda_re, s5_lambda_im, s5_log_dt, s5_b_re, s5_b_im, s5_c_re, s5_c_im, s5_d, s5_glu_w, s5_glu_b, mla_q_norm, mla_w_uq, mla_kv_norm, mla_w_ukv, w_branch, w_out, ffn2_norm, ffn2_w_gate, ffn2_w_up, ffn2_w_down, final_norm):
    n_batch, seq, d = x.shape
    n_ctx = ctx.shape[1]
    depth = ada_w.shape[0]
    assert 2 * n_batch == SUBLANES and seq % TM == 0 and (n_batch * n_ctx) % TM == 0
    n_lat = n_batch * seq
    n_all = n_lat + n_batch * n_ctx
    tiles_per_batch = seq // TM
    n_lat_tiles = n_lat // TM
    geo = dict(n_lat_tiles=n_lat_tiles, tiles_per_batch=tiles_per_batch, n_batch=n_batch)

    h = jnp.concatenate([x.reshape(n_lat, d), ctx.reshape(n_batch * n_ctx, d)], axis=0)
    cc = jnp.concatenate([c, c_ctx[None, :], jnp.zeros((SUBLANES - n_batch - 1, d), F32)], axis=0)
    mod = _ada_mod(cc, ada_w, ada_b)

    cs_sw, sn_sw = _rope_tables(seq, HEAD_DIM, 0, 2 * LANES, HEAD_DIM)
    ck_kr, sk_kr = _rope_tables(seq, MLA_ROPE, MLA_NOPE, LANES, LANES)
    rows_n = seq // GRID_W
    s5w = s5_d.shape[1]
    offs = (s5w, s5w + mla_w_uq.shape[1], s5w + mla_w_uq.shape[1] + mla_w_ukv.shape[1])
    sw_col0 = 3 * NA_HEADS * HEAD_DIM

    for l in range(depth):
        need_ctx = l < depth - 1
        last = l == depth - 1
        ml = mod[l]
        ones = jnp.ones((1, d), F32)
        wg3, wu3, wd3 = _ffn_weights(ffn1_w_gate[l], ffn1_w_up[l], ffn1_w_down[l])
        h = _ffn(h, n_all, ml, ffn1_norm[l][None, :], wg3, wu3, wd3, ones, base=0, final=False, **geo)
        proj_w, gate_w = _inproj_weights(w_in[l])
        pb, pf = _inproj(h, ml, mix_norm[l][None, :], proj_w, (cs_sw, sn_sw, ck_kr, sk_kr), **geo)
        bias = _na_bias_table(na_rpb[l].astype(F32), rows_n)
        y_na = _na_attention(pb, bias, n_batch=n_batch, seq=seq, ctx=n_ctx, need_ctx=need_ctx)
        y_sw_l, y_sw_c = _swa_attention(pb, swa_sink[l].astype(F32), n_batch=n_batch, seq=seq, ctx=n_ctx,
                                        need_ctx=need_ctx, col0=sw_col0)
        y_sw = jnp.concatenate([y_sw_l, y_sw_c], axis=0)
        wq, wqs, wk, wv = _mla_weights(mla_w_uq[l], mla_w_ukv[l])
        qm, km, vm = _mla_prep(pf, mla_q_norm[l][None, :], mla_kv_norm[l][None, :], wq, wqs, wk, wv, ck_kr, sk_kr,
                               n_lat_tiles=n_lat_tiles, tiles_per_batch=tiles_per_batch, offs=offs)
        y_mla = _mla_attention(qm, km, vm, n_batch=n_batch, seq=seq, ctx=n_ctx, need_ctx=need_ctx)
        s5p = _s5_params(s5_lambda_re[l], s5_lambda_im[l], s5_log_dt[l], s5_b_re[l], s5_b_im[l],
                         s5_c_re[l], s5_c_im[l])
        u_seq = _s5_to_seq(pf[:, :s5w], n_batch, seq, n_ctx)
        y_seq = _s5_scan(u_seq, *s5p)
        yf, yr = _s5_from_seq(y_seq, n_batch, seq, n_ctx)
        y_s5 = _s5_post(pf, yf, yr, s5_d[l][None, :].astype(F32), s5_glu_w[l].astype(BF16),
                        s5_glu_b[l][None, :].astype(F32))
        n_rows = n_all if need_ctx else n_lat
        h = _merge(h, n_rows, ml, mix_norm[l][None, :], (y_na, y_sw, y_s5, y_mla), gate_w,
                   w_branch[l].astype(BF16), w_out[l].astype(BF16), **geo)
        wg3, wu3, wd3 = _ffn_weights(ffn2_w_gate[l], ffn2_w_up[l], ffn2_w_down[l])
        h = _ffn(h, n_rows, ml, ffn2_norm[l][None, :], wg3, wu3, wd3, final_norm[None, :], base=6, final=last, **geo)
    return h.reshape(n_batch, seq, d)
```

```python
import functools
import math

import numpy as np
import jax
import jax.numpy as jnp
from jax import lax
from jax.experimental import pallas as pl
from jax.experimental.pallas import tpu as pltpu

F32 = jnp.float32
BF16 = jnp.bfloat16

GRID_W = 64
HEAD_DIM = 64
N_BRANCH = 4
NA_HEADS = 8
NA_WIN_ROWS = 8
NA_WIN_COLS = 16
SWA_HEADS = 8
SWA_KV_HEADS = 2
SWA_WINDOW = 128
S5_GROUP = 16
S5_STATE = 64
MLA_HEADS = 8
MLA_NOPE = 64
MLA_ROPE = 32
MLA_V = 64
MACARON_WEIGHT = 0.5
ROPE_BASE = 10000.0
EPS = 1e-6
N_MOD = 9

LANES = 128
SUBLANES = 8
TM = 512
FF_CHUNK = 256
NA_QROWS = 4
NA_KROWS = NA_QROWS + NA_WIN_ROWS - 1
SWA_BLK = 128
MLA_TQ = 256
S5_STEPS = 64
S5_PITCH = S5_STEPS + SUBLANES
NEG = -1e30
VMEM_LIMIT = 56 * 1024 * 1024


def _cparams(sem):
    return pltpu.CompilerParams(dimension_semantics=sem, vmem_limit_bytes=VMEM_LIMIT)


def _const_spec(shape):
    nd = len(shape)
    return pl.BlockSpec(shape, lambda *_: (0,) * nd, pipeline_mode=pl.Buffered(1))


def _dot(a, b):
    return jnp.dot(a, b, preferred_element_type=F32)


def _dot_t(a, b):
    return lax.dot_general(a, b, (((1,), (1,)), ((), ())), preferred_element_type=F32)


def _rms(x, w):
    return x * lax.rsqrt(jnp.mean(x * x, axis=-1, keepdims=True) + EPS) * w


def _rms_mod(x, w, shift, scale):
    return _rms(x, w) * (1.0 + scale) + shift


def _mod_row_map(n_lat_tiles, tiles_per_batch, n_batch):
    def f(i):
        return jnp.where(i < n_lat_tiles, i // tiles_per_batch, n_batch)
    return f


def _ada_kernel(c_ref, w_ref, b_ref, o_ref):
    c = c_ref[...]
    s = c * jax.nn.sigmoid(c)
    o_ref[0] = jnp.dot(s, w_ref[0], preferred_element_type=F32,
                       precision=lax.Precision.HIGHEST) + b_ref[0]


def _ada_mod(cc, ada_w, ada_b):
    depth, d, nd = ada_w.shape
    tn = 1024
    out = pl.pallas_call(
        _ada_kernel,
        out_shape=jax.ShapeDtypeStruct((depth, SUBLANES, nd), F32),
        grid=(depth, nd // tn),
        in_specs=[pl.BlockSpec((SUBLANES, d), lambda l, j: (0, 0)),
                  pl.BlockSpec((1, d, tn), lambda l, j: (l, 0, j)),
                  pl.BlockSpec((1, 1, tn), lambda l, j: (l, 0, j))],
        out_specs=pl.BlockSpec((1, SUBLANES, tn), lambda l, j: (l, 0, j)),
        compiler_params=_cparams(("arbitrary", "arbitrary")),
        name="ada_mod",
    )(cc, ada_w, ada_b.reshape(depth, 1, nd))
    return out.reshape(depth, SUBLANES, N_MOD, d)


def _ffn_kernel(x_ref, mod_ref, nw_ref, wg_ref, wu_ref, wd_ref, fw_ref, o_ref, acc_ref, *, base, final):
    x = x_ref[...]
    m = mod_ref[0]
    n = _rms_mod(x, nw_ref[...], m[base:base + 1], m[base + 1:base + 2]).astype(BF16)
    acc_ref[...] = jnp.zeros_like(acc_ref)

    def body(c, carry):
        g = _dot(n, wg_ref[c])
        u = _dot(n, wu_ref[c])
        a = (g * jax.nn.sigmoid(g) * u).astype(BF16)
        acc_ref[...] += _dot(a, wd_ref[c])
        return carry

    lax.fori_loop(0, wg_ref.shape[0], body, 0)
    out = x + MACARON_WEIGHT * m[base + 2:base + 3] * acc_ref[...]
    if final:
        out = _rms(out, fw_ref[...])
    o_ref[...] = out


def _ffn(h, n_rows, mod, norm_w, wg, wu, wd, final_w, *, base, final, n_lat_tiles, tiles_per_batch, n_batch):
    d = h.shape[1]
    nch, _, fc = wg.shape
    kern = functools.partial(_ffn_kernel, base=base, final=final)
    return pl.pallas_call(
        kern,
        out_shape=jax.ShapeDtypeStruct((n_rows, d), F32),
        grid=(n_rows // TM,),
        in_specs=[pl.BlockSpec((TM, d), lambda i: (i, 0)),
                  pl.BlockSpec((1, N_MOD, d), lambda i: (_mod_row_map(n_lat_tiles, tiles_per_batch, n_batch)(i), 0, 0)),
                  _const_spec((1, d)),
                  _const_spec((nch, d, fc)), _const_spec((nch, d, fc)), _const_spec((nch, fc, d)),
                  _const_spec((1, d))],
        out_specs=pl.BlockSpec((TM, d), lambda i: (i, 0)),
        scratch_shapes=[pltpu.VMEM((TM, d), F32)],
        compiler_params=_cparams(("arbitrary",)),
        name="ffn_final" if final else "ffn",
    )(h, mod, norm_w, wg, wu, wd, final_w)


def _inproj_kernel(x_ref, mod_ref, nw_ref, wa_ref, wr_ref, wrs_ref, wv_ref, wf_ref, wk_ref, wks_ref,
                   cs_ref, sn_ref, ck_ref, sk_ref, ob_ref, of_ref):
    x = x_ref[...]
    m = mod_ref[0]
    n = _rms_mod(x, nw_ref[...], m[3:4], m[4:5]).astype(BF16)
    na = wa_ref.shape[1]
    nr = wr_ref.shape[1]
    nv = wv_ref.shape[1]
    nf = wf_ref.shape[1]
    cw = 2 * LANES
    for c in range(na // cw):
        ob_ref[:, c * cw:(c + 1) * cw] = _dot(n, wa_ref[:, c * cw:(c + 1) * cw]).astype(BF16)
    cs = cs_ref[...]
    sn = sn_ref[...]
    for c in range(nr // cw):
        y = _dot(n, wr_ref[:, c * cw:(c + 1) * cw])
        ys = _dot(n, wrs_ref[:, c * cw:(c + 1) * cw])
        ob_ref[:, na + c * cw:na + (c + 1) * cw] = (y * cs + ys * sn).astype(BF16)
    for c in range(nv // cw):
        ob_ref[:, na + nr + c * cw:na + nr + (c + 1) * cw] = _dot(n, wv_ref[:, c * cw:(c + 1) * cw]).astype(BF16)
    for c in range(nf // LANES):
        of_ref[:, c * LANES:(c + 1) * LANES] = _dot(n, wf_ref[:, c * LANES:(c + 1) * LANES])
    of_ref[:, nf:nf + LANES] = _dot(n, wk_ref[...]) * ck_ref[...] + _dot(n, wks_ref[...]) * sk_ref[...]


def _inproj(h, mod, norm_w, w, tabs, *, n_lat_tiles, tiles_per_batch, n_batch):
    n_rows, d = h.shape
    wa, wr, wrs, wv, wf, wk, wks = w
    cs, sn, ck, sk = tabs
    nb = wa.shape[1] + wr.shape[1] + wv.shape[1]
    nf = wf.shape[1] + LANES
    tab_map = lambda i: (jnp.where(i < n_lat_tiles, i % tiles_per_batch, tiles_per_batch), 0)
    return pl.pallas_call(
        _inproj_kernel,
        out_shape=(jax.ShapeDtypeStruct((n_rows, nb), BF16), jax.ShapeDtypeStruct((n_rows, nf), F32)),
        grid=(n_rows // TM,),
        in_specs=[pl.BlockSpec((TM, d), lambda i: (i, 0)),
                  pl.BlockSpec((1, N_MOD, d), lambda i: (_mod_row_map(n_lat_tiles, tiles_per_batch, n_batch)(i), 0, 0)),
                  _const_spec((1, d))]
                 + [_const_spec(a.shape) for a in w]
                 + [pl.BlockSpec((TM, 2 * LANES), tab_map), pl.BlockSpec((TM, 2 * LANES), tab_map),
                    pl.BlockSpec((TM, LANES), tab_map), pl.BlockSpec((TM, LANES), tab_map)],
        out_specs=(pl.BlockSpec((TM, nb), lambda i: (i, 0)), pl.BlockSpec((TM, nf), lambda i: (i, 0))),
        compiler_params=_cparams(("arbitrary",)),
        name="inproj",
    )(h, mod, norm_w, *w, cs, sn, ck, sk)


def _lane_half_masks(rows):
    lane = lax.broadcasted_iota(jnp.int32, (rows, LANES), 1)
    return lane < HEAD_DIM


def _softmax2(s1, s2, sink=None):
    m = jnp.maximum(jnp.max(s1, axis=-1, keepdims=True), jnp.max(s2, axis=-1, keepdims=True))
    if sink is not None:
        m = jnp.maximum(m, sink)
    p1 = jnp.exp(s1 - m)
    p2 = jnp.exp(s2 - m)
    l = jnp.sum(p1, axis=-1, keepdims=True) + jnp.sum(p2, axis=-1, keepdims=True)
    if sink is not None:
        l = l + jnp.exp(sink - m)
    return p1, p2, l


def _na_kernel(q_ref, k_ref, v_ref, kc_ref, vc_ref, bias_ref, o_ref, *, n_blk, rows_n):
    blk = pl.program_id(1)
    nq = NA_QROWS * GRID_W
    nk = NA_KROWS * GRID_W
    ws = jnp.clip(NA_QROWS * blk - NA_WIN_ROWS // 2, 0, rows_n - NA_KROWS)
    r0 = pl.multiple_of(ws * GRID_W, GRID_W)
    lo = _lane_half_masks(nq)
    for j in range(NA_HEADS // 2):
        sl = slice(j * LANES, (j + 1) * LANES)
        qp = q_ref[:, sl]
        kw = k_ref[pl.ds(r0, nk), sl]
        vw = v_ref[pl.ds(r0, nk), sl]
        kc = kc_ref[:, sl]
        vc = vc_ref[:, sl]
        outs = []
        for half in range(2):
            qm = jnp.where(lo if half == 0 else jnp.logical_not(lo), qp, jnp.zeros_like(qp))
            s1 = _dot_t(qm, kw) + bias_ref[0, 2 * j + half]
            s2 = _dot_t(qm, kc)
            p1, p2, l = _softmax2(s1, s2)
            o = _dot(p1.astype(BF16), vw) + _dot(p2.astype(BF16), vc)
            outs.append(o / l)
        o_ref[:, sl] = jnp.where(lo, outs[0], outs[1]).astype(BF16)


def _na_attention(pb, bias, *, n_batch, seq, ctx, need_ctx):
    rows_n = seq // GRID_W
    n_blk = rows_n // NA_QROWS
    nq = NA_QROWS * GRID_W
    hw = NA_HEADS * HEAD_DIM
    n_q = n_blk + (1 if need_ctx else 0)
    assert ctx == nq
    lat_blocks = n_batch * seq // nq
    qmap = lambda b, i: (jnp.where(i < n_blk, b * n_blk + i, lat_blocks + b), 0)

    def bias_map(b, i):
        t = jnp.where(i == 0, 0, jnp.where(i == 1, 1, jnp.where(i == n_blk - 1, 3, jnp.where(i == n_blk, 4, 2))))
        return (t, 0, 0, 0)

    kern = functools.partial(_na_kernel, n_blk=n_blk, rows_n=rows_n)
    return pl.pallas_call(
        kern,
        out_shape=jax.ShapeDtypeStruct((n_batch * n_q * nq, hw), BF16),
        grid=(n_batch, n_q),
        in_specs=[pl.BlockSpec((nq, hw), qmap),
                  pl.BlockSpec((seq, hw), lambda b, i: (b, 1)),
                  pl.BlockSpec((seq, hw), lambda b, i: (b, 2)),
                  pl.BlockSpec((ctx, hw), lambda b, i: (n_batch * seq // ctx + b, 1)),
                  pl.BlockSpec((ctx, hw), lambda b, i: (n_batch * seq // ctx + b, 2)),
                  pl.BlockSpec((1, NA_HEADS, nq, NA_KROWS * GRID_W), bias_map)],
        out_specs=pl.BlockSpec((nq, hw), qmap),
        compiler_params=_cparams(("arbitrary", "arbitrary")),
        name="na_attn",
    )(pb, pb, pb, pb, pb, bias)


def _na_bias_table(rpb, rows_n):
    n_blk = rows_n // NA_QROWS
    kr_n = min(NA_WIN_ROWS, rows_n)
    n_heads = rpb.shape[0]
    col = np.arange(GRID_W)
    c0 = np.clip(col - NA_WIN_COLS // 2, 0, GRID_W - NA_WIN_COLS)
    col_ok = (col[None, :] >= c0[:, None]) & (col[None, :] < c0[:, None] + NA_WIN_COLS)
    padded = jnp.pad(rpb, ((0, 0), (0, 0), (GRID_W, GRID_W)))
    off = GRID_W + NA_WIN_COLS - 1
    toep = jnp.stack([padded[:, :, off - qc:off - qc + GRID_W] for qc in range(GRID_W)], axis=2)
    toep = jnp.where(col_ok[None, None], toep, NEG)
    masked = jnp.full((n_heads, GRID_W, GRID_W), NEG, F32)
    tabs = []
    for blk in (0, 1, 2, n_blk - 1):
        ws = int(np.clip(NA_QROWS * blk - NA_WIN_ROWS // 2, 0, rows_n - NA_KROWS))
        per_q = []
        for qr in range(NA_QROWS):
            r = NA_QROWS * blk + qr
            r0 = int(np.clip(r - kr_n // 2, 0, rows_n - kr_n))
            per_k = []
            for kk in range(NA_KROWS):
                kr = ws + kk
                ok = r0 <= kr < r0 + kr_n
                per_k.append(toep[:, kr - r + NA_WIN_ROWS - 1] if ok else masked)
            per_q.append(jnp.stack(per_k, axis=2))
        t = jnp.stack(per_q, axis=1)
        tabs.append(t.reshape(n_heads, NA_QROWS * GRID_W, NA_KROWS * GRID_W))
    tabs.append(jnp.full_like(tabs[0], NEG))
    return jnp.stack(tabs)


def _swa_kernel(sink_ref, q_ref, k_ref, v_ref, qc_ref, kc_ref, vc_ref, o_ref, oc_ref, *, seq, need_ctx):
    n_blk = seq // SWA_BLK
    band = 3 * SWA_BLK
    group = SWA_HEADS // SWA_KV_HEADS
    lo = _lane_half_masks(SWA_BLK)
    qi = lax.broadcasted_iota(jnp.int32, (SWA_BLK, band), 0)
    ki = lax.broadcasted_iota(jnp.int32, (SWA_BLK, band), 1)

    def head_attn(qp, half, h, kb, vb, kc, vc, mask_bias, lo_m):
        qm = jnp.where(lo_m if half == 0 else jnp.logical_not(lo_m), qp, jnp.zeros_like(qp))
        s2 = _dot_t(qm, kc)
        sink = sink_ref[h]
        if kb is None:
            m = jnp.maximum(jnp.max(s2, axis=-1, keepdims=True), sink)
            p2 = jnp.exp(s2 - m)
            l = jnp.sum(p2, axis=-1, keepdims=True) + jnp.exp(sink - m)
            return _dot(p2.astype(BF16), vc) / l
        s1 = _dot_t(qm, kb) + mask_bias
        p1, p2, l = _softmax2(s1, s2, sink)
        return (_dot(p1.astype(BF16), vb) + _dot(p2.astype(BF16), vc)) / l

    def blk_body(n, carry):
        start = jnp.clip((n - 1) * SWA_BLK, 0, seq - band)
        start = pl.multiple_of(start, SWA_BLK)
        q0 = pl.multiple_of(n * SWA_BLK, SWA_BLK)
        delta = (start + ki) - (q0 + qi)
        mask_bias = jnp.where(jnp.abs(delta) <= SWA_WINDOW, 0.0, NEG).astype(F32)
        for j in range(SWA_HEADS // 2):
            sl = slice(j * LANES, (j + 1) * LANES)
            kv = (2 * j) // group
            ksl = slice(kv * LANES, (kv + 1) * LANES)
            qp = q_ref[pl.ds(q0, SWA_BLK), sl]
            kb = k_ref[pl.ds(start, band), ksl]
            vb = v_ref[pl.ds(start, band), ksl]
            kc = kc_ref[:, ksl]
            vc = vc_ref[:, ksl]
            o0 = head_attn(qp, 0, 2 * j, kb, vb, kc, vc, mask_bias, lo)
            o1 = head_attn(qp, 1, 2 * j + 1, kb, vb, kc, vc, mask_bias, lo)
            o_ref[pl.ds(q0, SWA_BLK), sl] = jnp.where(lo, o0, o1).astype(BF16)
        return carry

    lax.fori_loop(0, n_blk, blk_body, 0)

    if need_ctx:
        nc = qc_ref.shape[0]
        lo_c = _lane_half_masks(nc)
        for j in range(SWA_HEADS // 2):
            sl = slice(j * LANES, (j + 1) * LANES)
            kv = (2 * j) // group
            ksl = slice(kv * LANES, (kv + 1) * LANES)
            qp = qc_ref[:, sl]
            kc = kc_ref[:, ksl]
            vc = vc_ref[:, ksl]
            o0 = head_attn(qp, 0, 2 * j, None, None, kc, vc, None, lo_c)
            o1 = head_attn(qp, 1, 2 * j + 1, None, None, kc, vc, None, lo_c)
            oc_ref[:, sl] = jnp.where(lo_c, o0, o1).astype(BF16)
    else:
        oc_ref[...] = jnp.zeros_like(oc_ref)


def _swa_attention(pb, sink, *, n_batch, seq, ctx, need_ctx, col0):
    hw = SWA_HEADS * HEAD_DIM
    kw = SWA_KV_HEADS * LANES
    qcol = col0 // hw
    kcol = (col0 + hw) // kw
    vcol = kcol + 1
    cblk = n_batch * seq // ctx
    kern = functools.partial(_swa_kernel, seq=seq, need_ctx=need_ctx)
    return pl.pallas_call(
        kern,
        out_shape=(jax.ShapeDtypeStruct((n_batch * seq, hw), BF16),
                   jax.ShapeDtypeStruct((n_batch * ctx, hw), BF16)),
        grid=(n_batch,),
        in_specs=[pl.BlockSpec(memory_space=pltpu.SMEM),
                  pl.BlockSpec((seq, hw), lambda b: (b, qcol)),
                  pl.BlockSpec((seq, kw), lambda b: (b, kcol)),
                  pl.BlockSpec((seq, kw), lambda b: (b, vcol)),
                  pl.BlockSpec((ctx, hw), lambda b: (cblk + b, qcol)),
                  pl.BlockSpec((ctx, kw), lambda b: (cblk + b, kcol)),
                  pl.BlockSpec((ctx, kw), lambda b: (cblk + b, vcol))],
        out_specs=(pl.BlockSpec((seq, hw), lambda b: (b, 0)),
                   pl.BlockSpec((ctx, hw), lambda b: (b, 0))),
        compiler_params=_cparams(("arbitrary",)),
        name="swa_attn",
    )(sink, pb, pb, pb, pb, pb, pb)


def _mla_prep_kernel(pf_ref, qn_ref, kn_ref, wq_ref, wqs_ref, wk_ref, wv_ref, cq_ref, sq_ref,
                     q_ref, k_ref, v_ref, *, off_cq, off_ckv, off_kr, q_lora, kv_lora):
    cq = pf_ref[:, off_cq:off_cq + q_lora]
    ckv = pf_ref[:, off_ckv:off_ckv + kv_lora]
    kr = pf_ref[:, off_kr:off_kr + LANES]
    cqn = _rms(cq, qn_ref[...]).astype(BF16)
    ckvn = _rms(ckv, kn_ref[...]).astype(BF16)
    cs = cq_ref[...]
    sn = sq_ref[...]
    for h in range(MLA_HEADS):
        sl = slice(h * LANES, (h + 1) * LANES)
        q = _dot(cqn, wq_ref[:, sl]) * cs + _dot(cqn, wqs_ref[:, sl]) * sn
        q_ref[:, sl] = q.astype(BF16)
        k_ref[:, sl] = (_dot(ckvn, wk_ref[:, sl]) + kr).astype(BF16)
    v_ref[...] = _dot(ckvn, wv_ref[...]).astype(BF16)


def _mla_prep(pf, qn, kn, wq, wqs, wk, wv, cq_tab, sq_tab, *, n_lat_tiles, tiles_per_batch, offs):
    n_rows = pf.shape[0]
    off_cq, off_ckv, off_kr = offs
    tab_map = lambda i: (jnp.where(i < n_lat_tiles, i % tiles_per_batch, tiles_per_batch), 0)
    kern = functools.partial(_mla_prep_kernel, off_cq=off_cq, off_ckv=off_ckv, off_kr=off_kr,
                             q_lora=wq.shape[0], kv_lora=wk.shape[0])
    hq = MLA_HEADS * LANES
    hv = MLA_HEADS * MLA_V
    return pl.pallas_call(
        kern,
        out_shape=(jax.ShapeDtypeStruct((n_rows, hq), BF16), jax.ShapeDtypeStruct((n_rows, hq), BF16),
                   jax.ShapeDtypeStruct((n_rows, hv), BF16)),
        grid=(n_rows // TM,),
        in_specs=[pl.BlockSpec((TM, pf.shape[1]), lambda i: (i, 0)),
                  _const_spec(qn.shape), _const_spec(kn.shape),
                  _const_spec(wq.shape), _const_spec(wqs.shape), _const_spec(wk.shape), _const_spec(wv.shape),
                  pl.BlockSpec((TM, LANES), tab_map), pl.BlockSpec((TM, LANES), tab_map)],
        out_specs=(pl.BlockSpec((TM, hq), lambda i: (i, 0)), pl.BlockSpec((TM, hq), lambda i: (i, 0)),
                   pl.BlockSpec((TM, hv), lambda i: (i, 0))),
        compiler_params=_cparams(("arbitrary",)),
        name="mla_prep",
    )(pf, qn, kn, wq, wqs, wk, wv, cq_tab, sq_tab)


def _mla_kernel(q_ref, k_ref, v_ref, kc_ref, vc_ref, o_ref, *, n_qt, scale):
    i = pl.program_id(2)
    lo = _lane_half_masks(q_ref.shape[0])

    def run(with_latent):
        outs = []
        for half in range(2):
            sl = slice(half * LANES, (half + 1) * LANES)
            q = q_ref[:, sl]
            s2 = _dot_t(q, kc_ref[:, sl])
            if with_latent:
                s1 = _dot_t(q, k_ref[:, sl])
                m = jnp.maximum(jnp.max(s1, axis=-1, keepdims=True), jnp.max(s2, axis=-1, keepdims=True))
                p1 = jnp.exp((s1 - m) * scale)
                p2 = jnp.exp((s2 - m) * scale)
                l = jnp.sum(p1, axis=-1, keepdims=True) + jnp.sum(p2, axis=-1, keepdims=True)
                o = _dot(p1.astype(BF16), v_ref[...]) + _dot(p2.astype(BF16), vc_ref[...])
            else:
                m = jnp.max(s2, axis=-1, keepdims=True)
                p2 = jnp.exp((s2 - m) * scale)
                l = jnp.sum(p2, axis=-1, keepdims=True)
                o = _dot(p2.astype(BF16), vc_ref[...])
            outs.append(o / l)
        o_ref[...] = jnp.where(lo, outs[0], outs[1]).astype(BF16)

    @pl.when(i < n_qt)
    def _():
        run(True)

    @pl.when(i >= n_qt)
    def _():
        run(False)


def _mla_attention(qm, km, vm, *, n_batch, seq, ctx, need_ctx):
    n_qt = seq // MLA_TQ
    n_q = n_qt + (1 if need_ctx else 0)
    assert ctx == MLA_TQ
    lat_tiles = n_batch * n_qt
    cblk = n_batch * seq // ctx
    qmap = lambda b, p, i: (jnp.where(i < n_qt, b * n_qt + i, lat_tiles + b), p)
    scale = float((MLA_NOPE + MLA_ROPE) ** -0.5)
    kern = functools.partial(_mla_kernel, n_qt=n_qt, scale=scale)
    return pl.pallas_call(
        kern,
        out_shape=jax.ShapeDtypeStruct((n_batch * n_q * MLA_TQ, MLA_HEADS * MLA_V), BF16),
        grid=(n_batch, MLA_HEADS // 2, n_q),
        in_specs=[pl.BlockSpec((MLA_TQ, 2 * LANES), qmap),
                  pl.BlockSpec((seq, 2 * LANES), lambda b, p, i: (b, p)),
                  pl.BlockSpec((seq, LANES), lambda b, p, i: (b, p)),
                  pl.BlockSpec((ctx, 2 * LANES), lambda b, p, i: (cblk + b, p)),
                  pl.BlockSpec((ctx, LANES), lambda b, p, i: (cblk + b, p))],
        out_specs=pl.BlockSpec((MLA_TQ, LANES), qmap),
        compiler_params=_cparams(("arbitrary", "arbitrary", "arbitrary")),
        name="mla_attn",
    )(qm, km, vm, km, vm)


def _s5_scan_kernel(uf0_ref, uf1_ref, uf2_ref, uf3_ref, ur0_ref, ur1_ref, ur2_ref, ur3_ref,
                    bre_ref, bim_ref, cre_ref, cim_ref, lre_ref, lim_ref, of_ref, or_ref,
                    hre_ref, him_ref, stage_ref, lhs_ref, bur_ref, bui_ref, sre_ref, sim_ref,
                    ysc_ref, ya_ref, yb_ref):
    c = pl.program_id(1)
    n_seq = SUBLANES
    half = n_seq // 2
    steps = S5_STEPS
    pitch = S5_PITCH
    rows = steps * n_seq
    ufs = (uf0_ref, uf1_ref, uf2_ref, uf3_ref)
    urs = (ur0_ref, ur1_ref, ur2_ref, ur3_ref)
    is_fwd = (lax.broadcasted_iota(jnp.int32, (rows, 1), 0) % n_seq) < half

    @pl.when(c == 0)
    def _():
        hre_ref[...] = jnp.zeros_like(hre_ref)
        him_ref[...] = jnp.zeros_like(him_ref)

    for b in range(half):
        stage_ref[b * pitch:b * pitch + steps, :] = ufs[b][...]
        stage_ref[(half + b) * pitch:(half + b) * pitch + steps, :] = urs[b][...]
    sub_fwd = lax.broadcasted_iota(jnp.int32, (n_seq, LANES), 0) < half
    for t in range(steps):
        ga = stage_ref[pl.ds(t, n_seq, stride=pitch), :]
        gb = stage_ref[pl.ds(steps - 1 - t, n_seq, stride=pitch), :]
        lhs_ref[t * n_seq:(t + 1) * n_seq, :] = jnp.where(sub_fwd, ga, gb)

    u = lhs_ref[...].astype(BF16)
    bur_ref[...] = jnp.where(is_fwd, _dot(u, bre_ref[0, 0]), _dot(u, bre_ref[1, 0]))
    bui_ref[...] = jnp.where(is_fwd, _dot(u, bim_ref[0, 0]), _dot(u, bim_ref[1, 0]))
    lre = lre_ref[0]
    lim = lim_ref[0]

    def step(t, carry):
        hr, hi = carry
        r0 = pl.multiple_of(t * n_seq, n_seq)
        nr = lre * hr - lim * hi + bur_ref[pl.ds(r0, n_seq), :]
        ni = lre * hi + lim * hr + bui_ref[pl.ds(r0, n_seq), :]
        sre_ref[pl.ds(r0, n_seq), :] = nr
        sim_ref[pl.ds(r0, n_seq), :] = ni
        return nr, ni

    hr, hi = lax.fori_loop(0, steps, step, (hre_ref[...], him_ref[...]), unroll=8)
    hre_ref[...] = hr
    him_ref[...] = hi
    sre = sre_ref[...].astype(BF16)
    sim = sim_ref[...].astype(BF16)
    yf = _dot(sre, cre_ref[0, 0]) + _dot(sim, cim_ref[0, 0])
    yr = _dot(sre, cre_ref[1, 0]) + _dot(sim, cim_ref[1, 0])
    ysc_ref[...] = jnp.where(is_fwd, yf, yr)
    for t in range(steps):
        g = ysc_ref[t * n_seq:(t + 1) * n_seq, :]
        ya_ref[pl.ds(t, n_seq, stride=pitch), :] = g
        yb_ref[pl.ds(steps - 1 - t, n_seq, stride=pitch), :] = g
    for b in range(half):
        of_ref[b] = ya_ref[b * pitch:b * pitch + steps, :]
        or_ref[b] = yb_ref[(half + b) * pitch:(half + b) * pitch + steps, :]


def _s5_scan(pf, bre, bim, cre, cim, lre, lim, *, n_batch, seq, ctx, width):
    steps = S5_STEPS
    nblk = width // LANES
    rows = steps * SUBLANES
    sw = bre.shape[-1]
    ncc = ctx // steps
    nlc = seq // steps
    ctx0 = n_batch * nlc

    def fwd_map(b):
        return lambda k, c: (jnp.where(c < ncc, ctx0 + b * ncc + c, b * nlc + (c - ncc)), k)

    def rev_map(b):
        return lambda k, c: (jnp.where(c < ncc, ctx0 + b * ncc + (ncc - 1 - c), b * nlc + (nlc - 1 - (c - ncc))), k)

    of_map = lambda k, c: (0, jnp.where(c < ncc, nlc + c, c - ncc), k)
    or_map = lambda k, c: (0, jnp.where(c < ncc, nlc + (ncc - 1 - c), nlc - 1 - (c - ncc)), k)
    u_specs = ([pl.BlockSpec((steps, LANES), fwd_map(b)) for b in range(n_batch)]
               + [pl.BlockSpec((steps, LANES), rev_map(b)) for b in range(n_batch)])
    out_sds = jax.ShapeDtypeStruct((n_batch, seq + ctx, width), F32)
    stage = pltpu.VMEM((SUBLANES * S5_PITCH, LANES), F32)
    return pl.pallas_call(
        _s5_scan_kernel,
        out_shape=(out_sds, out_sds),
        grid=(nblk, ncc + nlc),
        in_specs=u_specs + [
            pl.BlockSpec((2, 1, LANES, sw), lambda k, c: (0, k, 0, 0)),
            pl.BlockSpec((2, 1, LANES, sw), lambda k, c: (0, k, 0, 0)),
            pl.BlockSpec((2, 1, sw, LANES), lambda k, c: (0, k, 0, 0)),
            pl.BlockSpec((2, 1, sw, LANES), lambda k, c: (0, k, 0, 0)),
            pl.BlockSpec((1, SUBLANES, sw), lambda k, c: (k, 0, 0)),
            pl.BlockSpec((1, SUBLANES, sw), lambda k, c: (k, 0, 0))],
        out_specs=(pl.BlockSpec((n_batch, steps, LANES), of_map),
                   pl.BlockSpec((n_batch, steps, LANES), or_map)),
        scratch_shapes=[pltpu.VMEM((SUBLANES, sw), F32), pltpu.VMEM((SUBLANES, sw), F32),
                        stage, pltpu.VMEM((rows, LANES), F32),
                        pltpu.VMEM((rows, sw), F32), pltpu.VMEM((rows, sw), F32),
                        pltpu.VMEM((rows, sw), F32), pltpu.VMEM((rows, sw), F32),
                        pltpu.VMEM((rows, LANES), F32), stage, stage],
        compiler_params=_cparams(("arbitrary", "arbitrary")),
        name="s5_scan",
    )(*([pf] * (2 * n_batch)), bre, bim, cre, cim, lre, lim)


def _s5_post_kernel(pf_ref, yf_ref, yr_ref, d_ref, w_ref, b_ref, o_ref, *, width):
    u = pf_ref[:, 0:width]
    y = d_ref[...] * u + yf_ref[0] + yr_ref[0]
    k0 = math.sqrt(2.0 / math.pi)
    g = 0.5 * y * (1.0 + jnp.tanh(k0 * (y + 0.044715 * (y * y * y))))
    z = _dot(g.astype(BF16), w_ref[...]) + b_ref[...]
    o_ref[...] = (g * jax.nn.sigmoid(z)).astype(BF16)


def _s5_post(pf, yf, yr, d_skip, glu_w, glu_b, *, n_batch, seq, ctx):
    n_rows = pf.shape[0]
    width = yf.shape[2]
    tp = ctx
    nlt = seq // tp
    kern = functools.partial(_s5_post_kernel, width=width)
    tok_map = lambda b, j: (jnp.where(j < nlt, b * nlt + j, n_batch * nlt + b), 0)
    return pl.pallas_call(
        kern,
        out_shape=jax.ShapeDtypeStruct((n_rows, width), BF16),
        grid=(n_batch, nlt + 1),
        in_specs=[pl.BlockSpec((tp, pf.shape[1]), tok_map),
                  pl.BlockSpec((1, tp, width), lambda b, j: (b, j, 0)),
                  pl.BlockSpec((1, tp, width), lambda b, j: (b, j, 0)),
                  _const_spec((1, width)), _const_spec(glu_w.shape), _const_spec((1, width))],
        out_specs=pl.BlockSpec((tp, width), tok_map),
        compiler_params=_cparams(("arbitrary", "arbitrary")),
        name="s5_post",
    )(pf, yf, yr, d_skip, glu_w, glu_b)


def _merge_kernel(x_ref, mod_ref, nw_ref, y0_ref, y1_ref, y2_ref, y3_ref, wg_ref, wb_ref, wo_ref, o_ref, acc_ref):
    x = x_ref[...]
    m = mod_ref[0]
    n = _rms_mod(x, nw_ref[...], m[3:4], m[4:5]).astype(BF16)
    ys = (y0_ref, y1_ref, y2_ref, y3_ref)
    for b in range(N_BRANCH):
        gate = jax.nn.sigmoid(_dot(n, wg_ref[b]))
        contrib = gate * _dot(ys[b][...], wb_ref[b])
        if b == 0:
            acc_ref[...] = contrib
        else:
            acc_ref[...] += contrib
    o_ref[...] = x + m[5:6] * _dot(acc_ref[...].astype(BF16), wo_ref[...])


def _merge(h, n_rows, mod, norm_w, ys, wg, wb, wo, *, n_lat_tiles, tiles_per_batch, n_batch):
    d = h.shape[1]
    bw = wb.shape[1]
    return pl.pallas_call(
        _merge_kernel,
        out_shape=jax.ShapeDtypeStruct((n_rows, d), F32),
        grid=(n_rows // TM,),
        in_specs=[pl.BlockSpec((TM, d), lambda i: (i, 0)),
                  pl.BlockSpec((1, N_MOD, d), lambda i: (_mod_row_map(n_lat_tiles, tiles_per_batch, n_batch)(i), 0, 0)),
                  _const_spec((1, d))]
                 + [pl.BlockSpec((TM, bw), lambda i: (i, 0))] * N_BRANCH
                 + [_const_spec(wg.shape), _const_spec(wb.shape), _const_spec(wo.shape)],
        out_specs=pl.BlockSpec((TM, d), lambda i: (i, 0)),
        scratch_shapes=[pltpu.VMEM((TM, d), F32)],
        compiler_params=_cparams(("arbitrary",)),
        name="merge",
    )(h, mod, norm_w, *ys, wg, wb, wo)


def _swap_rot_pairs(w, nf):
    lead = w.shape[:-1]
    n = w.shape[-1]
    return w.reshape(lead + (n // (2 * nf), 2, nf))[..., ::-1, :].reshape(lead + (n,))


def _rope_tables(seq, dim, lane_off, width, period):
    nf = dim // 4
    pos = jnp.arange(seq)
    rows = (pos // GRID_W).astype(F32)
    cols = (pos % GRID_W).astype(F32)
    inv_freq = ROPE_BASE ** (-jnp.arange(nf, dtype=F32) / nf)
    ang_r = rows[:, None] * inv_freq[None, :]
    ang_c = cols[:, None] * inv_freq[None, :]
    cos = jnp.concatenate([jnp.cos(ang_r)] * 2 + [jnp.cos(ang_c)] * 2, axis=1)
    sin = jnp.concatenate([-jnp.sin(ang_r), jnp.sin(ang_r), -jnp.sin(ang_c), jnp.sin(ang_c)], axis=1)
    c_per = jnp.ones((seq, period), F32).at[:, lane_off:lane_off + dim].set(cos)
    s_per = jnp.zeros((seq, period), F32).at[:, lane_off:lane_off + dim].set(sin)
    reps = width // period
    c_tab = jnp.concatenate([jnp.tile(c_per, (1, reps)), jnp.ones((TM, width), F32)], axis=0)
    s_tab = jnp.concatenate([jnp.tile(s_per, (1, reps)), jnp.zeros((TM, width), F32)], axis=0)
    return c_tab, s_tab


def _inproj_weights(w_in):
    d = w_in.shape[0]
    hw = NA_HEADS * HEAD_DIM
    o_sq = 3 * hw
    o_sk = o_sq + SWA_HEADS * HEAD_DIM
    o_sv = o_sk + SWA_KV_HEADS * HEAD_DIM
    o_s5 = o_sv + SWA_KV_HEADS * HEAD_DIM
    s5w = 512
    o_cq = o_s5 + s5w
    o_ckv = o_cq + 256
    o_kr = o_ckv + 128
    o_g = o_kr + MLA_ROPE
    qscale = HEAD_DIM ** -0.5
    na = jnp.concatenate([w_in[:, :hw] * qscale, w_in[:, hw:3 * hw]], axis=1)
    sq = w_in[:, o_sq:o_sk] * qscale
    sk = w_in[:, o_sk:o_sv]
    sv = w_in[:, o_sv:o_s5]
    def dup_heads(a):
        return jnp.concatenate([a[:, kv * HEAD_DIM:(kv + 1) * HEAD_DIM]
                                for kv in range(SWA_KV_HEADS) for _ in range(2)], axis=1)

    sk_dup = dup_heads(sk)
    sv_dup = dup_heads(sv)
    wr = jnp.concatenate([sq, sk_dup], axis=1)
    wrs = _swap_rot_pairs(wr, HEAD_DIM // 4)
    wf = w_in[:, o_s5:o_kr]
    kr = w_in[:, o_kr:o_g]
    lpad = jnp.zeros((d, MLA_NOPE), F32)
    rpad = jnp.zeros((d, LANES - MLA_NOPE - MLA_ROPE), F32)
    wk = jnp.concatenate([lpad, kr, rpad], axis=1)
    wks = jnp.concatenate([lpad, _swap_rot_pairs(kr, MLA_ROPE // 4), rpad], axis=1)
    wg = w_in[:, o_g:].reshape(d, N_BRANCH, d).transpose(1, 0, 2)
    proj = tuple(a.astype(BF16) for a in (na, wr, wrs, sv_dup, wf, wk, wks))
    return proj, wg.astype(BF16)


def _mla_weights(w_uq, w_ukv):
    ql = w_uq.shape[0]
    kvl = w_ukv.shape[0]
    dq = MLA_NOPE + MLA_ROPE
    wq3 = w_uq.reshape(ql, MLA_HEADS, dq)
    pad = jnp.zeros((ql, MLA_HEADS, LANES - dq), F32)
    wq = jnp.concatenate([wq3, pad], axis=2).reshape(ql, MLA_HEADS * LANES)
    rope_sw = _swap_rot_pairs(wq3[:, :, MLA_NOPE:], MLA_ROPE // 4)
    wqs3 = jnp.concatenate([jnp.zeros((ql, MLA_HEADS, MLA_NOPE), F32), rope_sw, pad], axis=2)
    wqs = wqs3.reshape(ql, MLA_HEADS * LANES)
    wkv3 = w_ukv.reshape(kvl, MLA_HEADS, MLA_NOPE + MLA_V)
    wk = jnp.concatenate([wkv3[:, :, :MLA_NOPE], jnp.zeros((kvl, MLA_HEADS, LANES - MLA_NOPE), F32)], axis=2)
    wk = wk.reshape(kvl, MLA_HEADS * LANES)
    wv = wkv3[:, :, MLA_NOPE:].reshape(kvl, MLA_HEADS * MLA_V)
    return tuple(a.astype(BF16) for a in (wq, wqs, wk, wv))


def _s5_params(lam_re, lam_im, log_dt, b_re, b_im, c_re, c_im):
    lam = lax.complex(lam_re.astype(F32), lam_im.astype(F32))
    dt = jnp.exp(log_dt.astype(F32))[..., None]
    lam_bar = jnp.exp(lam * dt)
    b = lax.complex(b_re.astype(F32), b_im.astype(F32))
    b_bar = ((lam_bar - 1.0) / lam)[..., None] * b
    n_dir, g, p, cg = b_re.shape
    gpb = LANES // cg
    nblk = g // gpb
    eye = jnp.eye(gpb, dtype=F32)

    def in_map(x):
        x5 = x.reshape(n_dir, nblk, gpb, p, cg)
        return jnp.einsum('dkgpc,gh->dkgchp', x5, eye).reshape(n_dir, nblk, gpb * cg, gpb * p)

    def out_map(x):
        x5 = x.reshape(n_dir, nblk, gpb, cg, p)
        return jnp.einsum('dkgcp,gh->dkgphc', x5, eye).reshape(n_dir, nblk, gpb * p, gpb * cg)

    bre = in_map(jnp.real(b_bar)).astype(BF16)
    bim = in_map(jnp.imag(b_bar)).astype(BF16)
    cre = out_map(c_re.astype(F32)).astype(BF16)
    cim = out_map(-c_im.astype(F32)).astype(BF16)
    half = SUBLANES // 2

    def lam_rows(x):
        x3 = x.reshape(n_dir, nblk, gpb * p)
        return jnp.concatenate([jnp.broadcast_to(x3[0][:, None, :], (nblk, half, gpb * p)),
                                jnp.broadcast_to(x3[1][:, None, :], (nblk, half, gpb * p))], axis=1)

    return bre, bim, cre, cim, lam_rows(jnp.real(lam_bar)), lam_rows(jnp.imag(lam_bar))


def _ffn_weights(wg, wu, wd):
    d, ff = wg.shape
    nch = ff // FF_CHUNK
    wg3 = wg.reshape(d, nch, FF_CHUNK).transpose(1, 0, 2).astype(BF16)
    wu3 = wu.reshape(d, nch, FF_CHUNK).transpose(1, 0, 2).astype(BF16)
    wd3 = wd.reshape(nch, FF_CHUNK, d).astype(BF16)
    return wg3, wu3, wd3


def kernel(x, c, ctx, c_ctx, ada_w, ada_b, ffn1_norm, ffn1_w_gate, ffn1_w_up, ffn1_w_down, mix_norm, w_in, na_rpb, swa_sink, s5_lambda_re, s5_lambda_im, s5_log_dt, s5_b_re, s5_b_im, s5_c_re, s5_c_im, s5_d, s5_glu_w, s5_glu_b, mla_q_norm, mla_w_uq, mla_kv_norm, mla_w_ukv, w_branch, w_out, ffn2_norm, ffn2_w_gate, ffn2_w_up, ffn2_w_down, final_norm):
    n_batch, seq, d = x.shape
    n_ctx = ctx.shape[1]
    depth = ada_w.shape[0]
    assert 2 * n_batch == SUBLANES and seq % TM == 0 and (n_batch * n_ctx) % TM == 0
    n_lat = n_batch * seq
    n_all = n_lat + n_batch * n_ctx
    tiles_per_batch = seq // TM
    n_lat_tiles = n_lat // TM
    geo = dict(n_lat_tiles=n_lat_tiles, tiles_per_batch=tiles_per_batch, n_batch=n_batch)

    h = jnp.concatenate([x.reshape(n_lat, d), ctx.reshape(n_batch * n_ctx, d)], axis=0)
    cc = jnp.concatenate([c, c_ctx[None, :], jnp.zeros((SUBLANES - n_batch - 1, d), F32)], axis=0)
    mod = _ada_mod(cc, ada_w, ada_b)

    cs_sw, sn_sw = _rope_tables(seq, HEAD_DIM, 0, 2 * LANES, HEAD_DIM)
    ck_kr, sk_kr = _rope_tables(seq, MLA_ROPE, MLA_NOPE, LANES, LANES)
    rows_n = seq // GRID_W
    s5w = s5_d.shape[1]
    offs = (s5w, s5w + mla_w_uq.shape[1], s5w + mla_w_uq.shape[1] + mla_w_ukv.shape[1])
    sw_col0 = 3 * NA_HEADS * HEAD_DIM

    for l in range(depth):
        need_ctx = l < depth - 1
        last = l == depth - 1
        ml = mod[l]
        ones = jnp.ones((1, d), F32)
        wg3, wu3, wd3 = _ffn_weights(ffn1_w_gate[l], ffn1_w_up[l], ffn1_w_down[l])
        h = _ffn(h, n_all, ml, ffn1_norm[l][None, :], wg3, wu3, wd3, ones, base=0, final=False, **geo)
        proj_w, gate_w = _inproj_weights(w_in[l])
        pb, pf = _inproj(h, ml, mix_norm[l][None, :], proj_w, (cs_sw, sn_sw, ck_kr, sk_kr), **geo)
        bias = _na_bias_table(na_rpb[l].astype(F32), rows_n)
        y_na = _na_attention(pb, bias, n_batch=n_batch, seq=seq, ctx=n_ctx, need_ctx=need_ctx)
        y_sw_l, y_sw_c = _swa_attention(pb, swa_sink[l].astype(F32), n_batch=n_batch, seq=seq, ctx=n_ctx,
                                        need_ctx=need_ctx, col0=sw_col0)
        y_sw = jnp.concatenate([y_sw_l, y_sw_c], axis=0) if need_ctx else y_sw_l
        wq, wqs, wk, wv = _mla_weights(mla_w_uq[l], mla_w_ukv[l])
        qm, km, vm = _mla_prep(pf, mla_q_norm[l][None, :], mla_kv_norm[l][None, :], wq, wqs, wk, wv, ck_kr, sk_kr,
                               n_lat_tiles=n_lat_tiles, tiles_per_batch=tiles_per_batch, offs=offs)
        y_mla = _mla_attention(qm, km, vm, n_batch=n_batch, seq=seq, ctx=n_ctx, need_ctx=need_ctx)
        s5p = _s5_params(s5_lambda_re[l], s5_lambda_im[l], s5_log_dt[l], s5_b_re[l], s5_b_im[l],
                         s5_c_re[l], s5_c_im[l])
        yf, yr = _s5_scan(pf, *s5p, n_batch=n_batch, seq=seq, ctx=n_ctx, width=s5w)
        y_s5 = _s5_post(pf, yf, yr, s5_d[l][None, :].astype(F32), s5_glu_w[l].astype(BF16),
                        s5_glu_b[l][None, :].astype(F32), n_batch=n_batch, seq=seq, ctx=n_ctx)
        n_rows = n_all if need_ctx else n_lat
        h = _merge(h, n_rows, ml, mix_norm[l][None, :], (y_na, y_sw, y_s5, y_mla), gate_w,
                   w_branch[l].astype(BF16), w_out[l].astype(BF16), **geo)
        wg3, wu3, wd3 = _ffn_weights(ffn2_w_gate[l], ffn2_w_up[l], ffn2_w_down[l])
        h = _ffn(h, n_rows, ml, ffn2_norm[l][None, :], wg3, wu3, wd3, final_norm[None, :], base=6, final=last, **geo)
    return h.reshape(n_batch, seq, d)
```

```python
import functools
import math

import numpy as np
import jax
import jax.numpy as jnp
from jax import lax
from jax.experimental import pallas as pl
from jax.experimental.pallas import tpu as pltpu

F32 = jnp.float32
BF16 = jnp.bfloat16

GRID_W = 64
HEAD_DIM = 64
N_BRANCH = 4
NA_HEADS = 8
NA_WIN_ROWS = 8
NA_WIN_COLS = 16
SWA_HEADS = 8
SWA_KV_HEADS = 2
SWA_WINDOW = 128
S5_GROUP = 16
S5_STATE = 64
MLA_HEADS = 8
MLA_NOPE = 64
MLA_ROPE = 32
MLA_V = 64
MACARON_WEIGHT = 0.5
ROPE_BASE = 10000.0
EPS = 1e-6
N_MOD = 9

LANES = 128
SUBLANES = 8
TM = 512
FF_CHUNK = 256
NA_QROWS = 4
NA_KROWS = NA_QROWS + NA_WIN_ROWS - 1
SWA_BLK = 128
MLA_TQ = 512
MLA_SUB = 256
S5_STEPS = 128
S5_PITCH = S5_STEPS + SUBLANES
NEG = -1e30
VMEM_LIMIT = 56 * 1024 * 1024


def _cparams(sem):
    return pltpu.CompilerParams(dimension_semantics=sem, vmem_limit_bytes=VMEM_LIMIT)


def _const_spec(shape):
    nd = len(shape)
    return pl.BlockSpec(shape, lambda *_: (0,) * nd, pipeline_mode=pl.Buffered(1))


def _dot(a, b):
    return jnp.dot(a, b, preferred_element_type=F32)


def _dot_t(a, b):
    return lax.dot_general(a, b, (((1,), (1,)), ((), ())), preferred_element_type=F32)


def _rms(x, w):
    return x * lax.rsqrt(jnp.mean(x * x, axis=-1, keepdims=True) + EPS) * w


def _rms_mod(x, w, shift, scale):
    return _rms(x, w) * (1.0 + scale) + shift


def _mod_row_map(n_lat_tiles, tiles_per_batch, n_batch):
    def f(i):
        return jnp.where(i < n_lat_tiles, i // tiles_per_batch, n_batch)
    return f


def _ada_kernel(c_ref, w_ref, b_ref, o_ref):
    c = c_ref[...]
    s = c * jax.nn.sigmoid(c)
    w = w_ref[0]
    s_hi = s.astype(BF16)
    s_lo = (s - s_hi.astype(F32)).astype(BF16)
    w_hi = w.astype(BF16)
    w_lo = (w - w_hi.astype(F32)).astype(BF16)
    o_ref[0] = _dot(s_hi, w_hi) + (_dot(s_hi, w_lo) + _dot(s_lo, w_hi)) + b_ref[0]


def _ada_mod(cc, ada_w, ada_b):
    depth, d, nd = ada_w.shape
    tn = 1024
    out = pl.pallas_call(
        _ada_kernel,
        out_shape=jax.ShapeDtypeStruct((depth, SUBLANES, nd), F32),
        grid=(depth, nd // tn),
        in_specs=[pl.BlockSpec((SUBLANES, d), lambda l, j: (0, 0)),
                  pl.BlockSpec((1, d, tn), lambda l, j: (l, 0, j)),
                  pl.BlockSpec((1, 1, tn), lambda l, j: (l, 0, j))],
        out_specs=pl.BlockSpec((1, SUBLANES, tn), lambda l, j: (l, 0, j)),
        compiler_params=_cparams(("arbitrary", "arbitrary")),
        name="ada_mod",
    )(cc, ada_w, ada_b.reshape(depth, 1, nd))
    return out.reshape(depth, SUBLANES, N_MOD, d)


def _ffn_kernel(x_ref, mod_ref, nw_ref, wg_ref, wu_ref, wd_ref, fw_ref, o_ref, *, base, final):
    x = x_ref[...]
    m = mod_ref[0]
    n = _rms_mod(x, nw_ref[...], m[base:base + 1], m[base + 1:base + 2]).astype(BF16)
    ff = wg_ref.shape[1]
    fc = FF_CHUNK if ff % FF_CHUNK == 0 else ff
    acc = None
    for c in range(ff // fc):
        cs = slice(c * fc, (c + 1) * fc)
        g = _dot(n, wg_ref[:, cs])
        u = _dot(n, wu_ref[:, cs])
        a = (g * jax.nn.sigmoid(g) * u).astype(BF16)
        y = _dot(a, wd_ref[cs, :])
        acc = y if acc is None else acc + y
    out = x + MACARON_WEIGHT * m[base + 2:base + 3] * acc
    if final:
        out = _rms(out, fw_ref[...])
    o_ref[...] = out


def _ffn(h, n_rows, mod, norm_w, wg, wu, wd, final_w, *, base, final, n_lat_tiles, tiles_per_batch, n_batch):
    d = h.shape[1]
    ff = wg.shape[1]
    kern = functools.partial(_ffn_kernel, base=base, final=final)
    return pl.pallas_call(
        kern,
        out_shape=jax.ShapeDtypeStruct((n_rows, d), F32),
        grid=(n_rows // TM,),
        in_specs=[pl.BlockSpec((TM, d), lambda i: (i, 0)),
                  pl.BlockSpec((1, N_MOD, d), lambda i: (_mod_row_map(n_lat_tiles, tiles_per_batch, n_batch)(i), 0, 0)),
                  _const_spec((1, d)),
                  _const_spec((d, ff)), _const_spec((d, ff)), _const_spec((ff, d)),
                  _const_spec((1, d))],
        out_specs=pl.BlockSpec((TM, d), lambda i: (i, 0)),
        compiler_params=_cparams(("arbitrary",)),
        name="ffn_final" if final else "ffn",
    )(h, mod, norm_w, wg, wu, wd, final_w)


def _rope_apply(y, cs, sn, nf):
    w = y.shape[1]
    first = (lax.broadcasted_iota(jnp.int32, y.shape, 1) & nf) == 0
    ysw = jnp.where(first, pltpu.roll(y, w - nf, 1), pltpu.roll(y, nf, 1))
    return y * cs + ysw * sn


def _inproj_kernel(x_ref, mod_ref, nw_ref, wa_ref, wr_ref, wv_ref, wf_ref,
                   cs_ref, sn_ref, ck_ref, sk_ref, ob_ref, of_ref):
    x = x_ref[...]
    m = mod_ref[0]
    n = _rms_mod(x, nw_ref[...], m[3:4], m[4:5]).astype(BF16)
    na = wa_ref.shape[1]
    nr = wr_ref.shape[1]
    nv = wv_ref.shape[1]
    nf = wf_ref.shape[1]
    cw = 2 * LANES
    for c in range(na // cw):
        ob_ref[:, c * cw:(c + 1) * cw] = _dot(n, wa_ref[:, c * cw:(c + 1) * cw]).astype(BF16)
    for c in range(nr // cw):
        y = _dot(n, wr_ref[:, c * cw:(c + 1) * cw])
        ob_ref[:, na + c * cw:na + (c + 1) * cw] = _rope_apply(y, cs_ref[...], sn_ref[...], HEAD_DIM // 4).astype(BF16)
    for c in range(nv // cw):
        ob_ref[:, na + nr + c * cw:na + nr + (c + 1) * cw] = _dot(n, wv_ref[:, c * cw:(c + 1) * cw]).astype(BF16)
    for c in range(nf // cw - 1):
        of_ref[:, c * cw:(c + 1) * cw] = _dot(n, wf_ref[:, c * cw:(c + 1) * cw])
    y = _dot(n, wf_ref[:, nf - cw:nf])
    of_ref[:, nf - cw:nf] = _rope_apply(y, ck_ref[...], sk_ref[...], MLA_ROPE // 4)


def _inproj(h, mod, norm_w, w, tabs, *, n_lat_tiles, tiles_per_batch, n_batch):
    n_rows, d = h.shape
    wa, wr, wv, wf = w
    cs, sn, ck, sk = tabs
    nb = wa.shape[1] + wr.shape[1] + wv.shape[1]
    nf = wf.shape[1]
    tab_map = lambda i: (jnp.where(i < n_lat_tiles, i % tiles_per_batch, tiles_per_batch), 0)
    return pl.pallas_call(
        _inproj_kernel,
        out_shape=(jax.ShapeDtypeStruct((n_rows, nb), BF16), jax.ShapeDtypeStruct((n_rows, nf), F32)),
        grid=(n_rows // TM,),
        in_specs=[pl.BlockSpec((TM, d), lambda i: (i, 0)),
                  pl.BlockSpec((1, N_MOD, d), lambda i: (_mod_row_map(n_lat_tiles, tiles_per_batch, n_batch)(i), 0, 0)),
                  _const_spec((1, d))]
                 + [_const_spec(a.shape) for a in w]
                 + [pl.BlockSpec((TM, 2 * LANES), tab_map)] * 4,
        out_specs=(pl.BlockSpec((TM, nb), lambda i: (i, 0)), pl.BlockSpec((TM, nf), lambda i: (i, 0))),
        compiler_params=_cparams(("arbitrary",)),
        name="inproj",
    )(h, mod, norm_w, *w, cs, sn, ck, sk)


def _lane_half_masks(rows):
    lane = lax.broadcasted_iota(jnp.int32, (rows, LANES), 1)
    return lane < HEAD_DIM


def _softmax2(s1, s2, sink=None):
    m = jnp.maximum(jnp.max(s1, axis=-1, keepdims=True), jnp.max(s2, axis=-1, keepdims=True))
    if sink is not None:
        m = jnp.maximum(m, sink)
    p1 = jnp.exp(s1 - m)
    p2 = jnp.exp(s2 - m)
    l = jnp.sum(p1, axis=-1, keepdims=True) + jnp.sum(p2, axis=-1, keepdims=True)
    if sink is not None:
        l = l + jnp.exp(sink - m)
    return p1, p2, l


def _na_kernel(q_ref, k_ref, v_ref, kc_ref, vc_ref, bias_ref, o_ref, *, n_blk, rows_n):
    blk = pl.program_id(1)
    nq = NA_QROWS * GRID_W
    nk = NA_KROWS * GRID_W
    ws = jnp.clip(NA_QROWS * blk - NA_WIN_ROWS // 2, 0, rows_n - NA_KROWS)
    r0 = pl.multiple_of(ws * GRID_W, GRID_W)
    lo = _lane_half_masks(nq)
    for j in range(NA_HEADS // 2):
        sl = slice(j * LANES, (j + 1) * LANES)
        qp = q_ref[:, sl]
        kw = k_ref[pl.ds(r0, nk), sl]
        vw = v_ref[pl.ds(r0, nk), sl]
        kc = kc_ref[:, sl]
        vc = vc_ref[:, sl]
        zero = jnp.zeros_like(qp)
        qs = jnp.concatenate([jnp.where(lo, qp, zero), jnp.where(lo, zero, qp)], axis=0)
        s1 = _dot_t(qs, kw) + bias_ref[0, j]
        s2 = _dot_t(qs, kc)
        p1, p2, l = _softmax2(s1, s2)
        o = (_dot(p1.astype(BF16), vw) + _dot(p2.astype(BF16), vc)) / l
        o_ref[:, sl] = jnp.where(lo, o[:nq], o[nq:]).astype(BF16)


def _na_attention(pb, bias, *, n_batch, seq, ctx, need_ctx):
    rows_n = seq // GRID_W
    n_blk = rows_n // NA_QROWS
    nq = NA_QROWS * GRID_W
    hw = NA_HEADS * HEAD_DIM
    n_q = n_blk + (1 if need_ctx else 0)
    assert ctx == nq
    lat_blocks = n_batch * seq // nq
    qmap = lambda b, i: (jnp.where(i < n_blk, b * n_blk + i, lat_blocks + b), 0)

    def bias_map(b, i):
        t = jnp.where(i == 0, 0, jnp.where(i == 1, 1, jnp.where(i == n_blk - 1, 3, jnp.where(i == n_blk, 4, 2))))
        return (t, 0, 0, 0)

    kern = functools.partial(_na_kernel, n_blk=n_blk, rows_n=rows_n)
    return pl.pallas_call(
        kern,
        out_shape=jax.ShapeDtypeStruct((n_batch * n_q * nq, hw), BF16),
        grid=(n_batch, n_q),
        in_specs=[pl.BlockSpec((nq, hw), qmap),
                  pl.BlockSpec((seq, hw), lambda b, i: (b, 1)),
                  pl.BlockSpec((seq, hw), lambda b, i: (b, 2)),
                  pl.BlockSpec((ctx, hw), lambda b, i: (n_batch * seq // ctx + b, 1)),
                  pl.BlockSpec((ctx, hw), lambda b, i: (n_batch * seq // ctx + b, 2)),
                  pl.BlockSpec((1, NA_HEADS // 2, 2 * nq, NA_KROWS * GRID_W), bias_map)],
        out_specs=pl.BlockSpec((nq, hw), qmap),
        compiler_params=_cparams(("arbitrary", "arbitrary")),
        name="na_attn",
    )(pb, pb, pb, pb, pb, bias)


def _na_bias_table(rpb, rows_n):
    n_blk = rows_n // NA_QROWS
    kr_n = min(NA_WIN_ROWS, rows_n)
    n_heads = rpb.shape[0]
    col = np.arange(GRID_W)
    c0 = np.clip(col - NA_WIN_COLS // 2, 0, GRID_W - NA_WIN_COLS)
    col_ok = (col[None, :] >= c0[:, None]) & (col[None, :] < c0[:, None] + NA_WIN_COLS)
    padded = jnp.pad(rpb, ((0, 0), (0, 0), (GRID_W, GRID_W)))
    off = GRID_W + NA_WIN_COLS - 1
    toep = jnp.stack([padded[:, :, off - qc:off - qc + GRID_W] for qc in range(GRID_W)], axis=2)
    toep = jnp.where(col_ok[None, None], toep, NEG)
    masked = jnp.full((n_heads, GRID_W, GRID_W), NEG, F32)
    tabs = []
    for blk in (0, 1, 2, n_blk - 1):
        ws = int(np.clip(NA_QROWS * blk - NA_WIN_ROWS // 2, 0, rows_n - NA_KROWS))
        per_q = []
        for qr in range(NA_QROWS):
            r = NA_QROWS * blk + qr
            r0 = int(np.clip(r - kr_n // 2, 0, rows_n - kr_n))
            per_k = []
            for kk in range(NA_KROWS):
                kr = ws + kk
                ok = r0 <= kr < r0 + kr_n
                per_k.append(toep[:, kr - r + NA_WIN_ROWS - 1] if ok else masked)
            per_q.append(jnp.stack(per_k, axis=2))
        t = jnp.stack(per_q, axis=1)
        tabs.append(t.reshape(n_heads, NA_QROWS * GRID_W, NA_KROWS * GRID_W))
    tabs.append(jnp.full_like(tabs[0], NEG))
    return jnp.stack(tabs).reshape(len(tabs), n_heads // 2, 2 * NA_QROWS * GRID_W, NA_KROWS * GRID_W)


def _swa_kernel(sink_ref, q_ref, k_ref, v_ref, qc_ref, kc_ref, vc_ref, o_ref, oc_ref, *, seq, need_ctx):
    n_blk = seq // SWA_BLK
    band = 3 * SWA_BLK
    group = SWA_HEADS // SWA_KV_HEADS
    pairs = group // 2

    def group_attn(q_slabs, kv, kb, vb, kc, vc, mask_bias):
        r = q_slabs[0].shape[0]
        lo = _lane_half_masks(r)
        parts = []
        for qp in q_slabs:
            zero = jnp.zeros_like(qp)
            parts += [jnp.where(lo, qp, zero), jnp.where(lo, zero, qp)]
        qs = jnp.concatenate(parts, axis=0)
        row = lax.broadcasted_iota(jnp.int32, (group * r, 1), 0)
        sink = jnp.full((group * r, 1), sink_ref[kv * group + group - 1], F32)
        for g in range(group - 2, -1, -1):
            sink = jnp.where(row < (g + 1) * r, sink_ref[kv * group + g], sink)
        s2 = _dot_t(qs, kc)
        if kb is None:
            m = jnp.maximum(jnp.max(s2, axis=-1, keepdims=True), sink)
            p2 = jnp.exp(s2 - m)
            l = jnp.sum(p2, axis=-1, keepdims=True) + jnp.exp(sink - m)
            o = _dot(p2.astype(BF16), vc) / l
        else:
            s1 = _dot_t(qs, kb) + mask_bias
            m = jnp.maximum(jnp.maximum(jnp.max(s1, axis=-1, keepdims=True), jnp.max(s2, axis=-1, keepdims=True)), sink)
            p1 = jnp.exp(s1 - m)
            p2 = jnp.exp(s2 - m)
            l = jnp.sum(p1, axis=-1, keepdims=True) + jnp.sum(p2, axis=-1, keepdims=True) + jnp.exp(sink - m)
            o = (_dot(p1.astype(BF16), vb) + _dot(p2.astype(BF16), vc)) / l
        return [jnp.where(lo, o[(2 * i) * r:(2 * i + 1) * r], o[(2 * i + 1) * r:(2 * i + 2) * r]) for i in range(pairs)]

    qi = lax.broadcasted_iota(jnp.int32, (group * SWA_BLK, band), 0) & (SWA_BLK - 1)
    ki = lax.broadcasted_iota(jnp.int32, (group * SWA_BLK, band), 1)

    def blk_body(n, carry):
        start = jnp.clip((n - 1) * SWA_BLK, 0, seq - band)
        start = pl.multiple_of(start, SWA_BLK)
        q0 = pl.multiple_of(n * SWA_BLK, SWA_BLK)
        delta = (start + ki) - (q0 + qi)
        mask_bias = jnp.where(jnp.abs(delta) <= SWA_WINDOW, 0.0, NEG).astype(F32)
        for kv in range(SWA_KV_HEADS):
            ksl = slice(kv * LANES, (kv + 1) * LANES)
            slabs = [q_ref[pl.ds(q0, SWA_BLK), (kv * pairs + i) * LANES:(kv * pairs + i + 1) * LANES]
                     for i in range(pairs)]
            outs = group_attn(slabs, kv, k_ref[pl.ds(start, band), ksl], v_ref[pl.ds(start, band), ksl],
                              kc_ref[:, ksl], vc_ref[:, ksl], mask_bias)
            for i in range(pairs):
                o_ref[pl.ds(q0, SWA_BLK), (kv * pairs + i) * LANES:(kv * pairs + i + 1) * LANES] = outs[i].astype(BF16)
        return carry

    lax.fori_loop(0, n_blk, blk_body, 0)

    if need_ctx:
        for kv in range(SWA_KV_HEADS):
            ksl = slice(kv * LANES, (kv + 1) * LANES)
            slabs = [qc_ref[:, (kv * pairs + i) * LANES:(kv * pairs + i + 1) * LANES] for i in range(pairs)]
            outs = group_attn(slabs, kv, None, None, kc_ref[:, ksl], vc_ref[:, ksl], None)
            for i in range(pairs):
                oc_ref[:, (kv * pairs + i) * LANES:(kv * pairs + i + 1) * LANES] = outs[i].astype(BF16)
    else:
        oc_ref[...] = jnp.zeros_like(oc_ref)


def _swa_attention(pb, sink, *, n_batch, seq, ctx, need_ctx, col0):
    hw = SWA_HEADS * HEAD_DIM
    kw = SWA_KV_HEADS * LANES
    qcol = col0 // hw
    kcol = (col0 + hw) // kw
    vcol = kcol + 1
    cblk = n_batch * seq // ctx
    kern = functools.partial(_swa_kernel, seq=seq, need_ctx=need_ctx)
    return pl.pallas_call(
        kern,
        out_shape=(jax.ShapeDtypeStruct((n_batch * seq, hw), BF16),
                   jax.ShapeDtypeStruct((n_batch * ctx, hw), BF16)),
        grid=(n_batch,),
        in_specs=[pl.BlockSpec(memory_space=pltpu.SMEM),
                  pl.BlockSpec((seq, hw), lambda b: (b, qcol)),
                  pl.BlockSpec((seq, kw), lambda b: (b, kcol)),
                  pl.BlockSpec((seq, kw), lambda b: (b, vcol)),
                  pl.BlockSpec((ctx, hw), lambda b: (cblk + b, qcol)),
                  pl.BlockSpec((ctx, kw), lambda b: (cblk + b, kcol)),
                  pl.BlockSpec((ctx, kw), lambda b: (cblk + b, vcol))],
        out_specs=(pl.BlockSpec((seq, hw), lambda b: (b, 0)),
                   pl.BlockSpec((ctx, hw), lambda b: (b, 0))),
        compiler_params=_cparams(("arbitrary",)),
        name="swa_attn",
    )(sink, pb, pb, pb, pb, pb, pb)


def _mla_prep_kernel(pf_ref, qn_ref, kn_ref, wq_ref, wk_ref, wv_ref, cq_ref, sq_ref,
                     q_ref, k_ref, v_ref, *, off_cq, off_ckv, off_kr, q_lora, kv_lora):
    cq = pf_ref[:, off_cq:off_cq + q_lora]
    ckv = pf_ref[:, off_ckv:off_ckv + kv_lora]
    kr = pf_ref[:, off_kr:off_kr + LANES]
    kr2 = jnp.concatenate([kr, kr], axis=1)
    cqn = _rms(cq, qn_ref[...]).astype(BF16)
    ckvn = _rms(ckv, kn_ref[...]).astype(BF16)
    cw = 2 * LANES
    for c in range(MLA_HEADS * LANES // cw):
        sl = slice(c * cw, (c + 1) * cw)
        q = _rope_apply(_dot(cqn, wq_ref[:, sl]), cq_ref[...], sq_ref[...], MLA_ROPE // 4)
        q_ref[:, sl] = q.astype(BF16)
        k_ref[:, sl] = (_dot(ckvn, wk_ref[:, sl]) + kr2).astype(BF16)
    for c in range(MLA_HEADS * MLA_V // cw):
        sl = slice(c * cw, (c + 1) * cw)
        v_ref[:, sl] = _dot(ckvn, wv_ref[:, sl]).astype(BF16)


def _mla_prep(pf, qn, kn, wq, wk, wv, cq_tab, sq_tab, *, n_lat_tiles, tiles_per_batch, offs):
    n_rows = pf.shape[0]
    off_cq, off_ckv, off_kr = offs
    tab_map = lambda i: (jnp.where(i < n_lat_tiles, i % tiles_per_batch, tiles_per_batch), 0)
    kern = functools.partial(_mla_prep_kernel, off_cq=off_cq, off_ckv=off_ckv, off_kr=off_kr,
                             q_lora=wq.shape[0], kv_lora=wk.shape[0])
    hq = MLA_HEADS * LANES
    hv = MLA_HEADS * MLA_V
    return pl.pallas_call(
        kern,
        out_shape=(jax.ShapeDtypeStruct((n_rows, hq), BF16), jax.ShapeDtypeStruct((n_rows, hq), BF16),
                   jax.ShapeDtypeStruct((n_rows, hv), BF16)),
        grid=(n_rows // TM,),
        in_specs=[pl.BlockSpec((TM, pf.shape[1]), lambda i: (i, 0)),
                  _const_spec(qn.shape), _const_spec(kn.shape),
                  _const_spec(wq.shape), _const_spec(wk.shape), _const_spec(wv.shape),
                  pl.BlockSpec((TM, 2 * LANES), tab_map), pl.BlockSpec((TM, 2 * LANES), tab_map)],
        out_specs=(pl.BlockSpec((TM, hq), lambda i: (i, 0)), pl.BlockSpec((TM, hq), lambda i: (i, 0)),
                   pl.BlockSpec((TM, hv), lambda i: (i, 0))),
        compiler_params=_cparams(("arbitrary",)),
        name="mla_prep",
    )(pf, qn, kn, wq, wk, wv, cq_tab, sq_tab)


def _mla_body(q_ref, k_ref, v_ref, kc_ref, vc_ref, o_ref):
    sub = MLA_SUB
    n_sub = q_ref.shape[0] // sub
    lo = _lane_half_masks(sub)
    units = [(r, half) for r in range(n_sub) for half in range(2)]

    def scores(r, half):
        sl = slice(half * LANES, (half + 1) * LANES)
        q = q_ref[r * sub:(r + 1) * sub, sl]
        s1 = None if k_ref is None else _dot_t(q, k_ref[:, sl])
        return s1, _dot_t(q, kc_ref[:, sl])

    def finish(s1, s2):
        if s1 is None:
            m = jnp.max(s2, axis=-1, keepdims=True)
            p2 = jnp.exp2(s2 - m)
            l = jnp.sum(p2, axis=-1, keepdims=True)
            return _dot(p2.astype(BF16), vc_ref[...]) / l
        m = jnp.maximum(jnp.max(s1, axis=-1, keepdims=True), jnp.max(s2, axis=-1, keepdims=True))
        p1 = jnp.exp2(s1 - m)
        p2 = jnp.exp2(s2 - m)
        l = jnp.sum(p1, axis=-1, keepdims=True) + jnp.sum(p2, axis=-1, keepdims=True)
        return (_dot(p1.astype(BF16), v_ref[...]) + _dot(p2.astype(BF16), vc_ref[...])) / l

    outs = {}
    pending = scores(*units[0])
    for idx, u in enumerate(units):
        following = scores(*units[idx + 1]) if idx + 1 < len(units) else None
        outs[u] = finish(*pending)
        pending = following
    for r in range(n_sub):
        o_ref[r * sub:(r + 1) * sub, :] = jnp.where(lo, outs[(r, 0)], outs[(r, 1)]).astype(BF16)


def _mla_kernel(q_ref, k_ref, v_ref, kc_ref, vc_ref, o_ref):
    _mla_body(q_ref, k_ref, v_ref, kc_ref, vc_ref, o_ref)


def _mla_ctx_kernel(q_ref, kc_ref, vc_ref, o_ref):
    _mla_body(q_ref, None, None, kc_ref, vc_ref, o_ref)


def _mla_attention(qm, km, vm, *, n_batch, seq, ctx, need_ctx):
    n_qt = seq // MLA_TQ
    cblk = n_batch * seq // ctx
    hv = MLA_HEADS * MLA_V
    y_lat = pl.pallas_call(
        _mla_kernel,
        out_shape=jax.ShapeDtypeStruct((n_batch * seq, hv), BF16),
        grid=(n_batch, MLA_HEADS // 2, n_qt),
        in_specs=[pl.BlockSpec((MLA_TQ, 2 * LANES), lambda b, p, i: (b * n_qt + i, p)),
                  pl.BlockSpec((seq, 2 * LANES), lambda b, p, i: (b, p)),
                  pl.BlockSpec((seq, LANES), lambda b, p, i: (b, p)),
                  pl.BlockSpec((ctx, 2 * LANES), lambda b, p, i: (cblk + b, p)),
                  pl.BlockSpec((ctx, LANES), lambda b, p, i: (cblk + b, p))],
        out_specs=pl.BlockSpec((MLA_TQ, LANES), lambda b, p, i: (b * n_qt + i, p)),
        compiler_params=_cparams(("arbitrary", "arbitrary", "arbitrary")),
        name="mla_attn",
    )(qm, km, vm, km, vm)
    if not need_ctx:
        return y_lat
    assert ctx % MLA_SUB == 0
    y_ctx = pl.pallas_call(
        _mla_ctx_kernel,
        out_shape=jax.ShapeDtypeStruct((n_batch * ctx, hv), BF16),
        grid=(n_batch, MLA_HEADS // 2),
        in_specs=[pl.BlockSpec((ctx, 2 * LANES), lambda b, p: (cblk + b, p)),
                  pl.BlockSpec((ctx, 2 * LANES), lambda b, p: (cblk + b, p)),
                  pl.BlockSpec((ctx, LANES), lambda b, p: (cblk + b, p))],
        out_specs=pl.BlockSpec((ctx, LANES), lambda b, p: (b, p)),
        compiler_params=_cparams(("arbitrary", "arbitrary")),
        name="mla_ctx_attn",
    )(qm, km, vm)
    return jnp.concatenate([y_lat, y_ctx], axis=0)


def _s5_scan_kernel(uf0_ref, uf1_ref, uf2_ref, uf3_ref, ur0_ref, ur1_ref, ur2_ref, ur3_ref,
                    bmat_ref, cmat_ref, lam_ref, of_ref, or_ref,
                    h_ref, stage_ref, lhs_ref, bu_ref, st_ref, ysc_ref, ya_ref, yb_ref):
    c = pl.program_id(1)
    n_seq = SUBLANES
    half = n_seq // 2
    steps = S5_STEPS
    pitch = S5_PITCH
    rows = steps * n_seq
    sw = h_ref.shape[1] // 2
    ufs = (uf0_ref, uf1_ref, uf2_ref, uf3_ref)
    urs = (ur0_ref, ur1_ref, ur2_ref, ur3_ref)
    is_fwd = (lax.broadcasted_iota(jnp.int32, (rows, 1), 0) % n_seq) < half

    @pl.when(c == 0)
    def _():
        h_ref[...] = jnp.zeros_like(h_ref)

    for b in range(half):
        stage_ref[b * pitch:b * pitch + steps, :] = ufs[b][...]
        stage_ref[(half + b) * pitch:(half + b) * pitch + steps, :] = urs[b][...]
    sub_fwd = lax.broadcasted_iota(jnp.int32, (n_seq, LANES), 0) < half
    for t in range(steps):
        ga = stage_ref[pl.ds(t, n_seq, stride=pitch), :]
        gb = stage_ref[pl.ds(steps - 1 - t, n_seq, stride=pitch), :]
        lhs_ref[t * n_seq:(t + 1) * n_seq, 0:LANES] = jnp.where(sub_fwd, ga, 0.0)
        lhs_ref[t * n_seq:(t + 1) * n_seq, LANES:2 * LANES] = jnp.where(sub_fwd, 0.0, gb)

    bu_ref[...] = _dot(lhs_ref[...].astype(BF16), bmat_ref[0])
    lre = lam_ref[0, :, 0:sw]
    lim = lam_ref[0, :, sw:2 * sw]

    def step(t, carry):
        hr, hi = carry
        r0 = pl.multiple_of(t * n_seq, n_seq)
        nr = lre * hr - lim * hi + bu_ref[pl.ds(r0, n_seq), 0:sw]
        ni = lre * hi + lim * hr + bu_ref[pl.ds(r0, n_seq), sw:2 * sw]
        st_ref[pl.ds(r0, n_seq), 0:sw] = nr
        st_ref[pl.ds(r0, n_seq), sw:2 * sw] = ni
        return nr, ni

    hr, hi = lax.fori_loop(0, steps, step, (h_ref[:, 0:sw], h_ref[:, sw:2 * sw]), unroll=8)
    h_ref[:, 0:sw] = hr
    h_ref[:, sw:2 * sw] = hi
    hr_rows = rows // 2
    for part in range(2):
        rs = slice(part * hr_rows, (part + 1) * hr_rows)
        y2 = _dot(st_ref[rs, :].astype(BF16), cmat_ref[0])
        ysc_ref[rs, :] = jnp.where(is_fwd[rs], y2[:, 0:LANES], y2[:, LANES:2 * LANES])
    for t in range(steps):
        g = ysc_ref[t * n_seq:(t + 1) * n_seq, :]
        ya_ref[pl.ds(t, n_seq, stride=pitch), :] = g
        yb_ref[pl.ds(steps - 1 - t, n_seq, stride=pitch), :] = g
    for b in range(half):
        of_ref[b] = ya_ref[b * pitch:b * pitch + steps, :]
        or_ref[b] = yb_ref[(half + b) * pitch:(half + b) * pitch + steps, :]


def _s5_scan(pf, bmat, cmat, lam, *, n_batch, seq, ctx, width):
    steps = S5_STEPS
    nblk = width // LANES
    rows = steps * SUBLANES
    sw2 = bmat.shape[-1]
    ncc = ctx // steps
    nlc = seq // steps
    ctx0 = n_batch * nlc

    def fwd_map(b):
        return lambda k, c: (jnp.where(c < ncc, ctx0 + b * ncc + c, b * nlc + (c - ncc)), k)

    def rev_map(b):
        return lambda k, c: (jnp.where(c < ncc, ctx0 + b * ncc + (ncc - 1 - c), b * nlc + (nlc - 1 - (c - ncc))), k)

    of_map = lambda k, c: (0, jnp.where(c < ncc, nlc + c, c - ncc), k)
    or_map = lambda k, c: (0, jnp.where(c < ncc, nlc + (ncc - 1 - c), nlc - 1 - (c - ncc)), k)
    u_specs = ([pl.BlockSpec((steps, LANES), fwd_map(b)) for b in range(n_batch)]
               + [pl.BlockSpec((steps, LANES), rev_map(b)) for b in range(n_batch)])
    out_sds = jax.ShapeDtypeStruct((n_batch, seq + ctx, width), F32)
    stage = pltpu.VMEM((SUBLANES * S5_PITCH, LANES), F32)
    return pl.pallas_call(
        _s5_scan_kernel,
        out_shape=(out_sds, out_sds),
        grid=(nblk, ncc + nlc),
        in_specs=u_specs + [
            pl.BlockSpec((1, 2 * LANES, sw2), lambda k, c: (k, 0, 0)),
            pl.BlockSpec((1, sw2, 2 * LANES), lambda k, c: (k, 0, 0)),
            pl.BlockSpec((1, SUBLANES, sw2), lambda k, c: (k, 0, 0))],
        out_specs=(pl.BlockSpec((n_batch, steps, LANES), of_map),
                   pl.BlockSpec((n_batch, steps, LANES), or_map)),
        scratch_shapes=[pltpu.VMEM((SUBLANES, sw2), F32),
                        stage, pltpu.VMEM((rows, 2 * LANES), F32),
                        pltpu.VMEM((rows, sw2), F32), pltpu.VMEM((rows, sw2), F32),
                        pltpu.VMEM((rows, LANES), F32), stage, stage],
        compiler_params=_cparams(("arbitrary", "arbitrary")),
        name="s5_scan",
    )(*([pf] * (2 * n_batch)), bmat, cmat, lam)


def _s5_post_kernel(pf_ref, yf_ref, yr_ref, d_ref, w_ref, b_ref, o_ref, *, width):
    u = pf_ref[:, 0:width]
    y = d_ref[...] * u + yf_ref[0] + yr_ref[0]
    k0 = math.sqrt(2.0 / math.pi)
    g = 0.5 * y * (1.0 + jnp.tanh(k0 * (y + 0.044715 * (y * y * y))))
    z = _dot(g.astype(BF16), w_ref[...]) + b_ref[...]
    o_ref[...] = (g * jax.nn.sigmoid(z)).astype(BF16)


def _s5_post(pf, yf, yr, d_skip, glu_w, glu_b, *, n_batch, seq, ctx):
    n_rows = pf.shape[0]
    width = yf.shape[2]
    tp = ctx
    nlt = seq // tp
    kern = functools.partial(_s5_post_kernel, width=width)
    tok_map = lambda b, j: (jnp.where(j < nlt, b * nlt + j, n_batch * nlt + b), 0)
    return pl.pallas_call(
        kern,
        out_shape=jax.ShapeDtypeStruct((n_rows, width), BF16),
        grid=(n_batch, nlt + 1),
        in_specs=[pl.BlockSpec((tp, pf.shape[1]), tok_map),
                  pl.BlockSpec((1, tp, width), lambda b, j: (b, j, 0)),
                  pl.BlockSpec((1, tp, width), lambda b, j: (b, j, 0)),
                  _const_spec((1, width)), _const_spec(glu_w.shape), _const_spec((1, width))],
        out_specs=pl.BlockSpec((tp, width), tok_map),
        compiler_params=_cparams(("arbitrary", "arbitrary")),
        name="s5_post",
    )(pf, yf, yr, d_skip, glu_w, glu_b)


def _merge_kernel(x_ref, mod_ref, nw_ref, y0_ref, y1_ref, y2_ref, y3_ref, wg_ref, wb_ref, wo_ref, o_ref, acc_ref):
    x = x_ref[...]
    m = mod_ref[0]
    n = _rms_mod(x, nw_ref[...], m[3:4], m[4:5]).astype(BF16)
    ys = (y0_ref, y1_ref, y2_ref, y3_ref)
    for b in range(N_BRANCH):
        gate = jax.nn.sigmoid(_dot(n, wg_ref[b]))
        contrib = gate * _dot(ys[b][...], wb_ref[b])
        if b == 0:
            acc_ref[...] = contrib
        else:
            acc_ref[...] += contrib
    o_ref[...] = x + m[5:6] * _dot(acc_ref[...].astype(BF16), wo_ref[...])


def _merge(h, n_rows, mod, norm_w, ys, wg, wb, wo, *, n_lat_tiles, tiles_per_batch, n_batch):
    d = h.shape[1]
    bw = wb.shape[1]
    return pl.pallas_call(
        _merge_kernel,
        out_shape=jax.ShapeDtypeStruct((n_rows, d), F32),
        grid=(n_rows // TM,),
        in_specs=[pl.BlockSpec((TM, d), lambda i: (i, 0)),
                  pl.BlockSpec((1, N_MOD, d), lambda i: (_mod_row_map(n_lat_tiles, tiles_per_batch, n_batch)(i), 0, 0)),
                  _const_spec((1, d))]
                 + [pl.BlockSpec((TM, bw), lambda i: (i, 0))] * N_BRANCH
                 + [_const_spec(wg.shape), _const_spec(wb.shape), _const_spec(wo.shape)],
        out_specs=pl.BlockSpec((TM, d), lambda i: (i, 0)),
        scratch_shapes=[pltpu.VMEM((TM, d), F32)],
        compiler_params=_cparams(("arbitrary",)),
        name="merge",
    )(h, mod, norm_w, *ys, wg, wb, wo)


def _swap_rot_pairs(w, nf):
    lead = w.shape[:-1]
    n = w.shape[-1]
    return w.reshape(lead + (n // (2 * nf), 2, nf))[..., ::-1, :].reshape(lead + (n,))


def _rope_tables(seq, dim, lane_off, width, period):
    nf = dim // 4
    pos = jnp.arange(seq)
    rows = (pos // GRID_W).astype(F32)
    cols = (pos % GRID_W).astype(F32)
    inv_freq = ROPE_BASE ** (-jnp.arange(nf, dtype=F32) / nf)
    ang_r = rows[:, None] * inv_freq[None, :]
    ang_c = cols[:, None] * inv_freq[None, :]
    cos = jnp.concatenate([jnp.cos(ang_r)] * 2 + [jnp.cos(ang_c)] * 2, axis=1)
    sin = jnp.concatenate([-jnp.sin(ang_r), jnp.sin(ang_r), -jnp.sin(ang_c), jnp.sin(ang_c)], axis=1)
    c_per = jnp.ones((seq, period), F32).at[:, lane_off:lane_off + dim].set(cos)
    s_per = jnp.zeros((seq, period), F32).at[:, lane_off:lane_off + dim].set(sin)
    reps = width // period
    c_tab = jnp.concatenate([jnp.tile(c_per, (1, reps)), jnp.ones((TM, width), F32)], axis=0)
    s_tab = jnp.concatenate([jnp.tile(s_per, (1, reps)), jnp.zeros((TM, width), F32)], axis=0)
    return c_tab, s_tab


def _inproj_weights(w_in):
    d = w_in.shape[0]
    hw = NA_HEADS * HEAD_DIM
    o_sq = 3 * hw
    o_sk = o_sq + SWA_HEADS * HEAD_DIM
    o_sv = o_sk + SWA_KV_HEADS * HEAD_DIM
    o_s5 = o_sv + SWA_KV_HEADS * HEAD_DIM
    s5w = 512
    o_cq = o_s5 + s5w
    o_ckv = o_cq + 256
    o_kr = o_ckv + 128
    o_g = o_kr + MLA_ROPE
    qscale = HEAD_DIM ** -0.5
    na = jnp.concatenate([w_in[:, :hw] * qscale, w_in[:, hw:3 * hw]], axis=1)
    sq = w_in[:, o_sq:o_sk] * qscale
    sk = w_in[:, o_sk:o_sv]
    sv = w_in[:, o_sv:o_s5]
    def dup_heads(a):
        return jnp.concatenate([a[:, kv * HEAD_DIM:(kv + 1) * HEAD_DIM]
                                for kv in range(SWA_KV_HEADS) for _ in range(2)], axis=1)

    sk_dup = dup_heads(sk)
    sv_dup = dup_heads(sv)
    wr = jnp.concatenate([sq, sk_dup], axis=1)
    kr = w_in[:, o_kr:o_g]
    lpad = jnp.zeros((d, MLA_NOPE), F32)
    rpad = jnp.zeros((d, LANES - MLA_NOPE - MLA_ROPE), F32)
    wf = jnp.concatenate([w_in[:, o_s5:o_kr], lpad, kr, rpad], axis=1)
    wg = w_in[:, o_g:].reshape(d, N_BRANCH, d).transpose(1, 0, 2)
    proj = tuple(a.astype(BF16) for a in (na, wr, sv_dup, wf))
    return proj, wg.astype(BF16)


def _mla_weights(w_uq, w_ukv):
    ql = w_uq.shape[0]
    kvl = w_ukv.shape[0]
    dq = MLA_NOPE + MLA_ROPE
    wq3 = w_uq.reshape(ql, MLA_HEADS, dq)
    pad = jnp.zeros((ql, MLA_HEADS, LANES - dq), F32)
    qscale = math.log2(math.e) * dq ** -0.5
    wq = jnp.concatenate([wq3 * qscale, pad], axis=2).reshape(ql, MLA_HEADS * LANES)
    wkv3 = w_ukv.reshape(kvl, MLA_HEADS, MLA_NOPE + MLA_V)
    wk = jnp.concatenate([wkv3[:, :, :MLA_NOPE], jnp.zeros((kvl, MLA_HEADS, LANES - MLA_NOPE), F32)], axis=2)
    wk = wk.reshape(kvl, MLA_HEADS * LANES)
    wv = wkv3[:, :, MLA_NOPE:].reshape(kvl, MLA_HEADS * MLA_V)
    return tuple(a.astype(BF16) for a in (wq, wk, wv))


def _s5_params(lam_re, lam_im, log_dt, b_re, b_im, c_re, c_im):
    a = lam_re.astype(F32)
    w = lam_im.astype(F32)
    dt = jnp.exp(log_dt.astype(F32))[..., None]
    mag = jnp.exp(a * dt)
    lb_re = mag * jnp.cos(w * dt)
    lb_im = mag * jnp.sin(w * dt)
    den = a * a + w * w
    cf_re = ((lb_re - 1.0) * a + lb_im * w) / den
    cf_im = (lb_im * a - (lb_re - 1.0) * w) / den
    bb_re = cf_re[..., None] * b_re - cf_im[..., None] * b_im
    bb_im = cf_re[..., None] * b_im + cf_im[..., None] * b_re
    n_dir, g, p, cg = b_re.shape
    gpb = LANES // cg
    nblk = g // gpb
    eye = jnp.eye(gpb, dtype=F32)

    def in_map(x):
        x5 = jnp.swapaxes(x.reshape(n_dir, nblk, gpb, p, cg), 3, 4)
        full = x5[:, :, :, :, None, :] * eye[None, None, :, None, :, None]
        return full.reshape(n_dir, nblk, gpb * cg, gpb * p)

    def out_map(x):
        x5 = jnp.swapaxes(x.reshape(n_dir, nblk, gpb, cg, p), 3, 4)
        full = x5[:, :, :, :, None, :] * eye[None, None, :, None, :, None]
        return full.reshape(n_dir, nblk, gpb * p, gpb * cg)

    b_in = jnp.concatenate([in_map(bb_re), in_map(bb_im)], axis=3)
    bmat = jnp.concatenate([b_in[0], b_in[1]], axis=1).astype(BF16)
    c_out = jnp.concatenate([out_map(c_re.astype(F32)), out_map(-c_im.astype(F32))], axis=2)
    cmat = jnp.concatenate([c_out[0], c_out[1]], axis=2).astype(BF16)
    half = SUBLANES // 2
    lam2 = jnp.concatenate([lb_re.reshape(n_dir, nblk, gpb * p), lb_im.reshape(n_dir, nblk, gpb * p)], axis=2)
    lam = jnp.concatenate([jnp.broadcast_to(lam2[0][:, None, :], (nblk, half, 2 * gpb * p)),
                           jnp.broadcast_to(lam2[1][:, None, :], (nblk, half, 2 * gpb * p))], axis=1)
    return bmat, cmat, lam


def _ffn_weights(wg, wu, wd):
    d, ff = wg.shape
    wg3 = wg.astype(BF16)
    wu3 = wu.astype(BF16)
    wd3 = wd.astype(BF16)
    return wg3, wu3, wd3


def kernel(x, c, ctx, c_ctx, ada_w, ada_b, ffn1_norm, ffn1_w_gate, ffn1_w_up, ffn1_w_down, mix_norm, w_in, na_rpb, swa_sink, s5_lambda_re, s5_lambda_im, s5_log_dt, s5_b_re, s5_b_im, s5_c_re, s5_c_im, s5_d, s5_glu_w, s5_glu_b, mla_q_norm, mla_w_uq, mla_kv_norm, mla_w_ukv, w_branch, w_out, ffn2_norm, ffn2_w_gate, ffn2_w_up, ffn2_w_down, final_norm):
    n_batch, seq, d = x.shape
    n_ctx = ctx.shape[1]
    depth = ada_w.shape[0]
    assert 2 * n_batch == SUBLANES and seq % TM == 0 and (n_batch * n_ctx) % TM == 0
    n_lat = n_batch * seq
    n_all = n_lat + n_batch * n_ctx
    tiles_per_batch = seq // TM
    n_lat_tiles = n_lat // TM
    geo = dict(n_lat_tiles=n_lat_tiles, tiles_per_batch=tiles_per_batch, n_batch=n_batch)

    h = jnp.concatenate([x.reshape(n_lat, d), ctx.reshape(n_batch * n_ctx, d)], axis=0)
    cc = jnp.concatenate([c, c_ctx[None, :], jnp.zeros((SUBLANES - n_batch - 1, d), F32)], axis=0)
    mod = _ada_mod(cc, ada_w, ada_b)

    cs_sw, sn_sw = _rope_tables(seq, HEAD_DIM, 0, 2 * LANES, HEAD_DIM)
    ck_kr, sk_kr = _rope_tables(seq, MLA_ROPE, LANES + MLA_NOPE, 2 * LANES, 2 * LANES)
    cq_ml, sq_ml = _rope_tables(seq, MLA_ROPE, MLA_NOPE, 2 * LANES, LANES)
    rows_n = seq // GRID_W
    s5w = s5_d.shape[1]
    offs = (s5w, s5w + mla_w_uq.shape[1], s5w + mla_w_uq.shape[1] + mla_w_ukv.shape[1])
    sw_col0 = 3 * NA_HEADS * HEAD_DIM

    for l in range(depth):
        need_ctx = l < depth - 1
        last = l == depth - 1
        ml = mod[l]
        ones = jnp.ones((1, d), F32)
        wg3, wu3, wd3 = _ffn_weights(ffn1_w_gate[l], ffn1_w_up[l], ffn1_w_down[l])
        h = _ffn(h, n_all, ml, ffn1_norm[l][None, :], wg3, wu3, wd3, ones, base=0, final=False, **geo)
        proj_w, gate_w = _inproj_weights(w_in[l])
        pb, pf = _inproj(h, ml, mix_norm[l][None, :], proj_w, (cs_sw, sn_sw, ck_kr, sk_kr), **geo)
        bias = _na_bias_table(na_rpb[l].astype(F32), rows_n)
        y_na = _na_attention(pb, bias, n_batch=n_batch, seq=seq, ctx=n_ctx, need_ctx=need_ctx)
        y_sw_l, y_sw_c = _swa_attention(pb, swa_sink[l].astype(F32), n_batch=n_batch, seq=seq, ctx=n_ctx,
                                        need_ctx=need_ctx, col0=sw_col0)
        y_sw = jnp.concatenate([y_sw_l, y_sw_c], axis=0) if need_ctx else y_sw_l
        wq, wk, wv = _mla_weights(mla_w_uq[l], mla_w_ukv[l])
        qm, km, vm = _mla_prep(pf, mla_q_norm[l][None, :], mla_kv_norm[l][None, :], wq, wk, wv, cq_ml, sq_ml,
                               n_lat_tiles=n_lat_tiles, tiles_per_batch=tiles_per_batch, offs=offs)
        y_mla = _mla_attention(qm, km, vm, n_batch=n_batch, seq=seq, ctx=n_ctx, need_ctx=need_ctx)
        s5p = _s5_params(s5_lambda_re[l], s5_lambda_im[l], s5_log_dt[l], s5_b_re[l], s5_b_im[l],
                         s5_c_re[l], s5_c_im[l])
        yf, yr = _s5_scan(pf, *s5p, n_batch=n_batch, seq=seq, ctx=n_ctx, width=s5w)
        y_s5 = _s5_post(pf, yf, yr, s5_d[l][None, :].astype(F32), s5_glu_w[l].astype(BF16),
                        s5_glu_b[l][None, :].astype(F32), n_batch=n_batch, seq=seq, ctx=n_ctx)
        n_rows = n_all if need_ctx else n_lat
        h = _merge(h, n_rows, ml, mix_norm[l][None, :], (y_na, y_sw, y_s5, y_mla), gate_w,
                   w_branch[l].astype(BF16), w_out[l].astype(BF16), **geo)
        wg3, wu3, wd3 = _ffn_weights(ffn2_w_gate[l], ffn2_w_up[l], ffn2_w_down[l])
        h = _ffn(h, n_rows, ml, ffn2_norm[l][None, :], wg3, wu3, wd3, final_norm[None, :], base=6, final=last, **geo)
    return h.reshape(n_batch, seq, d)
```

```python
import functools
import math

import numpy as np
import jax
import jax.numpy as jnp
from jax import lax
from jax.experimental import pallas as pl
from jax.experimental.pallas import tpu as pltpu

F32 = jnp.float32
BF16 = jnp.bfloat16

GRID_W = 64
HEAD_DIM = 64
N_BRANCH = 4
NA_HEADS = 8
NA_WIN_ROWS = 8
NA_WIN_COLS = 16
SWA_HEADS = 8
SWA_KV_HEADS = 2
SWA_WINDOW = 128
S5_GROUP = 16
S5_STATE = 64
MLA_HEADS = 8
MLA_NOPE = 64
MLA_ROPE = 32
MLA_V = 64
MACARON_WEIGHT = 0.5
ROPE_BASE = 10000.0
EPS = 1e-6
N_MOD = 9

LANES = 128
SUBLANES = 8
TM = 512
FF_CHUNK = 256
NA_QROWS = 4
NA_KROWS = NA_QROWS + NA_WIN_ROWS - 1
SWA_BLK = 128
MLA_TQ = 512
MLA_SUB = 256
S5_STEPS = 128
S5_PITCH = S5_STEPS + SUBLANES
NEG = -1e30
VMEM_LIMIT = 56 * 1024 * 1024


def _cparams(sem):
    return pltpu.CompilerParams(dimension_semantics=sem, vmem_limit_bytes=VMEM_LIMIT)


def _const_spec(shape):
    nd = len(shape)
    return pl.BlockSpec(shape, lambda *_: (0,) * nd, pipeline_mode=pl.Buffered(1))


def _dot(a, b):
    return jnp.dot(a, b, preferred_element_type=F32)


def _dot_t(a, b):
    return lax.dot_general(a, b, (((1,), (1,)), ((), ())), preferred_element_type=F32)


def _rms(x, w):
    return x * lax.rsqrt(jnp.mean(x * x, axis=-1, keepdims=True) + EPS) * w


def _rms_mod(x, w, shift, scale):
    return _rms(x, w) * (1.0 + scale) + shift


def _mod_row_map(n_lat_tiles, tiles_per_batch, n_batch):
    def f(i):
        return jnp.where(i < n_lat_tiles, i // tiles_per_batch, n_batch)
    return f


def _ada_kernel(c_ref, w_ref, b_ref, o_ref):
    c = c_ref[...]
    s = c * jax.nn.sigmoid(c)
    w = w_ref[0]
    s_hi = s.astype(BF16)
    s_lo = (s - s_hi.astype(F32)).astype(BF16)
    w_hi = w.astype(BF16)
    w_lo = (w - w_hi.astype(F32)).astype(BF16)
    o_ref[0] = _dot(s_hi, w_hi) + (_dot(s_hi, w_lo) + _dot(s_lo, w_hi)) + b_ref[0]


def _ada_mod(cc, ada_w, ada_b):
    depth, d, nd = ada_w.shape
    tn = 1024
    out = pl.pallas_call(
        _ada_kernel,
        out_shape=jax.ShapeDtypeStruct((depth, SUBLANES, nd), F32),
        grid=(depth, nd // tn),
        in_specs=[pl.BlockSpec((SUBLANES, d), lambda l, j: (0, 0)),
                  pl.BlockSpec((1, d, tn), lambda l, j: (l, 0, j)),
                  pl.BlockSpec((1, 1, tn), lambda l, j: (l, 0, j))],
        out_specs=pl.BlockSpec((1, SUBLANES, tn), lambda l, j: (l, 0, j)),
        compiler_params=_cparams(("arbitrary", "arbitrary")),
        name="ada_mod",
    )(cc, ada_w, ada_b.reshape(depth, 1, nd))
    return out.reshape(depth, SUBLANES, N_MOD, d)


def _ffn_kernel(x_ref, xc_ref, mod_ref, nw_ref, wg_ref, wu_ref, wd_ref, fw_ref, o_ref, *, base, final, n_first):
    x = x_ref[...]
    if n_first is not None:
        x = jnp.where(pl.program_id(0) < n_first, x, xc_ref[...])
    m = mod_ref[0]
    n = _rms_mod(x, nw_ref[...], m[base:base + 1], m[base + 1:base + 2]).astype(BF16)
    ff = wg_ref.shape[1]
    fc = FF_CHUNK if ff % FF_CHUNK == 0 else ff
    acc = None
    for c in range(ff // fc):
        cs = slice(c * fc, (c + 1) * fc)
        g = _dot(n, wg_ref[:, cs])
        u = _dot(n, wu_ref[:, cs])
        a = (g * jax.nn.sigmoid(g) * u).astype(BF16)
        y = _dot(a, wd_ref[cs, :])
        acc = y if acc is None else acc + y
    out = x + MACARON_WEIGHT * m[base + 2:base + 3] * acc
    if final:
        out = _rms(out, fw_ref[...])
    o_ref[...] = out


def _ffn(h, n_rows, mod, norm_w, wg, wu, wd, final_w, *, base, final, n_lat_tiles, tiles_per_batch, n_batch,
         h_tail=None):
    d = h.shape[1]
    ff = wg.shape[1]
    if h_tail is None:
        n_first = None
        h_tail = h
        x_map = lambda i: (i, 0)
        t_map = lambda i: (0, 0)
    else:
        n_first = h.shape[0] // TM
        x_map = lambda i: (jnp.minimum(i, n_first - 1), 0)
        t_map = lambda i: (jnp.maximum(i - n_first, 0), 0)
    kern = functools.partial(_ffn_kernel, base=base, final=final, n_first=n_first)
    return pl.pallas_call(
        kern,
        out_shape=jax.ShapeDtypeStruct((n_rows, d), F32),
        grid=(n_rows // TM,),
        in_specs=[pl.BlockSpec((TM, d), x_map),
                  pl.BlockSpec((TM, d) if n_first is not None else (SUBLANES, d), t_map),
                  pl.BlockSpec((1, N_MOD, d), lambda i: (_mod_row_map(n_lat_tiles, tiles_per_batch, n_batch)(i), 0, 0)),
                  _const_spec((1, d)),
                  _const_spec((d, ff)), _const_spec((d, ff)), _const_spec((ff, d)),
                  _const_spec((1, d))],
        out_specs=pl.BlockSpec((TM, d), lambda i: (i, 0)),
        compiler_params=_cparams(("arbitrary",)),
        name="ffn_final" if final else "ffn",
    )(h, h_tail, mod, norm_w, wg, wu, wd, final_w)


def _rope_apply(y, cs, sn, nf):
    w = y.shape[1]
    first = (lax.broadcasted_iota(jnp.int32, y.shape, 1) & nf) == 0
    ysw = jnp.where(first, pltpu.roll(y, w - nf, 1), pltpu.roll(y, nf, 1))
    return y * cs + ysw * sn


def _inproj_kernel(x_ref, mod_ref, nw_ref, wa_ref, wr_ref, wv_ref, wf_ref,
                   cs_ref, sn_ref, ck_ref, sk_ref, ob_ref, of_ref):
    x = x_ref[...]
    m = mod_ref[0]
    n = _rms_mod(x, nw_ref[...], m[3:4], m[4:5]).astype(BF16)
    na = wa_ref.shape[1]
    nr = wr_ref.shape[1]
    nv = wv_ref.shape[1]
    nf = wf_ref.shape[1]
    cw = 2 * LANES
    for c in range(na // cw):
        ob_ref[:, c * cw:(c + 1) * cw] = _dot(n, wa_ref[:, c * cw:(c + 1) * cw]).astype(BF16)
    for c in range(nr // cw):
        y = _dot(n, wr_ref[:, c * cw:(c + 1) * cw])
        ob_ref[:, na + c * cw:na + (c + 1) * cw] = _rope_apply(y, cs_ref[...], sn_ref[...], HEAD_DIM // 4).astype(BF16)
    for c in range(nv // cw):
        ob_ref[:, na + nr + c * cw:na + nr + (c + 1) * cw] = _dot(n, wv_ref[:, c * cw:(c + 1) * cw]).astype(BF16)
    for c in range(nf // cw - 1):
        of_ref[:, c * cw:(c + 1) * cw] = _dot(n, wf_ref[:, c * cw:(c + 1) * cw])
    y = _dot(n, wf_ref[:, nf - cw:nf])
    of_ref[:, nf - cw:nf] = _rope_apply(y, ck_ref[...], sk_ref[...], MLA_ROPE // 4)


def _inproj(h, mod, norm_w, w, tabs, *, n_lat_tiles, tiles_per_batch, n_batch):
    n_rows, d = h.shape
    wa, wr, wv, wf = w
    cs, sn, ck, sk = tabs
    nb = wa.shape[1] + wr.shape[1] + wv.shape[1]
    nf = wf.shape[1]
    tab_map = lambda i: (jnp.where(i < n_lat_tiles, i % tiles_per_batch, tiles_per_batch), 0)
    return pl.pallas_call(
        _inproj_kernel,
        out_shape=(jax.ShapeDtypeStruct((n_rows, nb), BF16), jax.ShapeDtypeStruct((n_rows, nf), F32)),
        grid=(n_rows // TM,),
        in_specs=[pl.BlockSpec((TM, d), lambda i: (i, 0)),
                  pl.BlockSpec((1, N_MOD, d), lambda i: (_mod_row_map(n_lat_tiles, tiles_per_batch, n_batch)(i), 0, 0)),
                  _const_spec((1, d))]
                 + [_const_spec(a.shape) for a in w]
                 + [pl.BlockSpec((TM, 2 * LANES), tab_map)] * 4,
        out_specs=(pl.BlockSpec((TM, nb), lambda i: (i, 0)), pl.BlockSpec((TM, nf), lambda i: (i, 0))),
        compiler_params=_cparams(("arbitrary",)),
        name="inproj",
    )(h, mod, norm_w, *w, cs, sn, ck, sk)


def _lane_half_masks(rows):
    lane = lax.broadcasted_iota(jnp.int32, (rows, LANES), 1)
    return lane < HEAD_DIM


def _softmax2(s1, s2, sink=None):
    m = jnp.maximum(jnp.max(s1, axis=-1, keepdims=True), jnp.max(s2, axis=-1, keepdims=True))
    if sink is not None:
        m = jnp.maximum(m, sink)
    p1 = jnp.exp(s1 - m)
    p2 = jnp.exp(s2 - m)
    l = jnp.sum(p1, axis=-1, keepdims=True) + jnp.sum(p2, axis=-1, keepdims=True)
    if sink is not None:
        l = l + jnp.exp(sink - m)
    return p1, p2, l


def _na_kernel(q_ref, k_ref, v_ref, kc_ref, vc_ref, bias_ref, o_ref, *, n_blk, rows_n):
    blk = pl.program_id(1)
    nq = NA_QROWS * GRID_W
    nk = NA_KROWS * GRID_W
    ws = jnp.clip(NA_QROWS * blk - NA_WIN_ROWS // 2, 0, rows_n - NA_KROWS)
    r0 = pl.multiple_of(ws * GRID_W, GRID_W)
    lo = _lane_half_masks(nq)
    for j in range(NA_HEADS // 2):
        sl = slice(j * LANES, (j + 1) * LANES)
        qp = q_ref[:, sl]
        kw = k_ref[pl.ds(r0, nk), sl]
        vw = v_ref[pl.ds(r0, nk), sl]
        kc = kc_ref[:, sl]
        vc = vc_ref[:, sl]
        zero = jnp.zeros_like(qp)
        qs = jnp.concatenate([jnp.where(lo, qp, zero), jnp.where(lo, zero, qp)], axis=0)
        s1 = _dot_t(qs, kw) + bias_ref[0, j]
        s2 = _dot_t(qs, kc)
        p1, p2, l = _softmax2(s1, s2)
        o = (_dot(p1.astype(BF16), vw) + _dot(p2.astype(BF16), vc)) / l
        o_ref[:, sl] = jnp.where(lo, o[:nq], o[nq:]).astype(BF16)


def _na_attention(pb, bias, *, n_batch, seq, ctx, need_ctx):
    rows_n = seq // GRID_W
    n_blk = rows_n // NA_QROWS
    nq = NA_QROWS * GRID_W
    hw = NA_HEADS * HEAD_DIM
    n_q = n_blk + (1 if need_ctx else 0)
    assert ctx == nq
    lat_blocks = n_batch * seq // nq
    qmap = lambda b, i: (jnp.where(i < n_blk, b * n_blk + i, lat_blocks + b), 0)

    def bias_map(b, i):
        t = jnp.where(i == 0, 0, jnp.where(i == 1, 1, jnp.where(i == n_blk - 1, 3, jnp.where(i == n_blk, 4, 2))))
        return (t, 0, 0, 0)

    kern = functools.partial(_na_kernel, n_blk=n_blk, rows_n=rows_n)
    return pl.pallas_call(
        kern,
        out_shape=jax.ShapeDtypeStruct((n_batch * n_q * nq, hw), BF16),
        grid=(n_batch, n_q),
        in_specs=[pl.BlockSpec((nq, hw), qmap),
                  pl.BlockSpec((seq, hw), lambda b, i: (b, 1)),
                  pl.BlockSpec((seq, hw), lambda b, i: (b, 2)),
                  pl.BlockSpec((ctx, hw), lambda b, i: (n_batch * seq // ctx + b, 1)),
                  pl.BlockSpec((ctx, hw), lambda b, i: (n_batch * seq // ctx + b, 2)),
                  pl.BlockSpec((1, NA_HEADS // 2, 2 * nq, NA_KROWS * GRID_W), bias_map)],
        out_specs=pl.BlockSpec((nq, hw), qmap),
        compiler_params=_cparams(("arbitrary", "arbitrary")),
        name="na_attn",
    )(pb, pb, pb, pb, pb, bias)


def _na_bias_table(rpb, rows_n):
    n_blk = rows_n // NA_QROWS
    kr_n = min(NA_WIN_ROWS, rows_n)
    n_heads = rpb.shape[0]
    col = np.arange(GRID_W)
    c0 = np.clip(col - NA_WIN_COLS // 2, 0, GRID_W - NA_WIN_COLS)
    col_ok = (col[None, :] >= c0[:, None]) & (col[None, :] < c0[:, None] + NA_WIN_COLS)
    padded = jnp.pad(rpb, ((0, 0), (0, 0), (GRID_W, GRID_W)))
    off = GRID_W + NA_WIN_COLS - 1
    toep = jnp.stack([padded[:, :, off - qc:off - qc + GRID_W] for qc in range(GRID_W)], axis=2)
    toep = jnp.where(col_ok[None, None], toep, NEG)
    masked = jnp.full((n_heads, 1, GRID_W, GRID_W), NEG, F32)
    ext = jnp.concatenate([masked, toep, masked], axis=1)
    pair2 = jnp.concatenate([ext[:, :-1], ext[:, 1:]], axis=-1)
    n_dr = 2 * NA_WIN_ROWS - 1
    plans = []
    for blk in (0, 1, 2, n_blk - 1):
        ws = int(np.clip(NA_QROWS * blk - NA_WIN_ROWS // 2, 0, rows_n - NA_KROWS))
        plan = []
        for qr in range(NA_QROWS):
            r = NA_QROWS * blk + qr
            r0 = int(np.clip(r - kr_n // 2, 0, rows_n - kr_n))
            row = []
            for kk in range(0, NA_KROWS, 2):
                oks = tuple(r0 <= ws + kk + i < r0 + kr_n and kk + i < NA_KROWS for i in range(2))
                row.append((ws + kk - r + NA_WIN_ROWS, oks))
            plan.append(tuple(row))
        plans.append(tuple(plan))
    plans.append(None)
    nq = NA_QROWS * GRID_W
    nk = NA_KROWS * GRID_W
    kern = functools.partial(_na_bias_kernel, plans=tuple(plans))
    return pl.pallas_call(
        kern,
        out_shape=jax.ShapeDtypeStruct((len(plans), n_heads // 2, 2 * nq, nk), F32),
        grid=(n_heads // 2,),
        in_specs=[pl.BlockSpec((2, n_dr + 1, GRID_W, 2 * GRID_W), lambda j: (j, 0, 0, 0))],
        out_specs=pl.BlockSpec((len(plans), 1, 2 * nq, nk), lambda j: (0, j, 0, 0)),
        compiler_params=_cparams(("arbitrary",)),
        name="na_bias",
    )(pair2)


def _na_bias_kernel(p2_ref, o_ref, *, plans):
    nq = NA_QROWS * GRID_W
    lo = lax.broadcasted_iota(jnp.int32, (GRID_W, 2 * GRID_W), 1) < GRID_W
    neg = jnp.full((GRID_W, 2 * GRID_W), NEG, F32)
    for t, plan in enumerate(plans):
        if plan is None:
            o_ref[t, 0] = jnp.full(o_ref.shape[2:], NEG, F32)
            continue
        for half in range(2):
            for qr, row in enumerate(plan):
                rs = slice(half * nq + qr * GRID_W, half * nq + (qr + 1) * GRID_W)
                for kp, (e, (ok_a, ok_b)) in enumerate(row):
                    width = min(2 * GRID_W, o_ref.shape[3] - kp * 2 * GRID_W)
                    if ok_a or ok_b:
                        tile = p2_ref[half, e]
                        if not ok_a:
                            tile = jnp.where(lo, neg, tile)
                        if not ok_b:
                            tile = jnp.where(lo, tile, neg)
                    else:
                        tile = neg
                    o_ref[t, 0, rs, kp * 2 * GRID_W:kp * 2 * GRID_W + width] = tile[:, :width]


def _swa_kernel(sink_ref, q_ref, k_ref, v_ref, qc_ref, kc_ref, vc_ref, o_ref, oc_ref, *, seq, need_ctx):
    n_blk = seq // SWA_BLK
    band = 3 * SWA_BLK
    group = SWA_HEADS // SWA_KV_HEADS
    pairs = group // 2

    def group_attn(q_slabs, kv, kb, vb, kc, vc, mask_bias):
        r = q_slabs[0].shape[0]
        lo = _lane_half_masks(r)
        parts = []
        for qp in q_slabs:
            zero = jnp.zeros_like(qp)
            parts += [jnp.where(lo, qp, zero), jnp.where(lo, zero, qp)]
        qs = jnp.concatenate(parts, axis=0)
        row = lax.broadcasted_iota(jnp.int32, (group * r, 1), 0)
        sink = jnp.full((group * r, 1), sink_ref[kv * group + group - 1], F32)
        for g in range(group - 2, -1, -1):
            sink = jnp.where(row < (g + 1) * r, sink_ref[kv * group + g], sink)
        s2 = _dot_t(qs, kc)
        if kb is None:
            m = jnp.maximum(jnp.max(s2, axis=-1, keepdims=True), sink)
            p2 = jnp.exp(s2 - m)
            l = jnp.sum(p2, axis=-1, keepdims=True) + jnp.exp(sink - m)
            o = _dot(p2.astype(BF16), vc) / l
        else:
            s1 = _dot_t(qs, kb) + mask_bias
            m = jnp.maximum(jnp.maximum(jnp.max(s1, axis=-1, keepdims=True), jnp.max(s2, axis=-1, keepdims=True)), sink)
            p1 = jnp.exp(s1 - m)
            p2 = jnp.exp(s2 - m)
            l = jnp.sum(p1, axis=-1, keepdims=True) + jnp.sum(p2, axis=-1, keepdims=True) + jnp.exp(sink - m)
            o = (_dot(p1.astype(BF16), vb) + _dot(p2.astype(BF16), vc)) / l
        return [jnp.where(lo, o[(2 * i) * r:(2 * i + 1) * r], o[(2 * i + 1) * r:(2 * i + 2) * r]) for i in range(pairs)]

    qi = lax.broadcasted_iota(jnp.int32, (group * SWA_BLK, band), 0) & (SWA_BLK - 1)
    ki = lax.broadcasted_iota(jnp.int32, (group * SWA_BLK, band), 1)

    def blk_body(n, carry):
        start = jnp.clip((n - 1) * SWA_BLK, 0, seq - band)
        start = pl.multiple_of(start, SWA_BLK)
        q0 = pl.multiple_of(n * SWA_BLK, SWA_BLK)
        delta = (start + ki) - (q0 + qi)
        mask_bias = jnp.where(jnp.abs(delta) <= SWA_WINDOW, 0.0, NEG).astype(F32)
        for kv in range(SWA_KV_HEADS):
            ksl = slice(kv * LANES, (kv + 1) * LANES)
            slabs = [q_ref[pl.ds(q0, SWA_BLK), (kv * pairs + i) * LANES:(kv * pairs + i + 1) * LANES]
                     for i in range(pairs)]
            outs = group_attn(slabs, kv, k_ref[pl.ds(start, band), ksl], v_ref[pl.ds(start, band), ksl],
                              kc_ref[:, ksl], vc_ref[:, ksl], mask_bias)
            for i in range(pairs):
                o_ref[pl.ds(q0, SWA_BLK), (kv * pairs + i) * LANES:(kv * pairs + i + 1) * LANES] = outs[i].astype(BF16)
        return carry

    lax.fori_loop(0, n_blk, blk_body, 0)

    if need_ctx:
        for kv in range(SWA_KV_HEADS):
            ksl = slice(kv * LANES, (kv + 1) * LANES)
            slabs = [qc_ref[:, (kv * pairs + i) * LANES:(kv * pairs + i + 1) * LANES] for i in range(pairs)]
            outs = group_attn(slabs, kv, None, None, kc_ref[:, ksl], vc_ref[:, ksl], None)
            for i in range(pairs):
                oc_ref[:, (kv * pairs + i) * LANES:(kv * pairs + i + 1) * LANES] = outs[i].astype(BF16)
    else:
        oc_ref[...] = jnp.zeros_like(oc_ref)


def _swa_attention(pb, sink, *, n_batch, seq, ctx, need_ctx, col0):
    hw = SWA_HEADS * HEAD_DIM
    kw = SWA_KV_HEADS * LANES
    qcol = col0 // hw
    kcol = (col0 + hw) // kw
    vcol = kcol + 1
    cblk = n_batch * seq // ctx
    kern = functools.partial(_swa_kernel, seq=seq, need_ctx=need_ctx)
    return pl.pallas_call(
        kern,
        out_shape=(jax.ShapeDtypeStruct((n_batch * seq, hw), BF16),
                   jax.ShapeDtypeStruct((n_batch * ctx, hw), BF16)),
        grid=(n_batch,),
        in_specs=[pl.BlockSpec(memory_space=pltpu.SMEM),
                  pl.BlockSpec((seq, hw), lambda b: (b, qcol)),
                  pl.BlockSpec((seq, kw), lambda b: (b, kcol)),
                  pl.BlockSpec((seq, kw), lambda b: (b, vcol)),
                  pl.BlockSpec((ctx, hw), lambda b: (cblk + b, qcol)),
                  pl.BlockSpec((ctx, kw), lambda b: (cblk + b, kcol)),
                  pl.BlockSpec((ctx, kw), lambda b: (cblk + b, vcol))],
        out_specs=(pl.BlockSpec((seq, hw), lambda b: (b, 0)),
                   pl.BlockSpec((ctx, hw), lambda b: (b, 0))),
        compiler_params=_cparams(("arbitrary",)),
        name="swa_attn",
    )(sink, pb, pb, pb, pb, pb, pb)


def _mla_prep_kernel(pf_ref, qn_ref, kn_ref, wq_ref, wk_ref, wv_ref, cq_ref, sq_ref,
                     q_ref, k_ref, v_ref, *, off_cq, off_ckv, off_kr, q_lora, kv_lora):
    cq = pf_ref[:, off_cq:off_cq + q_lora]
    ckv = pf_ref[:, off_ckv:off_ckv + kv_lora]
    kr = pf_ref[:, off_kr:off_kr + LANES]
    kr2 = jnp.concatenate([kr, kr], axis=1)
    cqn = _rms(cq, qn_ref[...]).astype(BF16)
    ckvn = _rms(ckv, kn_ref[...]).astype(BF16)
    cw = 2 * LANES
    for c in range(MLA_HEADS * LANES // cw):
        sl = slice(c * cw, (c + 1) * cw)
        q = _rope_apply(_dot(cqn, wq_ref[:, sl]), cq_ref[...], sq_ref[...], MLA_ROPE // 4)
        q_ref[:, sl] = q.astype(BF16)
        k_ref[:, sl] = (_dot(ckvn, wk_ref[:, sl]) + kr2).astype(BF16)
    ones_hi = jnp.where(_lane_half_masks(1), 0.0, 1.0).astype(F32)
    ones_hi = jnp.concatenate([ones_hi, ones_hi], axis=1)
    for c in range(MLA_HEADS * LANES // cw):
        sl = slice(c * cw, (c + 1) * cw)
        v_ref[:, sl] = (_dot(ckvn, wv_ref[:, sl]) + ones_hi).astype(BF16)


def _mla_prep(pf, qn, kn, wq, wk, wv, cq_tab, sq_tab, *, n_lat_tiles, tiles_per_batch, offs):
    n_rows = pf.shape[0]
    off_cq, off_ckv, off_kr = offs
    tab_map = lambda i: (jnp.where(i < n_lat_tiles, i % tiles_per_batch, tiles_per_batch), 0)
    kern = functools.partial(_mla_prep_kernel, off_cq=off_cq, off_ckv=off_ckv, off_kr=off_kr,
                             q_lora=wq.shape[0], kv_lora=wk.shape[0])
    hq = MLA_HEADS * LANES
    hv = MLA_HEADS * LANES
    return pl.pallas_call(
        kern,
        out_shape=(jax.ShapeDtypeStruct((n_rows, hq), BF16), jax.ShapeDtypeStruct((n_rows, hq), BF16),
                   jax.ShapeDtypeStruct((n_rows, hv), BF16)),
        grid=(n_rows // TM,),
        in_specs=[pl.BlockSpec((TM, pf.shape[1]), lambda i: (i, 0)),
                  _const_spec(qn.shape), _const_spec(kn.shape),
                  _const_spec(wq.shape), _const_spec(wk.shape), _const_spec(wv.shape),
                  pl.BlockSpec((TM, 2 * LANES), tab_map), pl.BlockSpec((TM, 2 * LANES), tab_map)],
        out_specs=(pl.BlockSpec((TM, hq), lambda i: (i, 0)), pl.BlockSpec((TM, hq), lambda i: (i, 0)),
                   pl.BlockSpec((TM, hv), lambda i: (i, 0))),
        compiler_params=_cparams(("arbitrary",)),
        name="mla_prep",
    )(pf, qn, kn, wq, wk, wv, cq_tab, sq_tab)


def _mla_body(q_ref, k_ref, v_ref, kc_ref, vc_ref, o_ref):
    sub = MLA_SUB
    n_sub = q_ref.shape[0] // sub
    lo = _lane_half_masks(sub)
    units = [(r, half) for r in range(n_sub) for half in range(2)]

    def scores(r, half):
        sl = slice(half * LANES, (half + 1) * LANES)
        q = q_ref[r * sub:(r + 1) * sub, sl]
        s1 = None if k_ref is None else _dot_t(q, k_ref[:, sl])
        return s1, _dot_t(q, kc_ref[:, sl])

    def finish(half, s1, s2):
        sl = slice(half * LANES, (half + 1) * LANES)
        if s1 is None:
            m = jnp.max(s2, axis=-1, keepdims=True)
            return _dot(jnp.exp2(s2 - m).astype(BF16), vc_ref[:, sl])
        m = jnp.maximum(jnp.max(s1, axis=-1, keepdims=True), jnp.max(s2, axis=-1, keepdims=True))
        p1 = jnp.exp2(s1 - m)
        p2 = jnp.exp2(s2 - m)
        return _dot(p1.astype(BF16), v_ref[:, sl]) + _dot(p2.astype(BF16), vc_ref[:, sl])

    outs = {}
    pending = scores(*units[0])
    for idx, u in enumerate(units):
        following = scores(*units[idx + 1]) if idx + 1 < len(units) else None
        outs[u] = finish(u[1], *pending)
        pending = following
    for r in range(n_sub):
        o0, o1 = outs[(r, 0)], outs[(r, 1)]
        r0 = pltpu.roll(o0, HEAD_DIM, 1)
        r1 = pltpu.roll(o1, HEAD_DIM, 1)
        o_ref[r * sub:(r + 1) * sub, :] = jnp.where(lo, o0 / r0, r1 / o1).astype(BF16)


def _mla_kernel(q_ref, k_ref, v_ref, kc_ref, vc_ref, o_ref):
    _mla_body(q_ref, k_ref, v_ref, kc_ref, vc_ref, o_ref)


def _mla_ctx_kernel(q_ref, kc_ref, vc_ref, o_ref):
    _mla_body(q_ref, None, None, kc_ref, vc_ref, o_ref)


def _mla_attention(qm, km, vm, *, n_batch, seq, ctx, need_ctx):
    n_qt = seq // MLA_TQ
    cblk = n_batch * seq // ctx
    hv = MLA_HEADS * MLA_V
    y_lat = pl.pallas_call(
        _mla_kernel,
        out_shape=jax.ShapeDtypeStruct((n_batch * seq, hv), BF16),
        grid=(n_batch, MLA_HEADS // 2, n_qt),
        in_specs=[pl.BlockSpec((MLA_TQ, 2 * LANES), lambda b, p, i: (b * n_qt + i, p)),
                  pl.BlockSpec((seq, 2 * LANES), lambda b, p, i: (b, p)),
                  pl.BlockSpec((seq, 2 * LANES), lambda b, p, i: (b, p)),
                  pl.BlockSpec((ctx, 2 * LANES), lambda b, p, i: (cblk + b, p)),
                  pl.BlockSpec((ctx, 2 * LANES), lambda b, p, i: (cblk + b, p))],
        out_specs=pl.BlockSpec((MLA_TQ, LANES), lambda b, p, i: (b * n_qt + i, p)),
        compiler_params=_cparams(("arbitrary", "arbitrary", "arbitrary")),
        name="mla_attn",
    )(qm, km, vm, km, vm)
    if not need_ctx:
        return y_lat
    assert ctx % MLA_SUB == 0
    y_ctx = pl.pallas_call(
        _mla_ctx_kernel,
        out_shape=jax.ShapeDtypeStruct((n_batch * ctx, hv), BF16),
        grid=(n_batch, MLA_HEADS // 2),
        in_specs=[pl.BlockSpec((ctx, 2 * LANES), lambda b, p: (cblk + b, p)),
                  pl.BlockSpec((ctx, 2 * LANES), lambda b, p: (cblk + b, p)),
                  pl.BlockSpec((ctx, 2 * LANES), lambda b, p: (cblk + b, p))],
        out_specs=pl.BlockSpec((ctx, LANES), lambda b, p: (b, p)),
        compiler_params=_cparams(("arbitrary", "arbitrary")),
        name="mla_ctx_attn",
    )(qm, km, vm)
    return jnp.concatenate([y_lat, y_ctx], axis=0)


def _s5_scan_kernel(uf0_ref, uf1_ref, uf2_ref, uf3_ref, ur0_ref, ur1_ref, ur2_ref, ur3_ref,
                    bmat_ref, cmat_ref, lam_ref, of_ref, or_ref,
                    h_ref, stage_ref, lhs_ref, bu_ref, st_ref, ysc_ref, ya_ref, yb_ref):
    c = pl.program_id(0)
    n_seq = SUBLANES
    half = n_seq // 2
    steps = S5_STEPS
    pitch = S5_PITCH
    rows = steps * n_seq
    nblk = h_ref.shape[0]
    sw = h_ref.shape[2] // 2
    ufs = (uf0_ref, uf1_ref, uf2_ref, uf3_ref)
    urs = (ur0_ref, ur1_ref, ur2_ref, ur3_ref)
    is_fwd = (lax.broadcasted_iota(jnp.int32, (rows, 1), 0) % n_seq) < half
    sub_fwd = lax.broadcasted_iota(jnp.int32, (n_seq, LANES), 0) < half

    @pl.when(c == 0)
    def _():
        h_ref[...] = jnp.zeros_like(h_ref)

    for k in range(nblk):
        ks = slice(k * LANES, (k + 1) * LANES)
        z = k % 2
        for b in range(half):
            stage_ref[k, b * pitch:b * pitch + steps, :] = ufs[b][:, ks]
            stage_ref[k, (half + b) * pitch:(half + b) * pitch + steps, :] = urs[b][:, ks]
        for t in range(steps):
            ga = stage_ref[k, pl.ds(t, n_seq, stride=pitch), :]
            gb = stage_ref[k, pl.ds(steps - 1 - t, n_seq, stride=pitch), :]
            lhs_ref[z, t * n_seq:(t + 1) * n_seq, 0:LANES] = jnp.where(sub_fwd, ga, 0.0)
            lhs_ref[z, t * n_seq:(t + 1) * n_seq, LANES:2 * LANES] = jnp.where(sub_fwd, 0.0, gb)

        bu_ref[z] = _dot(lhs_ref[z].astype(BF16), bmat_ref[k])
        lre = lam_ref[k, :, 0:sw]
        lim = lam_ref[k, :, sw:2 * sw]
        hr = h_ref[k, :, 0:sw]
        hi = h_ref[k, :, sw:2 * sw]
        for t in range(steps):
            rs = slice(t * n_seq, (t + 1) * n_seq)
            hr, hi = (lre * hr - lim * hi + bu_ref[z, rs, 0:sw],
                      lre * hi + lim * hr + bu_ref[z, rs, sw:2 * sw])
            st_ref[z, rs, 0:sw] = hr
            st_ref[z, rs, sw:2 * sw] = hi
        h_ref[k, :, 0:sw] = hr
        h_ref[k, :, sw:2 * sw] = hi
        hr_rows = rows // 2
        for part in range(2):
            rs = slice(part * hr_rows, (part + 1) * hr_rows)
            y2 = _dot(st_ref[z, rs, :].astype(BF16), cmat_ref[k])
            ysc_ref[z, rs, :] = jnp.where(is_fwd[rs], y2[:, 0:LANES], y2[:, LANES:2 * LANES])
        for t in range(steps):
            g = ysc_ref[z, t * n_seq:(t + 1) * n_seq, :]
            ya_ref[k, pl.ds(t, n_seq, stride=pitch), :] = g
            yb_ref[k, pl.ds(steps - 1 - t, n_seq, stride=pitch), :] = g
        for b in range(half):
            of_ref[b, :, ks] = ya_ref[k, b * pitch:b * pitch + steps, :]
            or_ref[b, :, ks] = yb_ref[k, (half + b) * pitch:(half + b) * pitch + steps, :]


def _s5_scan(pf, bmat, cmat, lam, *, n_batch, seq, ctx, width):
    steps = S5_STEPS
    nblk = width // LANES
    rows = steps * SUBLANES
    sw2 = bmat.shape[-1]
    ncc = ctx // steps
    nlc = seq // steps
    ctx0 = n_batch * nlc

    def fwd_map(b):
        return lambda c: (jnp.where(c < ncc, ctx0 + b * ncc + c, b * nlc + (c - ncc)), 0)

    def rev_map(b):
        return lambda c: (jnp.where(c < ncc, ctx0 + b * ncc + (ncc - 1 - c), b * nlc + (nlc - 1 - (c - ncc))), 0)

    of_map = lambda c: (0, jnp.where(c < ncc, nlc + c, c - ncc), 0)
    or_map = lambda c: (0, jnp.where(c < ncc, nlc + (ncc - 1 - c), nlc - 1 - (c - ncc)), 0)
    u_specs = ([pl.BlockSpec((steps, width), fwd_map(b)) for b in range(n_batch)]
               + [pl.BlockSpec((steps, width), rev_map(b)) for b in range(n_batch)])
    out_sds = jax.ShapeDtypeStruct((n_batch, seq + ctx, width), F32)
    stage = pltpu.VMEM((nblk, SUBLANES * S5_PITCH, LANES), F32)
    return pl.pallas_call(
        _s5_scan_kernel,
        out_shape=(out_sds, out_sds),
        grid=(ncc + nlc,),
        in_specs=u_specs + [_const_spec(bmat.shape), _const_spec(cmat.shape), _const_spec(lam.shape)],
        out_specs=(pl.BlockSpec((n_batch, steps, width), of_map),
                   pl.BlockSpec((n_batch, steps, width), or_map)),
        scratch_shapes=[pltpu.VMEM((nblk, SUBLANES, sw2), F32),
                        stage, pltpu.VMEM((2, rows, 2 * LANES), F32),
                        pltpu.VMEM((2, rows, sw2), F32), pltpu.VMEM((2, rows, sw2), F32),
                        pltpu.VMEM((2, rows, LANES), F32), stage, stage],
        compiler_params=_cparams(("arbitrary",)),
        name="s5_scan",
    )(*([pf] * (2 * n_batch)), bmat, cmat, lam)


def _s5_post_kernel(pf_ref, yf_ref, yr_ref, d_ref, w_ref, b_ref, o_ref, *, width):
    u = pf_ref[:, 0:width]
    y = d_ref[...] * u + yf_ref[0] + yr_ref[0]
    k0 = math.sqrt(2.0 / math.pi)
    g = 0.5 * y * (1.0 + jnp.tanh(k0 * (y + 0.044715 * (y * y * y))))
    z = _dot(g.astype(BF16), w_ref[...]) + b_ref[...]
    o_ref[...] = (g * jax.nn.sigmoid(z)).astype(BF16)


def _s5_post(pf, yf, yr, d_skip, glu_w, glu_b, *, n_batch, seq, ctx):
    n_rows = pf.shape[0]
    width = yf.shape[2]
    tp = ctx
    nlt = seq // tp
    kern = functools.partial(_s5_post_kernel, width=width)
    tok_map = lambda b, j: (jnp.where(j < nlt, b * nlt + j, n_batch * nlt + b), 0)
    return pl.pallas_call(
        kern,
        out_shape=jax.ShapeDtypeStruct((n_rows, width), BF16),
        grid=(n_batch, nlt + 1),
        in_specs=[pl.BlockSpec((tp, pf.shape[1]), tok_map),
                  pl.BlockSpec((1, tp, width), lambda b, j: (b, j, 0)),
                  pl.BlockSpec((1, tp, width), lambda b, j: (b, j, 0)),
                  _const_spec((1, width)), _const_spec(glu_w.shape), _const_spec((1, width))],
        out_specs=pl.BlockSpec((tp, width), tok_map),
        compiler_params=_cparams(("arbitrary", "arbitrary")),
        name="s5_post",
    )(pf, yf, yr, d_skip, glu_w, glu_b)


def _merge_kernel(x_ref, mod_ref, nw_ref, y0_ref, y1_ref, y2_ref, y3_ref, wg_ref, wb_ref, wo_ref, o_ref, acc_ref):
    x = x_ref[...]
    m = mod_ref[0]
    n = _rms_mod(x, nw_ref[...], m[3:4], m[4:5]).astype(BF16)
    ys = (y0_ref, y1_ref, y2_ref, y3_ref)
    for b in range(N_BRANCH):
        gate = jax.nn.sigmoid(_dot(n, wg_ref[b]))
        contrib = gate * _dot(ys[b][...], wb_ref[b])
        if b == 0:
            acc_ref[...] = contrib
        else:
            acc_ref[...] += contrib
    o_ref[...] = x + m[5:6] * _dot(acc_ref[...].astype(BF16), wo_ref[...])


def _merge(h, n_rows, mod, norm_w, ys, wg, wb, wo, *, n_lat_tiles, tiles_per_batch, n_batch):
    d = h.shape[1]
    bw = wb.shape[1]
    return pl.pallas_call(
        _merge_kernel,
        out_shape=jax.ShapeDtypeStruct((n_rows, d), F32),
        grid=(n_rows // TM,),
        in_specs=[pl.BlockSpec((TM, d), lambda i: (i, 0)),
                  pl.BlockSpec((1, N_MOD, d), lambda i: (_mod_row_map(n_lat_tiles, tiles_per_batch, n_batch)(i), 0, 0)),
                  _const_spec((1, d))]
                 + [pl.BlockSpec((TM, bw), lambda i: (i, 0))] * N_BRANCH
                 + [_const_spec(wg.shape), _const_spec(wb.shape), _const_spec(wo.shape)],
        out_specs=pl.BlockSpec((TM, d), lambda i: (i, 0)),
        scratch_shapes=[pltpu.VMEM((TM, d), F32)],
        compiler_params=_cparams(("arbitrary",)),
        name="merge",
    )(h, mod, norm_w, *ys, wg, wb, wo)


def _swap_rot_pairs(w, nf):
    lead = w.shape[:-1]
    n = w.shape[-1]
    return w.reshape(lead + (n // (2 * nf), 2, nf))[..., ::-1, :].reshape(lead + (n,))


def _rope_tables(seq, dim, lane_off, width, period):
    nf = dim // 4
    pos = jnp.arange(seq)
    rows = (pos // GRID_W).astype(F32)
    cols = (pos % GRID_W).astype(F32)
    inv_freq = ROPE_BASE ** (-jnp.arange(nf, dtype=F32) / nf)
    ang_r = rows[:, None] * inv_freq[None, :]
    ang_c = cols[:, None] * inv_freq[None, :]
    cos = jnp.concatenate([jnp.cos(ang_r)] * 2 + [jnp.cos(ang_c)] * 2, axis=1)
    sin = jnp.concatenate([-jnp.sin(ang_r), jnp.sin(ang_r), -jnp.sin(ang_c), jnp.sin(ang_c)], axis=1)
    c_per = jnp.ones((seq, period), F32).at[:, lane_off:lane_off + dim].set(cos)
    s_per = jnp.zeros((seq, period), F32).at[:, lane_off:lane_off + dim].set(sin)
    reps = width // period
    c_tab = jnp.concatenate([jnp.tile(c_per, (1, reps)), jnp.ones((TM, width), F32)], axis=0)
    s_tab = jnp.concatenate([jnp.tile(s_per, (1, reps)), jnp.zeros((TM, width), F32)], axis=0)
    return c_tab, s_tab


def _inproj_weights(w_in):
    d = w_in.shape[0]
    hw = NA_HEADS * HEAD_DIM
    o_sq = 3 * hw
    o_sk = o_sq + SWA_HEADS * HEAD_DIM
    o_sv = o_sk + SWA_KV_HEADS * HEAD_DIM
    o_s5 = o_sv + SWA_KV_HEADS * HEAD_DIM
    s5w = 512
    o_cq = o_s5 + s5w
    o_ckv = o_cq + 256
    o_kr = o_ckv + 128
    o_g = o_kr + MLA_ROPE
    qscale = HEAD_DIM ** -0.5
    na = jnp.concatenate([w_in[:, :hw] * qscale, w_in[:, hw:3 * hw]], axis=1)
    sq = w_in[:, o_sq:o_sk] * qscale
    sk = w_in[:, o_sk:o_sv]
    sv = w_in[:, o_sv:o_s5]
    def dup_heads(a):
        return jnp.concatenate([a[:, kv * HEAD_DIM:(kv + 1) * HEAD_DIM]
                                for kv in range(SWA_KV_HEADS) for _ in range(2)], axis=1)

    sk_dup = dup_heads(sk)
    sv_dup = dup_heads(sv)
    wr = jnp.concatenate([sq, sk_dup], axis=1)
    kr = w_in[:, o_kr:o_g]
    lpad = jnp.zeros((d, MLA_NOPE), F32)
    rpad = jnp.zeros((d, LANES - MLA_NOPE - MLA_ROPE), F32)
    wf = jnp.concatenate([w_in[:, o_s5:o_kr], lpad, kr, rpad], axis=1)
    wg = w_in[:, o_g:].reshape(d, N_BRANCH, d).transpose(1, 0, 2)
    proj = tuple(a.astype(BF16) for a in (na, wr, sv_dup, wf))
    return proj, wg.astype(BF16)


def _mla_weights(w_uq, w_ukv):
    ql = w_uq.shape[0]
    kvl = w_ukv.shape[0]
    dq = MLA_NOPE + MLA_ROPE
    wq3 = w_uq.reshape(ql, MLA_HEADS, dq)
    pad = jnp.zeros((ql, MLA_HEADS, LANES - dq), F32)
    qscale = math.log2(math.e) * dq ** -0.5
    wq = jnp.concatenate([wq3 * qscale, pad], axis=2).reshape(ql, MLA_HEADS * LANES)
    wkv3 = w_ukv.reshape(kvl, MLA_HEADS, MLA_NOPE + MLA_V)
    wk = jnp.concatenate([wkv3[:, :, :MLA_NOPE], jnp.zeros((kvl, MLA_HEADS, LANES - MLA_NOPE), F32)], axis=2)
    wk = wk.reshape(kvl, MLA_HEADS * LANES)
    wv = jnp.concatenate([wkv3[:, :, MLA_NOPE:], jnp.zeros((kvl, MLA_HEADS, LANES - MLA_V), F32)], axis=2)
    wv = wv.reshape(kvl, MLA_HEADS * LANES)
    return tuple(a.astype(BF16) for a in (wq, wk, wv))


def _s5_params(lam_re, lam_im, log_dt, b_re, b_im, c_re, c_im):
    a = lam_re.astype(F32)
    w = lam_im.astype(F32)
    dt = jnp.exp(log_dt.astype(F32))[..., None]
    mag = jnp.exp(a * dt)
    lb_re = mag * jnp.cos(w * dt)
    lb_im = mag * jnp.sin(w * dt)
    den = a * a + w * w
    cf_re = ((lb_re - 1.0) * a + lb_im * w) / den
    cf_im = (lb_im * a - (lb_re - 1.0) * w) / den
    bb_re = cf_re[..., None] * b_re - cf_im[..., None] * b_im
    bb_im = cf_re[..., None] * b_im + cf_im[..., None] * b_re
    n_dir, g, p, cg = b_re.shape
    gpb = LANES // cg
    nblk = g // gpb
    eye = jnp.eye(gpb, dtype=F32)

    def in_map(x):
        x5 = jnp.swapaxes(x.reshape(n_dir, nblk, gpb, p, cg), 3, 4)
        full = x5[:, :, :, :, None, :] * eye[None, None, :, None, :, None]
        return full.reshape(n_dir, nblk, gpb * cg, gpb * p)

    def out_map(x):
        x5 = jnp.swapaxes(x.reshape(n_dir, nblk, gpb, cg, p), 3, 4)
        full = x5[:, :, :, :, None, :] * eye[None, None, :, None, :, None]
        return full.reshape(n_dir, nblk, gpb * p, gpb * cg)

    b_in = jnp.concatenate([in_map(bb_re), in_map(bb_im)], axis=3)
    bmat = jnp.concatenate([b_in[0], b_in[1]], axis=1).astype(BF16)
    c_out = jnp.concatenate([out_map(c_re.astype(F32)), out_map(-c_im.astype(F32))], axis=2)
    cmat = jnp.concatenate([c_out[0], c_out[1]], axis=2).astype(BF16)
    half = SUBLANES // 2
    lam2 = jnp.concatenate([lb_re.reshape(n_dir, nblk, gpb * p), lb_im.reshape(n_dir, nblk, gpb * p)], axis=2)
    lam = jnp.concatenate([jnp.broadcast_to(lam2[0][:, None, :], (nblk, half, 2 * gpb * p)),
                           jnp.broadcast_to(lam2[1][:, None, :], (nblk, half, 2 * gpb * p))], axis=1)
    return bmat, cmat, lam


def _ffn_weights(wg, wu, wd):
    d, ff = wg.shape
    wg3 = wg.astype(BF16)
    wu3 = wu.astype(BF16)
    wd3 = wd.astype(BF16)
    return wg3, wu3, wd3


def kernel(x, c, ctx, c_ctx, ada_w, ada_b, ffn1_norm, ffn1_w_gate, ffn1_w_up, ffn1_w_down, mix_norm, w_in, na_rpb, swa_sink, s5_lambda_re, s5_lambda_im, s5_log_dt, s5_b_re, s5_b_im, s5_c_re, s5_c_im, s5_d, s5_glu_w, s5_glu_b, mla_q_norm, mla_w_uq, mla_kv_norm, mla_w_ukv, w_branch, w_out, ffn2_norm, ffn2_w_gate, ffn2_w_up, ffn2_w_down, final_norm):
    n_batch, seq, d = x.shape
    n_ctx = ctx.shape[1]
    depth = ada_w.shape[0]
    assert 2 * n_batch == SUBLANES and seq % TM == 0 and (n_batch * n_ctx) % TM == 0
    n_lat = n_batch * seq
    n_all = n_lat + n_batch * n_ctx
    tiles_per_batch = seq // TM
    n_lat_tiles = n_lat // TM
    geo = dict(n_lat_tiles=n_lat_tiles, tiles_per_batch=tiles_per_batch, n_batch=n_batch)

    h = x.reshape(n_lat, d)
    h_ctx = ctx.reshape(n_batch * n_ctx, d)
    cc = jnp.concatenate([c, c_ctx[None, :], jnp.zeros((SUBLANES - n_batch - 1, d), F32)], axis=0)
    mod = _ada_mod(cc, ada_w, ada_b)

    cs_sw, sn_sw = _rope_tables(seq, HEAD_DIM, 0, 2 * LANES, HEAD_DIM)
    ck_kr, sk_kr = _rope_tables(seq, MLA_ROPE, LANES + MLA_NOPE, 2 * LANES, 2 * LANES)
    cq_ml, sq_ml = _rope_tables(seq, MLA_ROPE, MLA_NOPE, 2 * LANES, LANES)
    rows_n = seq // GRID_W
    s5w = s5_d.shape[1]
    offs = (s5w, s5w + mla_w_uq.shape[1], s5w + mla_w_uq.shape[1] + mla_w_ukv.shape[1])
    sw_col0 = 3 * NA_HEADS * HEAD_DIM

    for l in range(depth):
        need_ctx = l < depth - 1
        last = l == depth - 1
        ml = mod[l]
        ones = jnp.ones((1, d), F32)
        wg3, wu3, wd3 = _ffn_weights(ffn1_w_gate[l], ffn1_w_up[l], ffn1_w_down[l])
        h = _ffn(h, n_all, ml, ffn1_norm[l][None, :], wg3, wu3, wd3, ones, base=0, final=False,
                 h_tail=h_ctx if l == 0 else None, **geo)
        proj_w, gate_w = _inproj_weights(w_in[l])
        pb, pf = _inproj(h, ml, mix_norm[l][None, :], proj_w, (cs_sw, sn_sw, ck_kr, sk_kr), **geo)
        bias = _na_bias_table(na_rpb[l].astype(F32), rows_n)
        y_na = _na_attention(pb, bias, n_batch=n_batch, seq=seq, ctx=n_ctx, need_ctx=need_ctx)
        y_sw_l, y_sw_c = _swa_attention(pb, swa_sink[l].astype(F32), n_batch=n_batch, seq=seq, ctx=n_ctx,
                                        need_ctx=need_ctx, col0=sw_col0)
        y_sw = jnp.concatenate([y_sw_l, y_sw_c], axis=0) if need_ctx else y_sw_l
        wq, wk, wv = _mla_weights(mla_w_uq[l], mla_w_ukv[l])
        qm, km, vm = _mla_prep(pf, mla_q_norm[l][None, :], mla_kv_norm[l][None, :], wq, wk, wv, cq_ml, sq_ml,
                               n_lat_tiles=n_lat_tiles, tiles_per_batch=tiles_per_batch, offs=offs)
        y_mla = _mla_attention(qm, km, vm, n_batch=n_batch, seq=seq, ctx=n_ctx, need_ctx=need_ctx)
        s5p = _s5_params(s5_lambda_re[l], s5_lambda_im[l], s5_log_dt[l], s5_b_re[l], s5_b_im[l],
                         s5_c_re[l], s5_c_im[l])
        yf, yr = _s5_scan(pf, *s5p, n_batch=n_batch, seq=seq, ctx=n_ctx, width=s5w)
        y_s5 = _s5_post(pf, yf, yr, s5_d[l][None, :].astype(F32), s5_glu_w[l].astype(BF16),
                        s5_glu_b[l][None, :].astype(F32), n_batch=n_batch, seq=seq, ctx=n_ctx)
        n_rows = n_all if need_ctx else n_lat
        h = _merge(h, n_rows, ml, mix_norm[l][None, :], (y_na, y_sw, y_s5, y_mla), gate_w,
                   w_branch[l].astype(BF16), w_out[l].astype(BF16), **geo)
        wg3, wu3, wd3 = _ffn_weights(ffn2_w_gate[l], ffn2_w_up[l], ffn2_w_down[l])
        h = _ffn(h, n_rows, ml, ffn2_norm[l][None, :], wg3, wu3, wd3, final_norm[None, :], base=6, final=last, **geo)
    return h.reshape(n_batch, seq, d)
```

```python
import functools
import math

import numpy as np
import jax
import jax.numpy as jnp
from jax import lax
from jax.experimental import pallas as pl
from jax.experimental.pallas import tpu as pltpu

F32 = jnp.float32
BF16 = jnp.bfloat16

GRID_W = 64
HEAD_DIM = 64
N_BRANCH = 4
NA_HEADS = 8
NA_WIN_ROWS = 8
NA_WIN_COLS = 16
SWA_HEADS = 8
SWA_KV_HEADS = 2
SWA_WINDOW = 128
S5_GROUP = 16
S5_STATE = 64
MLA_HEADS = 8
MLA_NOPE = 64
MLA_ROPE = 32
MLA_V = 64
MACARON_WEIGHT = 0.5
ROPE_BASE = 10000.0
EPS = 1e-6
N_MOD = 9

LANES = 128
SUBLANES = 8
TM = 512
FF_CHUNK = 256
NA_QROWS = 4
NA_KROWS = NA_QROWS + NA_WIN_ROWS - 1
SWA_BLK = 128
SWA_UNROLL = 2
MLA_TQ = 512
MLA_SUB = 256
S5_STEPS = 128
S5_PITCH = S5_STEPS + SUBLANES
NEG = -1e30
LOG2E = math.log2(math.e)
VMEM_LIMIT = 56 * 1024 * 1024


def _cparams(sem, flags=None):
    return pltpu.CompilerParams(dimension_semantics=sem, vmem_limit_bytes=VMEM_LIMIT, flags=flags)


def _const_spec(shape):
    nd = len(shape)
    return pl.BlockSpec(shape, lambda *_: (0,) * nd, pipeline_mode=pl.Buffered(1))


def _dot(a, b):
    return jnp.dot(a, b, preferred_element_type=F32)


def _dot_t(a, b):
    return lax.dot_general(a, b, (((1,), (1,)), ((), ())), preferred_element_type=F32)


def _rms(x, w):
    return x * lax.rsqrt(jnp.mean(x * x, axis=-1, keepdims=True) + EPS) * w


def _rms_mod(x, w, shift, scale):
    return _rms(x, w) * (1.0 + scale) + shift


def _mod_row_map(n_lat_tiles, tiles_per_batch, n_batch):
    def f(i):
        return jnp.where(i < n_lat_tiles, i // tiles_per_batch, n_batch)
    return f


def _ada_kernel(c_ref, w_ref, b_ref, o_ref):
    c = c_ref[...]
    s = c * jax.nn.sigmoid(c)
    w = w_ref[0]
    s_hi = s.astype(BF16)
    s_lo = (s - s_hi.astype(F32)).astype(BF16)
    w_hi = w.astype(BF16)
    w_lo = (w - w_hi.astype(F32)).astype(BF16)
    o_ref[0] = _dot(s_hi, w_hi) + (_dot(s_hi, w_lo) + _dot(s_lo, w_hi)) + b_ref[0]


def _ada_mod(cc, ada_w, ada_b):
    depth, d, nd = ada_w.shape
    tn = 1024
    out = pl.pallas_call(
        _ada_kernel,
        out_shape=jax.ShapeDtypeStruct((depth, SUBLANES, nd), F32),
        grid=(depth, nd // tn),
        in_specs=[pl.BlockSpec((SUBLANES, d), lambda l, j: (0, 0)),
                  pl.BlockSpec((1, d, tn), lambda l, j: (l, 0, j)),
                  pl.BlockSpec((1, 1, tn), lambda l, j: (l, 0, j))],
        out_specs=pl.BlockSpec((1, SUBLANES, tn), lambda l, j: (l, 0, j)),
        compiler_params=_cparams(("arbitrary", "arbitrary")),
        name="ada_mod",
    )(cc, ada_w, ada_b.reshape(depth, 1, nd))
    return out.reshape(depth, SUBLANES, N_MOD, d)


def _ffn_kernel(x_ref, xc_ref, mod_ref, nw_ref, wg_ref, wu_ref, wd_ref, fw_ref, o_ref, *, base, final, n_first):
    x = x_ref[...]
    if n_first is not None:
        x = jnp.where(pl.program_id(0) < n_first, x, xc_ref[...])
    m = mod_ref[0]
    n = _rms_mod(x, nw_ref[...], m[base:base + 1], m[base + 1:base + 2]).astype(BF16)
    ff = wg_ref.shape[1]
    fc = FF_CHUNK if ff % FF_CHUNK == 0 else ff
    acc = None
    for c in range(ff // fc):
        cs = slice(c * fc, (c + 1) * fc)
        g = _dot(n, wg_ref[:, cs])
        u = _dot(n, wu_ref[:, cs])
        a = (g * jax.nn.sigmoid(g) * u).astype(BF16)
        y = _dot(a, wd_ref[cs, :])
        acc = y if acc is None else acc + y
    out = x + MACARON_WEIGHT * m[base + 2:base + 3] * acc
    if final:
        out = _rms(out, fw_ref[...])
    o_ref[...] = out


def _ffn(h, n_rows, mod, norm_w, wg, wu, wd, final_w, *, base, final, n_lat_tiles, tiles_per_batch, n_batch,
         h_tail=None):
    d = h.shape[1]
    ff = wg.shape[1]
    if h_tail is None:
        n_first = None
        h_tail = h
        x_map = lambda i: (i, 0)
        t_map = lambda i: (0, 0)
    else:
        n_first = h.shape[0] // TM
        x_map = lambda i: (jnp.minimum(i, n_first - 1), 0)
        t_map = lambda i: (jnp.maximum(i - n_first, 0), 0)
    kern = functools.partial(_ffn_kernel, base=base, final=final, n_first=n_first)
    return pl.pallas_call(
        kern,
        out_shape=jax.ShapeDtypeStruct((n_rows, d), F32),
        grid=(n_rows // TM,),
        in_specs=[pl.BlockSpec((TM, d), x_map),
                  pl.BlockSpec((TM, d) if n_first is not None else (SUBLANES, d), t_map),
                  pl.BlockSpec((1, N_MOD, d), lambda i: (_mod_row_map(n_lat_tiles, tiles_per_batch, n_batch)(i), 0, 0)),
                  _const_spec((1, d)),
                  _const_spec((d, ff)), _const_spec((d, ff)), _const_spec((ff, d)),
                  _const_spec((1, d))],
        out_specs=pl.BlockSpec((TM, d), lambda i: (i, 0)),
        compiler_params=_cparams(("arbitrary",)),
        name="ffn_final" if final else "ffn",
    )(h, h_tail, mod, norm_w, wg, wu, wd, final_w)


def _rope_apply(y, cs, sn, nf):
    w = y.shape[1]
    first = (lax.broadcasted_iota(jnp.int32, y.shape, 1) & nf) == 0
    ysw = jnp.where(first, pltpu.roll(y, w - nf, 1), pltpu.roll(y, nf, 1))
    return y * cs + ysw * sn


def _inproj_kernel(x_ref, mod_ref, nw_ref, wa_ref, wr_ref, wv_ref, wf_ref,
                   cs_ref, sn_ref, ck_ref, sk_ref, ob_ref, of_ref):
    x = x_ref[...]
    m = mod_ref[0]
    n = _rms_mod(x, nw_ref[...], m[3:4], m[4:5]).astype(BF16)
    na = wa_ref.shape[1]
    nr = wr_ref.shape[1]
    nv = wv_ref.shape[1]
    nf = wf_ref.shape[1]
    cw = 2 * LANES
    for c in range(na // cw):
        ob_ref[:, c * cw:(c + 1) * cw] = _dot(n, wa_ref[:, c * cw:(c + 1) * cw]).astype(BF16)
    for c in range(nr // cw):
        y = _dot(n, wr_ref[:, c * cw:(c + 1) * cw])
        ob_ref[:, na + c * cw:na + (c + 1) * cw] = _rope_apply(y, cs_ref[...], sn_ref[...], HEAD_DIM // 4).astype(BF16)
    for c in range(nv // cw):
        ob_ref[:, na + nr + c * cw:na + nr + (c + 1) * cw] = _dot(n, wv_ref[:, c * cw:(c + 1) * cw]).astype(BF16)
    for c in range(nf // cw - 1):
        of_ref[:, c * cw:(c + 1) * cw] = _dot(n, wf_ref[:, c * cw:(c + 1) * cw])
    y = _dot(n, wf_ref[:, nf - cw:nf])
    of_ref[:, nf - cw:nf] = _rope_apply(y, ck_ref[...], sk_ref[...], MLA_ROPE // 4)


def _inproj(h, mod, norm_w, w, tabs, *, n_lat_tiles, tiles_per_batch, n_batch):
    n_rows, d = h.shape
    wa, wr, wv, wf = w
    cs, sn, ck, sk = tabs
    nb = wa.shape[1] + wr.shape[1] + wv.shape[1]
    nf = wf.shape[1]
    tab_map = lambda i: (jnp.where(i < n_lat_tiles, i % tiles_per_batch, tiles_per_batch), 0)
    return pl.pallas_call(
        _inproj_kernel,
        out_shape=(jax.ShapeDtypeStruct((n_rows, nb), BF16), jax.ShapeDtypeStruct((n_rows, nf), F32)),
        grid=(n_rows // TM,),
        in_specs=[pl.BlockSpec((TM, d), lambda i: (i, 0)),
                  pl.BlockSpec((1, N_MOD, d), lambda i: (_mod_row_map(n_lat_tiles, tiles_per_batch, n_batch)(i), 0, 0)),
                  _const_spec((1, d))]
                 + [_const_spec(a.shape) for a in w]
                 + [pl.BlockSpec((TM, 2 * LANES), tab_map)] * 4,
        out_specs=(pl.BlockSpec((TM, nb), lambda i: (i, 0)), pl.BlockSpec((TM, nf), lambda i: (i, 0))),
        compiler_params=_cparams(("arbitrary",)),
        name="inproj",
    )(h, mod, norm_w, *w, cs, sn, ck, sk)


def _lane_half_masks(rows):
    lane = lax.broadcasted_iota(jnp.int32, (rows, LANES), 1)
    return lane < HEAD_DIM


def _with_ones(v):
    return jnp.concatenate([v, jnp.ones_like(v)], axis=1)


def _na_kernel(q_ref, k_ref, v_ref, kc_ref, vc_ref, bias_ref, o_ref, *, n_blk, rows_n):
    blk = pl.program_id(1)
    nq = NA_QROWS * GRID_W
    nk = NA_KROWS * GRID_W
    ws = jnp.clip(NA_QROWS * blk - NA_WIN_ROWS // 2, 0, rows_n - NA_KROWS)
    r0 = pl.multiple_of(ws * GRID_W, GRID_W)
    lo = _lane_half_masks(nq)
    for j in range(NA_HEADS // 2):
        sl = slice(j * LANES, (j + 1) * LANES)
        qp = q_ref[:, sl]
        kw = k_ref[pl.ds(r0, nk), sl]
        vw = v_ref[pl.ds(r0, nk), sl]
        kc = kc_ref[:, sl]
        vc = vc_ref[:, sl]
        zero = jnp.zeros_like(qp)
        qs = jnp.concatenate([jnp.where(lo, qp, zero), jnp.where(lo, zero, qp)], axis=0)
        s1 = _dot_t(qs, kw) + bias_ref[0, j]
        s2 = _dot_t(qs, kc)
        m = jnp.maximum(jnp.max(s1, axis=-1, keepdims=True), jnp.max(s2, axis=-1, keepdims=True))
        o2 = (_dot(jnp.exp2(s1 - m).astype(BF16), _with_ones(vw))
              + _dot(jnp.exp2(s2 - m).astype(BF16), _with_ones(vc)))
        o = o2[:, :LANES] / o2[:, LANES:]
        o_ref[:, sl] = jnp.where(lo, o[:nq], o[nq:]).astype(BF16)


def _na_attention(pb, bias, *, n_batch, seq, ctx, need_ctx):
    rows_n = seq // GRID_W
    n_blk = rows_n // NA_QROWS
    nq = NA_QROWS * GRID_W
    hw = NA_HEADS * HEAD_DIM
    n_q = n_blk + (1 if need_ctx else 0)
    assert ctx == nq
    lat_blocks = n_batch * seq // nq
    qmap = lambda b, i: (jnp.where(i < n_blk, b * n_blk + i, lat_blocks + b), 0)

    def bias_map(b, i):
        t = jnp.where(i == 0, 0, jnp.where(i == 1, 1, jnp.where(i == n_blk - 1, 3, jnp.where(i == n_blk, 4, 2))))
        return (t, 0, 0, 0)

    kern = functools.partial(_na_kernel, n_blk=n_blk, rows_n=rows_n)
    return pl.pallas_call(
        kern,
        out_shape=jax.ShapeDtypeStruct((n_batch * n_q * nq, hw), BF16),
        grid=(n_batch, n_q),
        in_specs=[pl.BlockSpec((nq, hw), qmap),
                  pl.BlockSpec((seq, hw), lambda b, i: (b, 1)),
                  pl.BlockSpec((seq, hw), lambda b, i: (b, 2)),
                  pl.BlockSpec((ctx, hw), lambda b, i: (n_batch * seq // ctx + b, 1)),
                  pl.BlockSpec((ctx, hw), lambda b, i: (n_batch * seq // ctx + b, 2)),
                  pl.BlockSpec((1, NA_HEADS // 2, 2 * nq, NA_KROWS * GRID_W), bias_map)],
        out_specs=pl.BlockSpec((nq, hw), qmap),
        compiler_params=_cparams(("arbitrary", "arbitrary")),
        name="na_attn",
    )(pb, pb, pb, pb, pb, bias)


def _na_bias_table(rpb, rows_n):
    n_blk = rows_n // NA_QROWS
    kr_n = min(NA_WIN_ROWS, rows_n)
    n_heads = rpb.shape[0]
    col = np.arange(GRID_W)
    c0 = np.clip(col - NA_WIN_COLS // 2, 0, GRID_W - NA_WIN_COLS)
    col_ok = (col[None, :] >= c0[:, None]) & (col[None, :] < c0[:, None] + NA_WIN_COLS)
    padded = jnp.pad(rpb * LOG2E, ((0, 0), (0, 0), (GRID_W, GRID_W)))
    off = GRID_W + NA_WIN_COLS - 1
    toep = jnp.stack([padded[:, :, off - qc:off - qc + GRID_W] for qc in range(GRID_W)], axis=2)
    toep = jnp.where(col_ok[None, None], toep, NEG)
    masked = jnp.full((n_heads, 1, GRID_W, GRID_W), NEG, F32)
    ext = jnp.concatenate([masked, toep, masked], axis=1)
    pair2 = jnp.concatenate([ext[:, :-1], ext[:, 1:]], axis=-1)
    n_dr = 2 * NA_WIN_ROWS - 1
    plans = []
    for blk in (0, 1, 2, n_blk - 1):
        ws = int(np.clip(NA_QROWS * blk - NA_WIN_ROWS // 2, 0, rows_n - NA_KROWS))
        plan = []
        for qr in range(NA_QROWS):
            r = NA_QROWS * blk + qr
            r0 = int(np.clip(r - kr_n // 2, 0, rows_n - kr_n))
            row = []
            for kk in range(0, NA_KROWS, 2):
                oks = tuple(r0 <= ws + kk + i < r0 + kr_n and kk + i < NA_KROWS for i in range(2))
                row.append((ws + kk - r + NA_WIN_ROWS, oks))
            plan.append(tuple(row))
        plans.append(tuple(plan))
    plans.append(None)
    nq = NA_QROWS * GRID_W
    nk = NA_KROWS * GRID_W
    kern = functools.partial(_na_bias_kernel, plans=tuple(plans))
    return pl.pallas_call(
        kern,
        out_shape=jax.ShapeDtypeStruct((len(plans), n_heads // 2, 2 * nq, nk), F32),
        grid=(n_heads // 2,),
        in_specs=[pl.BlockSpec((2, n_dr + 1, GRID_W, 2 * GRID_W), lambda j: (j, 0, 0, 0))],
        out_specs=pl.BlockSpec((len(plans), 1, 2 * nq, nk), lambda j: (0, j, 0, 0)),
        compiler_params=_cparams(("arbitrary",)),
        name="na_bias",
    )(pair2)


def _na_bias_kernel(p2_ref, o_ref, *, plans):
    nq = NA_QROWS * GRID_W
    lo = lax.broadcasted_iota(jnp.int32, (GRID_W, 2 * GRID_W), 1) < GRID_W
    neg = jnp.full((GRID_W, 2 * GRID_W), NEG, F32)
    for t, plan in enumerate(plans):
        if plan is None:
            o_ref[t, 0] = jnp.full(o_ref.shape[2:], NEG, F32)
            continue
        for half in range(2):
            for qr, row in enumerate(plan):
                rs = slice(half * nq + qr * GRID_W, half * nq + (qr + 1) * GRID_W)
                for kp, (e, (ok_a, ok_b)) in enumerate(row):
                    width = min(2 * GRID_W, o_ref.shape[3] - kp * 2 * GRID_W)
                    if ok_a or ok_b:
                        tile = p2_ref[half, e]
                        if not ok_a:
                            tile = jnp.where(lo, neg, tile)
                        if not ok_b:
                            tile = jnp.where(lo, tile, neg)
                    else:
                        tile = neg
                    o_ref[t, 0, rs, kp * 2 * GRID_W:kp * 2 * GRID_W + width] = tile[:, :width]


def _swa_kernel(sink_ref, q_ref, k_ref, v_ref, qc_ref, kc_ref, vc_ref, o_ref, oc_ref, *, seq, need_ctx):
    n_blk = seq // SWA_BLK
    band = 3 * SWA_BLK
    group = SWA_HEADS // SWA_KV_HEADS
    pairs = group // 2

    def group_attn(q_slabs, kv, kb, vb, kc, vc, mask_bias):
        r = q_slabs[0].shape[0]
        lo = _lane_half_masks(r)
        parts = []
        for qp in q_slabs:
            zero = jnp.zeros_like(qp)
            parts += [jnp.where(lo, qp, zero), jnp.where(lo, zero, qp)]
        qs = jnp.concatenate(parts, axis=0)
        row = lax.broadcasted_iota(jnp.int32, (group * r, 1), 0)
        sink = jnp.full((group * r, 1), sink_ref[kv * group + group - 1], F32)
        for g in range(group - 2, -1, -1):
            sink = jnp.where(row < (g + 1) * r, sink_ref[kv * group + g], sink)
        s2 = _dot_t(qs, kc)
        if kb is None:
            m = jnp.maximum(jnp.max(s2, axis=-1, keepdims=True), sink)
            o2 = _dot(jnp.exp2(s2 - m).astype(BF16), _with_ones(vc))
        else:
            s1 = _dot_t(qs, kb) + mask_bias
            m = jnp.maximum(jnp.maximum(jnp.max(s1, axis=-1, keepdims=True), jnp.max(s2, axis=-1, keepdims=True)), sink)
            o2 = (_dot(jnp.exp2(s1 - m).astype(BF16), _with_ones(vb))
                  + _dot(jnp.exp2(s2 - m).astype(BF16), _with_ones(vc)))
        o = o2[:, :LANES] / (o2[:, LANES:] + jnp.exp2(sink - m))
        return [jnp.where(lo, o[(2 * i) * r:(2 * i + 1) * r], o[(2 * i + 1) * r:(2 * i + 2) * r]) for i in range(pairs)]

    qi = lax.broadcasted_iota(jnp.int32, (group * SWA_BLK, band), 0) & (SWA_BLK - 1)
    ki = lax.broadcasted_iota(jnp.int32, (group * SWA_BLK, band), 1)

    def one_block(n):
        start = jnp.clip((n - 1) * SWA_BLK, 0, seq - band)
        start = pl.multiple_of(start, SWA_BLK)
        q0 = pl.multiple_of(n * SWA_BLK, SWA_BLK)
        delta = (start + ki) - (q0 + qi)
        mask_bias = jnp.where(jnp.abs(delta) <= SWA_WINDOW, 0.0, NEG).astype(F32)
        for kv in range(SWA_KV_HEADS):
            ksl = slice(kv * LANES, (kv + 1) * LANES)
            slabs = [q_ref[pl.ds(q0, SWA_BLK), (kv * pairs + i) * LANES:(kv * pairs + i + 1) * LANES]
                     for i in range(pairs)]
            outs = group_attn(slabs, kv, k_ref[pl.ds(start, band), ksl], v_ref[pl.ds(start, band), ksl],
                              kc_ref[:, ksl], vc_ref[:, ksl], mask_bias)
            for i in range(pairs):
                o_ref[pl.ds(q0, SWA_BLK), (kv * pairs + i) * LANES:(kv * pairs + i + 1) * LANES] = outs[i].astype(BF16)

    def blk_body(n2, carry):
        for j in range(SWA_UNROLL):
            one_block(n2 * SWA_UNROLL + j)
        return carry

    lax.fori_loop(0, n_blk // SWA_UNROLL, blk_body, 0)

    if need_ctx:
        for kv in range(SWA_KV_HEADS):
            ksl = slice(kv * LANES, (kv + 1) * LANES)
            slabs = [qc_ref[:, (kv * pairs + i) * LANES:(kv * pairs + i + 1) * LANES] for i in range(pairs)]
            outs = group_attn(slabs, kv, None, None, kc_ref[:, ksl], vc_ref[:, ksl], None)
            for i in range(pairs):
                oc_ref[:, (kv * pairs + i) * LANES:(kv * pairs + i + 1) * LANES] = outs[i].astype(BF16)
    else:
        oc_ref[...] = jnp.zeros_like(oc_ref)


def _swa_attention(pb, sink, *, n_batch, seq, ctx, need_ctx, col0):
    hw = SWA_HEADS * HEAD_DIM
    kw = SWA_KV_HEADS * LANES
    qcol = col0 // hw
    kcol = (col0 + hw) // kw
    vcol = kcol + 1
    cblk = n_batch * seq // ctx
    kern = functools.partial(_swa_kernel, seq=seq, need_ctx=need_ctx)
    return pl.pallas_call(
        kern,
        out_shape=(jax.ShapeDtypeStruct((n_batch * seq, hw), BF16),
                   jax.ShapeDtypeStruct((n_batch * ctx, hw), BF16)),
        grid=(n_batch,),
        in_specs=[pl.BlockSpec(memory_space=pltpu.SMEM),
                  pl.BlockSpec((seq, hw), lambda b: (b, qcol)),
                  pl.BlockSpec((seq, kw), lambda b: (b, kcol)),
                  pl.BlockSpec((seq, kw), lambda b: (b, vcol)),
                  pl.BlockSpec((ctx, hw), lambda b: (cblk + b, qcol)),
                  pl.BlockSpec((ctx, kw), lambda b: (cblk + b, kcol)),
                  pl.BlockSpec((ctx, kw), lambda b: (cblk + b, vcol))],
        out_specs=(pl.BlockSpec((seq, hw), lambda b: (b, 0)),
                   pl.BlockSpec((ctx, hw), lambda b: (b, 0))),
        compiler_params=_cparams(("arbitrary",)),
        name="swa_attn",
    )(sink, pb, pb, pb, pb, pb, pb)


def _mla_prep_kernel(pf_ref, qn_ref, kn_ref, wq_ref, wk_ref, wv_ref, cq_ref, sq_ref,
                     q_ref, k_ref, v_ref, *, off_cq, off_ckv, off_kr, q_lora, kv_lora):
    cq = pf_ref[:, off_cq:off_cq + q_lora]
    ckv = pf_ref[:, off_ckv:off_ckv + kv_lora]
    kr = pf_ref[:, off_kr:off_kr + LANES]
    kr2 = jnp.concatenate([kr, kr], axis=1)
    cqn = _rms(cq, qn_ref[...]).astype(BF16)
    ckvn = _rms(ckv, kn_ref[...]).astype(BF16)
    cw = 2 * LANES
    for c in range(MLA_HEADS * LANES // cw):
        sl = slice(c * cw, (c + 1) * cw)
        q = _rope_apply(_dot(cqn, wq_ref[:, sl]), cq_ref[...], sq_ref[...], MLA_ROPE // 4)
        q_ref[:, sl] = q.astype(BF16)
        k_ref[:, sl] = (_dot(ckvn, wk_ref[:, sl]) + kr2).astype(BF16)
    ones_hi = jnp.where(_lane_half_masks(1), 0.0, 1.0).astype(F32)
    ones_hi = jnp.concatenate([ones_hi, ones_hi], axis=1)
    for c in range(MLA_HEADS * LANES // cw):
        sl = slice(c * cw, (c + 1) * cw)
        v_ref[:, sl] = (_dot(ckvn, wv_ref[:, sl]) + ones_hi).astype(BF16)


def _mla_prep(pf, qn, kn, wq, wk, wv, cq_tab, sq_tab, *, n_lat_tiles, tiles_per_batch, offs):
    n_rows = pf.shape[0]
    off_cq, off_ckv, off_kr = offs
    tab_map = lambda i: (jnp.where(i < n_lat_tiles, i % tiles_per_batch, tiles_per_batch), 0)
    kern = functools.partial(_mla_prep_kernel, off_cq=off_cq, off_ckv=off_ckv, off_kr=off_kr,
                             q_lora=wq.shape[0], kv_lora=wk.shape[0])
    hq = MLA_HEADS * LANES
    hv = MLA_HEADS * LANES
    return pl.pallas_call(
        kern,
        out_shape=(jax.ShapeDtypeStruct((n_rows, hq), BF16), jax.ShapeDtypeStruct((n_rows, hq), BF16),
                   jax.ShapeDtypeStruct((n_rows, hv), BF16)),
        grid=(n_rows // TM,),
        in_specs=[pl.BlockSpec((TM, pf.shape[1]), lambda i: (i, 0)),
                  _const_spec(qn.shape), _const_spec(kn.shape),
                  _const_spec(wq.shape), _const_spec(wk.shape), _const_spec(wv.shape),
                  pl.BlockSpec((TM, 2 * LANES), tab_map), pl.BlockSpec((TM, 2 * LANES), tab_map)],
        out_specs=(pl.BlockSpec((TM, hq), lambda i: (i, 0)), pl.BlockSpec((TM, hq), lambda i: (i, 0)),
                   pl.BlockSpec((TM, hv), lambda i: (i, 0))),
        compiler_params=_cparams(("arbitrary",)),
        name="mla_prep",
    )(pf, qn, kn, wq, wk, wv, cq_tab, sq_tab)


def _mla_body(q_ref, k_ref, v_ref, kc_ref, vc_ref, o_ref):
    sub = MLA_SUB
    n_sub = q_ref.shape[0] // sub
    lo = _lane_half_masks(sub)
    units = [(r, half) for r in range(n_sub) for half in range(2)]

    def scores(r, half):
        sl = slice(half * LANES, (half + 1) * LANES)
        q = q_ref[r * sub:(r + 1) * sub, sl]
        s1 = None if k_ref is None else _dot_t(q, k_ref[:, sl])
        return s1, _dot_t(q, kc_ref[:, sl])

    def finish(half, s1, s2):
        sl = slice(half * LANES, (half + 1) * LANES)
        if s1 is None:
            m = jnp.max(s2, axis=-1, keepdims=True)
            return _dot(jnp.exp2(s2 - m).astype(BF16), vc_ref[:, sl])
        m = jnp.maximum(jnp.max(s1, axis=-1, keepdims=True), jnp.max(s2, axis=-1, keepdims=True))
        p1 = jnp.exp2(s1 - m)
        p2 = jnp.exp2(s2 - m)
        return _dot(p1.astype(BF16), v_ref[:, sl]) + _dot(p2.astype(BF16), vc_ref[:, sl])

    outs = {}
    pending = scores(*units[0])
    for idx, u in enumerate(units):
        following = scores(*units[idx + 1]) if idx + 1 < len(units) else None
        outs[u] = finish(u[1], *pending)
        pending = following
    for r in range(n_sub):
        o0, o1 = outs[(r, 0)], outs[(r, 1)]
        r0 = pltpu.roll(o0, HEAD_DIM, 1)
        r1 = pltpu.roll(o1, HEAD_DIM, 1)
        o_ref[r * sub:(r + 1) * sub, :] = jnp.where(lo, o0 / r0, r1 / o1).astype(BF16)


def _mla_kernel(q_ref, k_ref, v_ref, kc_ref, vc_ref, o_ref):
    _mla_body(q_ref, k_ref, v_ref, kc_ref, vc_ref, o_ref)


def _mla_ctx_kernel(q_ref, kc_ref, vc_ref, o_ref):
    _mla_body(q_ref, None, None, kc_ref, vc_ref, o_ref)


def _mla_attention(qm, km, vm, *, n_batch, seq, ctx, need_ctx):
    n_qt = seq // MLA_TQ
    cblk = n_batch * seq // ctx
    hv = MLA_HEADS * MLA_V
    y_lat = pl.pallas_call(
        _mla_kernel,
        out_shape=jax.ShapeDtypeStruct((n_batch * seq, hv), BF16),
        grid=(n_batch, MLA_HEADS // 2, n_qt),
        in_specs=[pl.BlockSpec((MLA_TQ, 2 * LANES), lambda b, p, i: (b * n_qt + i, p)),
                  pl.BlockSpec((seq, 2 * LANES), lambda b, p, i: (b, p)),
                  pl.BlockSpec((seq, 2 * LANES), lambda b, p, i: (b, p)),
                  pl.BlockSpec((ctx, 2 * LANES), lambda b, p, i: (cblk + b, p)),
                  pl.BlockSpec((ctx, 2 * LANES), lambda b, p, i: (cblk + b, p))],
        out_specs=pl.BlockSpec((MLA_TQ, LANES), lambda b, p, i: (b * n_qt + i, p)),
        compiler_params=_cparams(("arbitrary", "arbitrary", "arbitrary")),
        name="mla_attn",
    )(qm, km, vm, km, vm)
    if not need_ctx:
        return y_lat
    assert ctx % MLA_SUB == 0
    y_ctx = pl.pallas_call(
        _mla_ctx_kernel,
        out_shape=jax.ShapeDtypeStruct((n_batch * ctx, hv), BF16),
        grid=(n_batch, MLA_HEADS // 2),
        in_specs=[pl.BlockSpec((ctx, 2 * LANES), lambda b, p: (cblk + b, p)),
                  pl.BlockSpec((ctx, 2 * LANES), lambda b, p: (cblk + b, p)),
                  pl.BlockSpec((ctx, 2 * LANES), lambda b, p: (cblk + b, p))],
        out_specs=pl.BlockSpec((ctx, LANES), lambda b, p: (b, p)),
        compiler_params=_cparams(("arbitrary", "arbitrary")),
        name="mla_ctx_attn",
    )(qm, km, vm)
    return jnp.concatenate([y_lat, y_ctx], axis=0)


def _s5_scan_kernel(uf0_ref, uf1_ref, uf2_ref, uf3_ref, ur0_ref, ur1_ref, ur2_ref, ur3_ref,
                    bmat_ref, cmat_ref, lam_ref, of_ref, or_ref,
                    h_ref, stage_ref, lhs_ref, bu_ref, st_ref, ysc_ref, ya_ref, yb_ref):
    c = pl.program_id(0)
    n_seq = SUBLANES
    half = n_seq // 2
    steps = S5_STEPS
    pitch = S5_PITCH
    rows = steps * n_seq
    nblk = h_ref.shape[0]
    sw = h_ref.shape[2] // 2
    ufs = (uf0_ref, uf1_ref, uf2_ref, uf3_ref)
    urs = (ur0_ref, ur1_ref, ur2_ref, ur3_ref)
    is_fwd = (lax.broadcasted_iota(jnp.int32, (rows, 1), 0) % n_seq) < half
    sub_fwd = lax.broadcasted_iota(jnp.int32, (n_seq, LANES), 0) < half

    @pl.when(c == 0)
    def _():
        h_ref[...] = jnp.zeros_like(h_ref)

    def project_in(k):
        ks = slice(k * LANES, (k + 1) * LANES)
        z = k % 2
        for b in range(half):
            stage_ref[k, b * pitch:b * pitch + steps, :] = ufs[b][:, ks]
            stage_ref[k, (half + b) * pitch:(half + b) * pitch + steps, :] = urs[b][:, ks]
        for t in range(steps):
            ga = stage_ref[k, pl.ds(t, n_seq, stride=pitch), :]
            gb = stage_ref[k, pl.ds(steps - 1 - t, n_seq, stride=pitch), :]
            lhs_ref[z, t * n_seq:(t + 1) * n_seq, 0:LANES] = jnp.where(sub_fwd, ga, 0.0)
            lhs_ref[z, t * n_seq:(t + 1) * n_seq, LANES:2 * LANES] = jnp.where(sub_fwd, 0.0, gb)
        bu_ref[k] = _dot(lhs_ref[z].astype(BF16), bmat_ref[k])

    def recur(k):
        z = k % 2
        lre = lam_ref[k, :, 0:sw]
        lim = lam_ref[k, :, sw:2 * sw]
        hr = h_ref[k, :, 0:sw]
        hi = h_ref[k, :, sw:2 * sw]
        for t in range(steps):
            rs = slice(t * n_seq, (t + 1) * n_seq)
            hr, hi = (lre * hr - lim * hi + bu_ref[k, rs, 0:sw],
                      lre * hi + lim * hr + bu_ref[k, rs, sw:2 * sw])
            st_ref[z, rs, 0:sw] = hr
            st_ref[z, rs, sw:2 * sw] = hi
        h_ref[k, :, 0:sw] = hr
        h_ref[k, :, sw:2 * sw] = hi

    def read_out(k):
        ks = slice(k * LANES, (k + 1) * LANES)
        z = k % 2
        hr_rows = rows // 2
        for part in range(2):
            rs = slice(part * hr_rows, (part + 1) * hr_rows)
            y2 = _dot(st_ref[z, rs, :].astype(BF16), cmat_ref[k])
            ysc_ref[z, rs, :] = jnp.where(is_fwd[rs], y2[:, 0:LANES], y2[:, LANES:2 * LANES])
        for t in range(steps):
            g = ysc_ref[z, t * n_seq:(t + 1) * n_seq, :]
            ya_ref[k, pl.ds(t, n_seq, stride=pitch), :] = g
            yb_ref[k, pl.ds(steps - 1 - t, n_seq, stride=pitch), :] = g
        for b in range(half):
            of_ref[b, :, ks] = ya_ref[k, b * pitch:b * pitch + steps, :]
            or_ref[b, :, ks] = yb_ref[k, (half + b) * pitch:(half + b) * pitch + steps, :]

    for k in range(nblk):
        project_in(k)
    recur(0)
    for k in range(1, nblk):
        recur(k)
        read_out(k - 1)
    read_out(nblk - 1)


def _s5_scan(pf, bmat, cmat, lam, *, n_batch, seq, ctx, width):
    steps = S5_STEPS
    nblk = width // LANES
    rows = steps * SUBLANES
    sw2 = bmat.shape[-1]
    ncc = ctx // steps
    nlc = seq // steps
    ctx0 = n_batch * nlc

    def fwd_map(b):
        return lambda c: (jnp.where(c < ncc, ctx0 + b * ncc + c, b * nlc + (c - ncc)), 0)

    def rev_map(b):
        return lambda c: (jnp.where(c < ncc, ctx0 + b * ncc + (ncc - 1 - c), b * nlc + (nlc - 1 - (c - ncc))), 0)

    of_map = lambda c: (0, jnp.where(c < ncc, nlc + c, c - ncc), 0)
    or_map = lambda c: (0, jnp.where(c < ncc, nlc + (ncc - 1 - c), nlc - 1 - (c - ncc)), 0)
    u_specs = ([pl.BlockSpec((steps, width), fwd_map(b)) for b in range(n_batch)]
               + [pl.BlockSpec((steps, width), rev_map(b)) for b in range(n_batch)])
    out_sds = jax.ShapeDtypeStruct((n_batch, seq + ctx, width), F32)
    stage = pltpu.VMEM((nblk, SUBLANES * S5_PITCH, LANES), F32)
    return pl.pallas_call(
        _s5_scan_kernel,
        out_shape=(out_sds, out_sds),
        grid=(ncc + nlc,),
        in_specs=u_specs + [_const_spec(bmat.shape), _const_spec(cmat.shape), _const_spec(lam.shape)],
        out_specs=(pl.BlockSpec((n_batch, steps, width), of_map),
                   pl.BlockSpec((n_batch, steps, width), or_map)),
        scratch_shapes=[pltpu.VMEM((nblk, SUBLANES, sw2), F32),
                        stage, pltpu.VMEM((2, rows, 2 * LANES), F32),
                        pltpu.VMEM((nblk, rows, sw2), F32), pltpu.VMEM((2, rows, sw2), F32),
                        pltpu.VMEM((2, rows, LANES), F32), stage, stage],
        compiler_params=_cparams(("arbitrary",)),
        name="s5_scan",
    )(*([pf] * (2 * n_batch)), bmat, cmat, lam)


def _s5_post_kernel(pf_ref, yf_ref, yr_ref, d_ref, w_ref, b_ref, o_ref, *, width):
    u = pf_ref[:, 0:width]
    y = d_ref[...] * u + yf_ref[0] + yr_ref[0]
    k0 = math.sqrt(2.0 / math.pi)
    g = 0.5 * y * (1.0 + jnp.tanh(k0 * (y + 0.044715 * (y * y * y))))
    z = _dot(g.astype(BF16), w_ref[...]) + b_ref[...]
    o_ref[...] = (g * jax.nn.sigmoid(z)).astype(BF16)


def _s5_post(pf, yf, yr, d_skip, glu_w, glu_b, *, n_batch, seq, ctx):
    n_rows = pf.shape[0]
    width = yf.shape[2]
    tp = ctx
    nlt = seq // tp
    kern = functools.partial(_s5_post_kernel, width=width)
    tok_map = lambda b, j: (jnp.where(j < nlt, b * nlt + j, n_batch * nlt + b), 0)
    return pl.pallas_call(
        kern,
        out_shape=jax.ShapeDtypeStruct((n_rows, width), BF16),
        grid=(n_batch, nlt + 1),
        in_specs=[pl.BlockSpec((tp, width), tok_map),
                  pl.BlockSpec((1, tp, width), lambda b, j: (b, j, 0)),
                  pl.BlockSpec((1, tp, width), lambda b, j: (b, j, 0)),
                  _const_spec((1, width)), _const_spec(glu_w.shape), _const_spec((1, width))],
        out_specs=pl.BlockSpec((tp, width), tok_map),
        compiler_params=_cparams(("arbitrary", "arbitrary")),
        name="s5_post",
    )(pf, yf, yr, d_skip, glu_w, glu_b)


def _merge_kernel(x_ref, mod_ref, nw_ref, y0_ref, y1_ref, y2_ref, y3_ref, wg_ref, wb_ref, wo_ref, o_ref, acc_ref):
    x = x_ref[...]
    d = x.shape[1]
    m = mod_ref[0]
    n = _rms_mod(x, nw_ref[...], m[3:4], m[4:5]).astype(BF16)
    ys = (y0_ref, y1_ref, y2_ref, y3_ref)
    for b in range(N_BRANCH):
        gate = jax.nn.sigmoid(_dot(n, wg_ref[:, b * d:(b + 1) * d]))
        contrib = gate * _dot(ys[b][...], wb_ref[b])
        if b == 0:
            acc_ref[...] = contrib
        else:
            acc_ref[...] += contrib
    o_ref[...] = x + m[5:6] * _dot(acc_ref[...].astype(BF16), wo_ref[...])


def _merge(h, n_rows, mod, norm_w, ys, wg, wb, wo, *, n_lat_tiles, tiles_per_batch, n_batch):
    d = h.shape[1]
    bw = wb.shape[1]
    return pl.pallas_call(
        _merge_kernel,
        out_shape=jax.ShapeDtypeStruct((n_rows, d), F32),
        grid=(n_rows // TM,),
        in_specs=[pl.BlockSpec((TM, d), lambda i: (i, 0)),
                  pl.BlockSpec((1, N_MOD, d), lambda i: (_mod_row_map(n_lat_tiles, tiles_per_batch, n_batch)(i), 0, 0)),
                  _const_spec((1, d))]
                 + [pl.BlockSpec((TM, bw), lambda i: (i, 0))] * N_BRANCH
                 + [_const_spec(wg.shape), _const_spec(wb.shape), _const_spec(wo.shape)],
        out_specs=pl.BlockSpec((TM, d), lambda i: (i, 0)),
        scratch_shapes=[pltpu.VMEM((TM, d), F32)],
        compiler_params=_cparams(("arbitrary",)),
        name="merge",
    )(h, mod, norm_w, *ys, wg, wb, wo)


def _swap_rot_pairs(w, nf):
    lead = w.shape[:-1]
    n = w.shape[-1]
    return w.reshape(lead + (n // (2 * nf), 2, nf))[..., ::-1, :].reshape(lead + (n,))


def _rope_tables(seq, dim, lane_off, width, period):
    nf = dim // 4
    pos = jnp.arange(seq)
    rows = (pos // GRID_W).astype(F32)
    cols = (pos % GRID_W).astype(F32)
    inv_freq = ROPE_BASE ** (-jnp.arange(nf, dtype=F32) / nf)
    ang_r = rows[:, None] * inv_freq[None, :]
    ang_c = cols[:, None] * inv_freq[None, :]
    cos = jnp.concatenate([jnp.cos(ang_r)] * 2 + [jnp.cos(ang_c)] * 2, axis=1)
    sin = jnp.concatenate([-jnp.sin(ang_r), jnp.sin(ang_r), -jnp.sin(ang_c), jnp.sin(ang_c)], axis=1)
    c_per = jnp.ones((seq, period), F32).at[:, lane_off:lane_off + dim].set(cos)
    s_per = jnp.zeros((seq, period), F32).at[:, lane_off:lane_off + dim].set(sin)
    reps = width // period
    c_tab = jnp.concatenate([jnp.tile(c_per, (1, reps)), jnp.ones((TM, width), F32)], axis=0)
    s_tab = jnp.concatenate([jnp.tile(s_per, (1, reps)), jnp.zeros((TM, width), F32)], axis=0)
    return c_tab, s_tab


def _inproj_weights(w_in):
    d = w_in.shape[0]
    hw = NA_HEADS * HEAD_DIM
    o_sq = 3 * hw
    o_sk = o_sq + SWA_HEADS * HEAD_DIM
    o_sv = o_sk + SWA_KV_HEADS * HEAD_DIM
    o_s5 = o_sv + SWA_KV_HEADS * HEAD_DIM
    s5w = 512
    o_cq = o_s5 + s5w
    o_ckv = o_cq + 256
    o_kr = o_ckv + 128
    o_g = o_kr + MLA_ROPE
    qscale = LOG2E * HEAD_DIM ** -0.5
    na = jnp.concatenate([w_in[:, :hw] * qscale, w_in[:, hw:3 * hw]], axis=1)
    sq = w_in[:, o_sq:o_sk] * qscale
    sk = w_in[:, o_sk:o_sv]
    sv = w_in[:, o_sv:o_s5]
    def dup_heads(a):
        return jnp.concatenate([a[:, kv * HEAD_DIM:(kv + 1) * HEAD_DIM]
                                for kv in range(SWA_KV_HEADS) for _ in range(2)], axis=1)

    sk_dup = dup_heads(sk)
    sv_dup = dup_heads(sv)
    wr = jnp.concatenate([sq, sk_dup], axis=1)
    kr = w_in[:, o_kr:o_g]
    lpad = jnp.zeros((d, MLA_NOPE), F32)
    rpad = jnp.zeros((d, LANES - MLA_NOPE - MLA_ROPE), F32)
    wf = jnp.concatenate([w_in[:, o_s5:o_kr], lpad, kr, rpad], axis=1)
    wg = w_in[:, o_g:]
    proj = tuple(a.astype(BF16) for a in (na, wr, sv_dup, wf))
    return proj, wg.astype(BF16)


def _mla_weights(w_uq, w_ukv):
    ql = w_uq.shape[0]
    kvl = w_ukv.shape[0]
    dq = MLA_NOPE + MLA_ROPE
    wq3 = w_uq.reshape(ql, MLA_HEADS, dq)
    pad = jnp.zeros((ql, MLA_HEADS, LANES - dq), F32)
    qscale = math.log2(math.e) * dq ** -0.5
    wq = jnp.concatenate([wq3 * qscale, pad], axis=2).reshape(ql, MLA_HEADS * LANES)
    wkv3 = w_ukv.reshape(kvl, MLA_HEADS, MLA_NOPE + MLA_V)
    wk = jnp.concatenate([wkv3[:, :, :MLA_NOPE], jnp.zeros((kvl, MLA_HEADS, LANES - MLA_NOPE), F32)], axis=2)
    wk = wk.reshape(kvl, MLA_HEADS * LANES)
    wv = jnp.concatenate([wkv3[:, :, MLA_NOPE:], jnp.zeros((kvl, MLA_HEADS, LANES - MLA_V), F32)], axis=2)
    wv = wv.reshape(kvl, MLA_HEADS * LANES)
    return tuple(a.astype(BF16) for a in (wq, wk, wv))


def _s5_params(lam_re, lam_im, log_dt, b_re, b_im, c_re, c_im):
    a = lam_re.astype(F32)
    w = lam_im.astype(F32)
    dt = jnp.exp(log_dt.astype(F32))[..., None]
    mag = jnp.exp(a * dt)
    lb_re = mag * jnp.cos(w * dt)
    lb_im = mag * jnp.sin(w * dt)
    den = a * a + w * w
    cf_re = ((lb_re - 1.0) * a + lb_im * w) / den
    cf_im = (lb_im * a - (lb_re - 1.0) * w) / den
    bb_re = cf_re[..., None] * b_re - cf_im[..., None] * b_im
    bb_im = cf_re[..., None] * b_im + cf_im[..., None] * b_re
    n_dir, g, p, cg = b_re.shape
    gpb = LANES // cg
    nblk = g // gpb
    eye = jnp.eye(gpb, dtype=F32)

    def in_map(x):
        x5 = jnp.swapaxes(x.reshape(n_dir, nblk, gpb, p, cg), 3, 4)
        full = x5[:, :, :, :, None, :] * eye[None, None, :, None, :, None]
        return full.reshape(n_dir, nblk, gpb * cg, gpb * p)

    def out_map(x):
        x5 = jnp.swapaxes(x.reshape(n_dir, nblk, gpb, cg, p), 3, 4)
        full = x5[:, :, :, :, None, :] * eye[None, None, :, None, :, None]
        return full.reshape(n_dir, nblk, gpb * p, gpb * cg)

    b_in = jnp.concatenate([in_map(bb_re), in_map(bb_im)], axis=3)
    bmat = jnp.concatenate([b_in[0], b_in[1]], axis=1).astype(BF16)
    c_out = jnp.concatenate([out_map(c_re.astype(F32)), out_map(-c_im.astype(F32))], axis=2)
    cmat = jnp.concatenate([c_out[0], c_out[1]], axis=2).astype(BF16)
    half = SUBLANES // 2
    lam2 = jnp.concatenate([lb_re.reshape(n_dir, nblk, gpb * p), lb_im.reshape(n_dir, nblk, gpb * p)], axis=2)
    lam = jnp.concatenate([jnp.broadcast_to(lam2[0][:, None, :], (nblk, half, 2 * gpb * p)),
                           jnp.broadcast_to(lam2[1][:, None, :], (nblk, half, 2 * gpb * p))], axis=1)
    return bmat, cmat, lam


def _ffn_weights(wg, wu, wd):
    d, ff = wg.shape
    wg3 = wg.astype(BF16)
    wu3 = wu.astype(BF16)
    wd3 = wd.astype(BF16)
    return wg3, wu3, wd3


def kernel(x, c, ctx, c_ctx, ada_w, ada_b, ffn1_norm, ffn1_w_gate, ffn1_w_up, ffn1_w_down, mix_norm, w_in, na_rpb, swa_sink, s5_lambda_re, s5_lambda_im, s5_log_dt, s5_b_re, s5_b_im, s5_c_re, s5_c_im, s5_d, s5_glu_w, s5_glu_b, mla_q_norm, mla_w_uq, mla_kv_norm, mla_w_ukv, w_branch, w_out, ffn2_norm, ffn2_w_gate, ffn2_w_up, ffn2_w_down, final_norm):
    n_batch, seq, d = x.shape
    n_ctx = ctx.shape[1]
    depth = ada_w.shape[0]
    assert 2 * n_batch == SUBLANES and seq % TM == 0 and (n_batch * n_ctx) % TM == 0
    n_lat = n_batch * seq
    n_all = n_lat + n_batch * n_ctx
    tiles_per_batch = seq // TM
    n_lat_tiles = n_lat // TM
    geo = dict(n_lat_tiles=n_lat_tiles, tiles_per_batch=tiles_per_batch, n_batch=n_batch)

    h = x.reshape(n_lat, d)
    h_ctx = ctx.reshape(n_batch * n_ctx, d)
    cc = jnp.concatenate([c, c_ctx[None, :], jnp.zeros((SUBLANES - n_batch - 1, d), F32)], axis=0)
    mod = _ada_mod(cc, ada_w, ada_b)

    cs_sw, sn_sw = _rope_tables(seq, HEAD_DIM, 0, 2 * LANES, HEAD_DIM)
    ck_kr, sk_kr = _rope_tables(seq, MLA_ROPE, LANES + MLA_NOPE, 2 * LANES, 2 * LANES)
    cq_ml, sq_ml = _rope_tables(seq, MLA_ROPE, MLA_NOPE, 2 * LANES, LANES)
    rows_n = seq // GRID_W
    s5w = s5_d.shape[1]
    offs = (s5w, s5w + mla_w_uq.shape[1], s5w + mla_w_uq.shape[1] + mla_w_ukv.shape[1])
    sw_col0 = 3 * NA_HEADS * HEAD_DIM

    for l in range(depth):
        need_ctx = l < depth - 1
        last = l == depth - 1
        ml = mod[l]
        ones = jnp.ones((1, d), F32)
        wg3, wu3, wd3 = _ffn_weights(ffn1_w_gate[l], ffn1_w_up[l], ffn1_w_down[l])
        h = _ffn(h, n_all, ml, ffn1_norm[l][None, :], wg3, wu3, wd3, ones, base=0, final=False,
                 h_tail=h_ctx if l == 0 else None, **geo)
        proj_w, gate_w = _inproj_weights(w_in[l])
        pb, pf = _inproj(h, ml, mix_norm[l][None, :], proj_w, (cs_sw, sn_sw, ck_kr, sk_kr), **geo)
        bias = _na_bias_table(na_rpb[l].astype(F32), rows_n)
        y_na = _na_attention(pb, bias, n_batch=n_batch, seq=seq, ctx=n_ctx, need_ctx=need_ctx)
        y_sw_l, y_sw_c = _swa_attention(pb, swa_sink[l].astype(F32) * LOG2E, n_batch=n_batch, seq=seq, ctx=n_ctx,
                                        need_ctx=need_ctx, col0=sw_col0)
        y_sw = jnp.concatenate([y_sw_l, y_sw_c], axis=0) if need_ctx else y_sw_l
        wq, wk, wv = _mla_weights(mla_w_uq[l], mla_w_ukv[l])
        qm, km, vm = _mla_prep(pf, mla_q_norm[l][None, :], mla_kv_norm[l][None, :], wq, wk, wv, cq_ml, sq_ml,
                               n_lat_tiles=n_lat_tiles, tiles_per_batch=tiles_per_batch, offs=offs)
        y_mla = _mla_attention(qm, km, vm, n_batch=n_batch, seq=seq, ctx=n_ctx, need_ctx=need_ctx)
        s5p = _s5_params(s5_lambda_re[l], s5_lambda_im[l], s5_log_dt[l], s5_b_re[l], s5_b_im[l],
                         s5_c_re[l], s5_c_im[l])
        yf, yr = _s5_scan(pf, *s5p, n_batch=n_batch, seq=seq, ctx=n_ctx, width=s5w)
        y_s5 = _s5_post(pf, yf, yr, s5_d[l][None, :].astype(F32), s5_glu_w[l].astype(BF16),
                        s5_glu_b[l][None, :].astype(F32), n_batch=n_batch, seq=seq, ctx=n_ctx)
        n_rows = n_all if need_ctx else n_lat
        h = _merge(h, n_rows, ml, mix_norm[l][None, :], (y_na, y_sw, y_s5, y_mla), gate_w,
                   w_branch[l].astype(BF16), w_out[l].astype(BF16), **geo)
        wg3, wu3, wd3 = _ffn_weights(ffn2_w_gate[l], ffn2_w_up[l], ffn2_w_down[l])
        h = _ffn(h, n_rows, ml, ffn2_norm[l][None, :], wg3, wu3, wd3, final_norm[None, :], base=6, final=last, **geo)
    return h.reshape(n_batch, seq, d)
```

```python
import functools
import math

import numpy as np
import jax
import jax.numpy as jnp
from jax import lax
from jax.experimental import pallas as pl
from jax.experimental.pallas import tpu as pltpu

F32 = jnp.float32
BF16 = jnp.bfloat16

GRID_W = 64
HEAD_DIM = 64
N_BRANCH = 4
NA_HEADS = 8
NA_WIN_ROWS = 8
NA_WIN_COLS = 16
SWA_HEADS = 8
SWA_KV_HEADS = 2
SWA_WINDOW = 128
S5_GROUP = 16
S5_STATE = 64
MLA_HEADS = 8
MLA_NOPE = 64
MLA_ROPE = 32
MLA_V = 64
MACARON_WEIGHT = 0.5
ROPE_BASE = 10000.0
EPS = 1e-6
N_MOD = 9

LANES = 128
SUBLANES = 8
TM = 512
FF_CHUNK = 256
NA_QROWS = 4
NA_KROWS = NA_QROWS + NA_WIN_ROWS - 1
SWA_BLK = 128
SWA_UNROLL = 2
MLA_TQ = 512
MLA_SUB = 256
MLA_AHEAD = 1
S5_STEPS = 128
S5_PITCH = S5_STEPS + SUBLANES
NEG = -1e30
LOG2E = math.log2(math.e)
VMEM_LIMIT = 56 * 1024 * 1024


def _cparams(sem, flags=None):
    return pltpu.CompilerParams(dimension_semantics=sem, vmem_limit_bytes=VMEM_LIMIT, flags=flags)


def _const_spec(shape):
    nd = len(shape)
    return pl.BlockSpec(shape, lambda *_: (0,) * nd, pipeline_mode=pl.Buffered(1))


def _dot(a, b):
    return jnp.dot(a, b, preferred_element_type=F32)


def _dot_t(a, b):
    return lax.dot_general(a, b, (((1,), (1,)), ((), ())), preferred_element_type=F32)


def _rms(x, w):
    return x * lax.rsqrt(jnp.mean(x * x, axis=-1, keepdims=True) + EPS) * w


def _rms_mod(x, w, shift, scale):
    return _rms(x, w) * (1.0 + scale) + shift


def _mod_row_map(n_lat_tiles, tiles_per_batch, n_batch):
    def f(i):
        return jnp.where(i < n_lat_tiles, i // tiles_per_batch, n_batch)
    return f


def _ada_kernel(c_ref, w_ref, b_ref, o_ref):
    c = c_ref[...]
    s = c * jax.nn.sigmoid(c)
    w = w_ref[0]
    s_hi = s.astype(BF16)
    s_lo = (s - s_hi.astype(F32)).astype(BF16)
    w_hi = w.astype(BF16)
    w_lo = (w - w_hi.astype(F32)).astype(BF16)
    o_ref[0] = _dot(s_hi, w_hi) + (_dot(s_hi, w_lo) + _dot(s_lo, w_hi)) + b_ref[0]


def _ada_mod(cc, ada_w, ada_b):
    depth, d, nd = ada_w.shape
    tn = 1024
    out = pl.pallas_call(
        _ada_kernel,
        out_shape=jax.ShapeDtypeStruct((depth, SUBLANES, nd), F32),
        grid=(depth, nd // tn),
        in_specs=[pl.BlockSpec((SUBLANES, d), lambda l, j: (0, 0)),
                  pl.BlockSpec((1, d, tn), lambda l, j: (l, 0, j)),
                  pl.BlockSpec((1, 1, tn), lambda l, j: (l, 0, j))],
        out_specs=pl.BlockSpec((1, SUBLANES, tn), lambda l, j: (l, 0, j)),
        compiler_params=_cparams(("arbitrary", "arbitrary")),
        name="ada_mod",
    )(cc, ada_w, ada_b.reshape(depth, 1, nd))
    return out.reshape(depth, SUBLANES, N_MOD, d)


def _ffn_kernel(x_ref, xc_ref, mod_ref, nw_ref, wg_ref, wu_ref, wd_ref, fw_ref, *rest, base, final, n_first):
    if len(rest) == 1:
        (o_ref,) = rest
    else:
        ng_ref, nu_ref, nd_ref, o_ref, cg_ref, cu_ref, cd_ref = rest
        cg_ref[...] = ng_ref[...].astype(BF16)
        cu_ref[...] = nu_ref[...].astype(BF16)
        cd_ref[...] = nd_ref[...].astype(BF16)
    x = x_ref[...]
    if n_first is not None:
        x = jnp.where(pl.program_id(0) < n_first, x, xc_ref[...])
    m = mod_ref[0]
    n = _rms_mod(x, nw_ref[...], m[base:base + 1], m[base + 1:base + 2]).astype(BF16)
    ff = wg_ref.shape[1]
    fc = FF_CHUNK if ff % FF_CHUNK == 0 else ff
    acc = None
    for c in range(ff // fc):
        cs = slice(c * fc, (c + 1) * fc)
        g = _dot(n, wg_ref[:, cs])
        u = _dot(n, wu_ref[:, cs])
        a = (g * jax.nn.sigmoid(g) * u).astype(BF16)
        y = _dot(a, wd_ref[cs, :])
        acc = y if acc is None else acc + y
    out = x + MACARON_WEIGHT * m[base + 2:base + 3] * acc
    if final:
        out = _rms(out, fw_ref[...])
    o_ref[...] = out


def _ffn(h, n_rows, mod, norm_w, wg, wu, wd, final_w, *, base, final, n_lat_tiles, tiles_per_batch, n_batch,
         h_tail=None, cast_next=None):
    d = h.shape[1]
    ff = wg.shape[1]
    steps = n_rows // TM
    if h_tail is None:
        n_first = None
        h_tail = h
        x_map = lambda i: (i, 0)
        t_map = lambda i: (0, 0)
    else:
        n_first = h.shape[0] // TM
        x_map = lambda i: (jnp.minimum(i, n_first - 1), 0)
        t_map = lambda i: (jnp.maximum(i - n_first, 0), 0)
    kern = functools.partial(_ffn_kernel, base=base, final=final, n_first=n_first)
    in_specs = [pl.BlockSpec((TM, d), x_map),
                pl.BlockSpec((TM, d) if n_first is not None else (SUBLANES, d), t_map),
                pl.BlockSpec((1, N_MOD, d), lambda i: (_mod_row_map(n_lat_tiles, tiles_per_batch, n_batch)(i), 0, 0)),
                _const_spec((1, d)),
                _const_spec((d, ff)), _const_spec((d, ff)), _const_spec((ff, d)),
                _const_spec((1, d))]
    out_shape = jax.ShapeDtypeStruct((n_rows, d), F32)
    out_specs = pl.BlockSpec((TM, d), lambda i: (i, 0))
    operands = [h, h_tail, mod, norm_w, wg, wu, wd, final_w]
    if cast_next is not None:
        up_rows = min(r for r in range(2 * SUBLANES, d + 1, 2 * SUBLANES) if d % r == 0 and d // r <= steps)
        dn_rows = min(r for r in range(2 * SUBLANES, ff + 1, 2 * SUBLANES) if ff % r == 0 and ff // r <= steps)
        n_up, n_dn = d // up_rows, ff // dn_rows
        up_spec = pl.BlockSpec((up_rows, ff), lambda i: (jnp.minimum(i, n_up - 1), 0))
        dn_spec = pl.BlockSpec((dn_rows, d), lambda i: (jnp.minimum(i, n_dn - 1), 0))
        in_specs += [up_spec, up_spec, dn_spec]
        operands += list(cast_next)
        out_shape = (out_shape, jax.ShapeDtypeStruct((d, ff), BF16), jax.ShapeDtypeStruct((d, ff), BF16),
                     jax.ShapeDtypeStruct((ff, d), BF16))
        out_specs = (out_specs, up_spec, up_spec, dn_spec)
    return pl.pallas_call(
        kern,
        out_shape=out_shape,
        grid=(steps,),
        in_specs=in_specs,
        out_specs=out_specs,
        compiler_params=_cparams(("arbitrary",)),
        name="ffn_final" if final else "ffn",
    )(*operands)


def _rope_apply(y, cs, sn, nf):
    w = y.shape[1]
    first = (lax.broadcasted_iota(jnp.int32, y.shape, 1) & nf) == 0
    ysw = jnp.where(first, pltpu.roll(y, w - nf, 1), pltpu.roll(y, nf, 1))
    return y * cs + ysw * sn


def _inproj_kernel(x_ref, mod_ref, nw_ref, wa_ref, wr_ref, wv_ref, wf_ref,
                   cs_ref, sn_ref, ck_ref, sk_ref, ob_ref, of_ref):
    x = x_ref[...]
    m = mod_ref[0]
    n = _rms_mod(x, nw_ref[...], m[3:4], m[4:5]).astype(BF16)
    na = wa_ref.shape[1]
    nr = wr_ref.shape[1]
    nv = wv_ref.shape[1]
    nf = wf_ref.shape[1]
    cw = 2 * LANES
    for c in range(na // cw):
        ob_ref[:, c * cw:(c + 1) * cw] = _dot(n, wa_ref[:, c * cw:(c + 1) * cw]).astype(BF16)
    for c in range(nr // cw):
        y = _dot(n, wr_ref[:, c * cw:(c + 1) * cw])
        ob_ref[:, na + c * cw:na + (c + 1) * cw] = _rope_apply(y, cs_ref[...], sn_ref[...], HEAD_DIM // 4).astype(BF16)
    for c in range(nv // cw):
        ob_ref[:, na + nr + c * cw:na + nr + (c + 1) * cw] = _dot(n, wv_ref[:, c * cw:(c + 1) * cw]).astype(BF16)
    for c in range(nf // cw - 1):
        of_ref[:, c * cw:(c + 1) * cw] = _dot(n, wf_ref[:, c * cw:(c + 1) * cw])
    y = _dot(n, wf_ref[:, nf - cw:nf])
    of_ref[:, nf - cw:nf] = _rope_apply(y, ck_ref[...], sk_ref[...], MLA_ROPE // 4)


def _inproj(h, mod, norm_w, w, tabs, *, n_lat_tiles, tiles_per_batch, n_batch):
    n_rows, d = h.shape
    wa, wr, wv, wf = w
    cs, sn, ck, sk = tabs
    nb = wa.shape[1] + wr.shape[1] + wv.shape[1]
    nf = wf.shape[1]
    tab_map = lambda i: (jnp.where(i < n_lat_tiles, i % tiles_per_batch, tiles_per_batch), 0)
    return pl.pallas_call(
        _inproj_kernel,
        out_shape=(jax.ShapeDtypeStruct((n_rows, nb), BF16), jax.ShapeDtypeStruct((n_rows, nf), F32)),
        grid=(n_rows // TM,),
        in_specs=[pl.BlockSpec((TM, d), lambda i: (i, 0)),
                  pl.BlockSpec((1, N_MOD, d), lambda i: (_mod_row_map(n_lat_tiles, tiles_per_batch, n_batch)(i), 0, 0)),
                  _const_spec((1, d))]
                 + [_const_spec(a.shape) for a in w]
                 + [pl.BlockSpec((TM, 2 * LANES), tab_map)] * 4,
        out_specs=(pl.BlockSpec((TM, nb), lambda i: (i, 0)), pl.BlockSpec((TM, nf), lambda i: (i, 0))),
        compiler_params=_cparams(("arbitrary",)),
        name="inproj",
    )(h, mod, norm_w, *w, cs, sn, ck, sk)


def _lane_half_masks(rows):
    lane = lax.broadcasted_iota(jnp.int32, (rows, LANES), 1)
    return lane < HEAD_DIM


def _with_ones(v):
    return jnp.concatenate([v, jnp.ones_like(v)], axis=1)


def _na_kernel(q_ref, k_ref, v_ref, kc_ref, vc_ref, bias_ref, o_ref, *, n_blk, rows_n):
    blk = pl.program_id(1)
    nq = NA_QROWS * GRID_W
    nk = NA_KROWS * GRID_W
    ws = jnp.clip(NA_QROWS * blk - NA_WIN_ROWS // 2, 0, rows_n - NA_KROWS)
    r0 = pl.multiple_of(ws * GRID_W, GRID_W)
    lo = _lane_half_masks(nq)
    for j in range(NA_HEADS // 2):
        sl = slice(j * LANES, (j + 1) * LANES)
        qp = q_ref[:, sl]
        kw = k_ref[pl.ds(r0, nk), sl]
        vw = v_ref[pl.ds(r0, nk), sl]
        kc = kc_ref[:, sl]
        vc = vc_ref[:, sl]
        zero = jnp.zeros_like(qp)
        qs = jnp.concatenate([jnp.where(lo, qp, zero), jnp.where(lo, zero, qp)], axis=0)
        s1 = _dot_t(qs, kw) + bias_ref[0, j]
        s2 = _dot_t(qs, kc)
        m = jnp.maximum(jnp.max(s1, axis=-1, keepdims=True), jnp.max(s2, axis=-1, keepdims=True))
        o2 = (_dot(jnp.exp2(s1 - m).astype(BF16), _with_ones(vw))
              + _dot(jnp.exp2(s2 - m).astype(BF16), _with_ones(vc)))
        o = o2[:, :LANES] / o2[:, LANES:]
        o_ref[:, sl] = jnp.where(lo, o[:nq], o[nq:]).astype(BF16)


def _na_attention(pb, bias, *, n_batch, seq, ctx, need_ctx):
    rows_n = seq // GRID_W
    n_blk = rows_n // NA_QROWS
    nq = NA_QROWS * GRID_W
    hw = NA_HEADS * HEAD_DIM
    n_q = n_blk + (1 if need_ctx else 0)
    assert ctx == nq
    lat_blocks = n_batch * seq // nq
    qmap = lambda b, i: (jnp.where(i < n_blk, b * n_blk + i, lat_blocks + b), 0)

    def bias_map(b, i):
        t = jnp.where(i == 0, 0, jnp.where(i == 1, 1, jnp.where(i == n_blk - 1, 3, jnp.where(i == n_blk, 4, 2))))
        return (t, 0, 0, 0)

    kern = functools.partial(_na_kernel, n_blk=n_blk, rows_n=rows_n)
    return pl.pallas_call(
        kern,
        out_shape=jax.ShapeDtypeStruct((n_batch * n_q * nq, hw), BF16),
        grid=(n_batch, n_q),
        in_specs=[pl.BlockSpec((nq, hw), qmap),
                  pl.BlockSpec((seq, hw), lambda b, i: (b, 1)),
                  pl.BlockSpec((seq, hw), lambda b, i: (b, 2)),
                  pl.BlockSpec((ctx, hw), lambda b, i: (n_batch * seq // ctx + b, 1)),
                  pl.BlockSpec((ctx, hw), lambda b, i: (n_batch * seq // ctx + b, 2)),
                  pl.BlockSpec((1, NA_HEADS // 2, 2 * nq, NA_KROWS * GRID_W), bias_map)],
        out_specs=pl.BlockSpec((nq, hw), qmap),
        compiler_params=_cparams(("arbitrary", "arbitrary")),
        name="na_attn",
    )(pb, pb, pb, pb, pb, bias)


def _na_bias_table(rpb, rows_n):
    n_blk = rows_n // NA_QROWS
    kr_n = min(NA_WIN_ROWS, rows_n)
    n_heads = rpb.shape[0]
    col = np.arange(GRID_W)
    c0 = np.clip(col - NA_WIN_COLS // 2, 0, GRID_W - NA_WIN_COLS)
    col_ok = (col[None, :] >= c0[:, None]) & (col[None, :] < c0[:, None] + NA_WIN_COLS)
    padded = jnp.pad(rpb * LOG2E, ((0, 0), (0, 0), (GRID_W, GRID_W)))
    off = GRID_W + NA_WIN_COLS - 1
    toep = jnp.stack([padded[:, :, off - qc:off - qc + GRID_W] for qc in range(GRID_W)], axis=2)
    toep = jnp.where(col_ok[None, None], toep, NEG)
    masked = jnp.full((n_heads, 1, GRID_W, GRID_W), NEG, F32)
    ext = jnp.concatenate([masked, toep, masked], axis=1)
    pair2 = jnp.concatenate([ext[:, :-1], ext[:, 1:]], axis=-1)
    n_dr = 2 * NA_WIN_ROWS - 1
    plans = []
    for blk in (0, 1, 2, n_blk - 1):
        ws = int(np.clip(NA_QROWS * blk - NA_WIN_ROWS // 2, 0, rows_n - NA_KROWS))
        plan = []
        for qr in range(NA_QROWS):
            r = NA_QROWS * blk + qr
            r0 = int(np.clip(r - kr_n // 2, 0, rows_n - kr_n))
            row = []
            for kk in range(0, NA_KROWS, 2):
                oks = tuple(r0 <= ws + kk + i < r0 + kr_n and kk + i < NA_KROWS for i in range(2))
                row.append((ws + kk - r + NA_WIN_ROWS, oks))
            plan.append(tuple(row))
        plans.append(tuple(plan))
    plans.append(None)
    nq = NA_QROWS * GRID_W
    nk = NA_KROWS * GRID_W
    kern = functools.partial(_na_bias_kernel, plans=tuple(plans))
    return pl.pallas_call(
        kern,
        out_shape=jax.ShapeDtypeStruct((len(plans), n_heads // 2, 2 * nq, nk), F32),
        grid=(n_heads // 2,),
        in_specs=[pl.BlockSpec((2, n_dr + 1, GRID_W, 2 * GRID_W), lambda j: (j, 0, 0, 0))],
        out_specs=pl.BlockSpec((len(plans), 1, 2 * nq, nk), lambda j: (0, j, 0, 0)),
        compiler_params=_cparams(("arbitrary",)),
        name="na_bias",
    )(pair2)


def _na_bias_kernel(p2_ref, o_ref, *, plans):
    nq = NA_QROWS * GRID_W
    lo = lax.broadcasted_iota(jnp.int32, (GRID_W, 2 * GRID_W), 1) < GRID_W
    neg = jnp.full((GRID_W, 2 * GRID_W), NEG, F32)
    for t, plan in enumerate(plans):
        if plan is None:
            o_ref[t, 0] = jnp.full(o_ref.shape[2:], NEG, F32)
            continue
        for half in range(2):
            for qr, row in enumerate(plan):
                rs = slice(half * nq + qr * GRID_W, half * nq + (qr + 1) * GRID_W)
                for kp, (e, (ok_a, ok_b)) in enumerate(row):
                    width = min(2 * GRID_W, o_ref.shape[3] - kp * 2 * GRID_W)
                    if ok_a or ok_b:
                        tile = p2_ref[half, e]
                        if not ok_a:
                            tile = jnp.where(lo, neg, tile)
                        if not ok_b:
                            tile = jnp.where(lo, tile, neg)
                    else:
                        tile = neg
                    o_ref[t, 0, rs, kp * 2 * GRID_W:kp * 2 * GRID_W + width] = tile[:, :width]


def _swa_kernel(sink_ref, q_ref, k_ref, v_ref, qc_ref, kc_ref, vc_ref, o_ref, oc_ref, *, seq, need_ctx):
    n_blk = seq // SWA_BLK
    band = 3 * SWA_BLK
    group = SWA_HEADS // SWA_KV_HEADS
    pairs = group // 2

    def group_attn(q_slabs, kv, kb, vb, kc, vc, mask_bias):
        r = q_slabs[0].shape[0]
        lo = _lane_half_masks(r)
        parts = []
        for qp in q_slabs:
            zero = jnp.zeros_like(qp)
            parts += [jnp.where(lo, qp, zero), jnp.where(lo, zero, qp)]
        qs = jnp.concatenate(parts, axis=0)
        row = lax.broadcasted_iota(jnp.int32, (group * r, 1), 0)
        sink = jnp.full((group * r, 1), sink_ref[kv * group + group - 1], F32)
        for g in range(group - 2, -1, -1):
            sink = jnp.where(row < (g + 1) * r, sink_ref[kv * group + g], sink)
        s2 = _dot_t(qs, kc)
        if kb is None:
            m = jnp.maximum(jnp.max(s2, axis=-1, keepdims=True), sink)
            o2 = _dot(jnp.exp2(s2 - m).astype(BF16), _with_ones(vc))
        else:
            s1 = _dot_t(qs, kb) + mask_bias
            m = jnp.maximum(jnp.maximum(jnp.max(s1, axis=-1, keepdims=True), jnp.max(s2, axis=-1, keepdims=True)), sink)
            o2 = (_dot(jnp.exp2(s1 - m).astype(BF16), _with_ones(vb))
                  + _dot(jnp.exp2(s2 - m).astype(BF16), _with_ones(vc)))
        o = o2[:, :LANES] / (o2[:, LANES:] + jnp.exp2(sink - m))
        return [jnp.where(lo, o[(2 * i) * r:(2 * i + 1) * r], o[(2 * i + 1) * r:(2 * i + 2) * r]) for i in range(pairs)]

    qi = lax.broadcasted_iota(jnp.int32, (group * SWA_BLK, band), 0) & (SWA_BLK - 1)
    ki = lax.broadcasted_iota(jnp.int32, (group * SWA_BLK, band), 1)

    def one_block(n):
        start = jnp.clip((n - 1) * SWA_BLK, 0, seq - band)
        start = pl.multiple_of(start, SWA_BLK)
        q0 = pl.multiple_of(n * SWA_BLK, SWA_BLK)
        delta = (start + ki) - (q0 + qi)
        mask_bias = jnp.where(jnp.abs(delta) <= SWA_WINDOW, 0.0, NEG).astype(F32)
        for kv in range(SWA_KV_HEADS):
            ksl = slice(kv * LANES, (kv + 1) * LANES)
            slabs = [q_ref[pl.ds(q0, SWA_BLK), (kv * pairs + i) * LANES:(kv * pairs + i + 1) * LANES]
                     for i in range(pairs)]
            outs = group_attn(slabs, kv, k_ref[pl.ds(start, band), ksl], v_ref[pl.ds(start, band), ksl],
                              kc_ref[:, ksl], vc_ref[:, ksl], mask_bias)
            for i in range(pairs):
                o_ref[pl.ds(q0, SWA_BLK), (kv * pairs + i) * LANES:(kv * pairs + i + 1) * LANES] = outs[i].astype(BF16)

    def blk_body(n2, carry):
        for j in range(SWA_UNROLL):
            one_block(n2 * SWA_UNROLL + j)
        return carry

    lax.fori_loop(0, n_blk // SWA_UNROLL, blk_body, 0)

    if need_ctx:
        for kv in range(SWA_KV_HEADS):
            ksl = slice(kv * LANES, (kv + 1) * LANES)
            slabs = [qc_ref[:, (kv * pairs + i) * LANES:(kv * pairs + i + 1) * LANES] for i in range(pairs)]
            outs = group_attn(slabs, kv, None, None, kc_ref[:, ksl], vc_ref[:, ksl], None)
            for i in range(pairs):
                oc_ref[:, (kv * pairs + i) * LANES:(kv * pairs + i + 1) * LANES] = outs[i].astype(BF16)
    else:
        oc_ref[...] = jnp.zeros_like(oc_ref)


def _swa_attention(pb, sink, *, n_batch, seq, ctx, need_ctx, col0):
    hw = SWA_HEADS * HEAD_DIM
    kw = SWA_KV_HEADS * LANES
    qcol = col0 // hw
    kcol = (col0 + hw) // kw
    vcol = kcol + 1
    cblk = n_batch * seq // ctx
    kern = functools.partial(_swa_kernel, seq=seq, need_ctx=need_ctx)
    return pl.pallas_call(
        kern,
        out_shape=(jax.ShapeDtypeStruct((n_batch * seq, hw), BF16),
                   jax.ShapeDtypeStruct((n_batch * ctx, hw), BF16)),
        grid=(n_batch,),
        in_specs=[pl.BlockSpec(memory_space=pltpu.SMEM),
                  pl.BlockSpec((seq, hw), lambda b: (b, qcol)),
                  pl.BlockSpec((seq, kw), lambda b: (b, kcol)),
                  pl.BlockSpec((seq, kw), lambda b: (b, vcol)),
                  pl.BlockSpec((ctx, hw), lambda b: (cblk + b, qcol)),
                  pl.BlockSpec((ctx, kw), lambda b: (cblk + b, kcol)),
                  pl.BlockSpec((ctx, kw), lambda b: (cblk + b, vcol))],
        out_specs=(pl.BlockSpec((seq, hw), lambda b: (b, 0)),
                   pl.BlockSpec((ctx, hw), lambda b: (b, 0))),
        compiler_params=_cparams(("arbitrary",)),
        name="swa_attn",
    )(sink, pb, pb, pb, pb, pb, pb)


def _mla_prep_kernel(pf_ref, qn_ref, kn_ref, wq_ref, wqs_ref, wk_ref, wv_ref, cq_ref, sq_ref,
                     q_ref, k_ref, v_ref, *, off_cq, off_ckv, off_kr, q_lora, kv_lora):
    cq = pf_ref[:, off_cq:off_cq + q_lora]
    ckv = pf_ref[:, off_ckv:off_ckv + kv_lora]
    kr = pf_ref[:, off_kr:off_kr + LANES]
    kr2 = jnp.concatenate([kr, kr], axis=1)
    cqn = _rms(cq, qn_ref[...]).astype(BF16)
    ckvn = _rms(ckv, kn_ref[...]).astype(BF16)
    cw = 2 * LANES
    for c in range(MLA_HEADS * LANES // cw):
        sl = slice(c * cw, (c + 1) * cw)
        q = _dot(cqn, wq_ref[:, sl]) * cq_ref[...] + _dot(cqn, wqs_ref[:, sl]) * sq_ref[...]
        q_ref[:, sl] = q.astype(BF16)
        k_ref[:, sl] = (_dot(ckvn, wk_ref[:, sl]) + kr2).astype(BF16)
    ones_hi = jnp.where(_lane_half_masks(1), 0.0, 1.0).astype(F32)
    ones_hi = jnp.concatenate([ones_hi, ones_hi], axis=1)
    for c in range(MLA_HEADS * LANES // cw):
        sl = slice(c * cw, (c + 1) * cw)
        v_ref[:, sl] = (_dot(ckvn, wv_ref[:, sl]) + ones_hi).astype(BF16)


def _mla_prep(pf, qn, kn, wq, wqs, wk, wv, cq_tab, sq_tab, *, n_lat_tiles, tiles_per_batch, offs):
    n_rows = pf.shape[0]
    off_cq, off_ckv, off_kr = offs
    tab_map = lambda i: (jnp.where(i < n_lat_tiles, i % tiles_per_batch, tiles_per_batch), 0)
    kern = functools.partial(_mla_prep_kernel, off_cq=off_cq, off_ckv=off_ckv, off_kr=off_kr,
                             q_lora=wq.shape[0], kv_lora=wk.shape[0])
    hq = MLA_HEADS * LANES
    hv = MLA_HEADS * LANES
    return pl.pallas_call(
        kern,
        out_shape=(jax.ShapeDtypeStruct((n_rows, hq), BF16), jax.ShapeDtypeStruct((n_rows, hq), BF16),
                   jax.ShapeDtypeStruct((n_rows, hv), BF16)),
        grid=(n_rows // TM,),
        in_specs=[pl.BlockSpec((TM, pf.shape[1]), lambda i: (i, 0)),
                  _const_spec(qn.shape), _const_spec(kn.shape),
                  _const_spec(wq.shape), _const_spec(wqs.shape), _const_spec(wk.shape), _const_spec(wv.shape),
                  pl.BlockSpec((TM, 2 * LANES), tab_map), pl.BlockSpec((TM, 2 * LANES), tab_map)],
        out_specs=(pl.BlockSpec((TM, hq), lambda i: (i, 0)), pl.BlockSpec((TM, hq), lambda i: (i, 0)),
                   pl.BlockSpec((TM, hv), lambda i: (i, 0))),
        compiler_params=_cparams(("arbitrary",)),
        name="mla_prep",
    )(pf, qn, kn, wq, wqs, wk, wv, cq_tab, sq_tab)


def _mla_body(q_ref, k_ref, v_ref, kc_ref, vc_ref, o_ref):
    sub = MLA_SUB
    n_sub = q_ref.shape[0] // sub
    lo = _lane_half_masks(sub)
    units = [(r, half) for r in range(n_sub) for half in range(2)]

    def scores(r, half):
        sl = slice(half * LANES, (half + 1) * LANES)
        q = q_ref[r * sub:(r + 1) * sub, sl]
        s1 = None if k_ref is None else _dot_t(q, k_ref[:, sl])
        return s1, _dot_t(q, kc_ref[:, sl])

    def finish(half, s1, s2):
        sl = slice(half * LANES, (half + 1) * LANES)
        if s1 is None:
            m = jnp.max(s2, axis=-1, keepdims=True)
            return _dot(jnp.exp2(s2 - m).astype(BF16), vc_ref[:, sl])
        m = jnp.maximum(jnp.max(s1, axis=-1, keepdims=True), jnp.max(s2, axis=-1, keepdims=True))
        p1 = jnp.exp2(s1 - m)
        p2 = jnp.exp2(s2 - m)
        return _dot(p1.astype(BF16), v_ref[:, sl]) + _dot(p2.astype(BF16), vc_ref[:, sl])

    outs = {}
    ahead = MLA_AHEAD
    sc = {u: scores(*u) for u in units[:ahead]}
    for idx, u in enumerate(units):
        if idx + ahead < len(units):
            sc[units[idx + ahead]] = scores(*units[idx + ahead])
        outs[u] = finish(u[1], *sc.pop(u))
    for r in range(n_sub):
        o0, o1 = outs[(r, 0)], outs[(r, 1)]
        r0 = pltpu.roll(o0, HEAD_DIM, 1)
        r1 = pltpu.roll(o1, HEAD_DIM, 1)
        o_ref[r * sub:(r + 1) * sub, :] = jnp.where(lo, o0 / r0, r1 / o1).astype(BF16)


def _mla_kernel(q_ref, k_ref, v_ref, kc_ref, vc_ref, o_ref):
    _mla_body(q_ref, k_ref, v_ref, kc_ref, vc_ref, o_ref)


def _mla_ctx_kernel(q_ref, kc_ref, vc_ref, o_ref):
    _mla_body(q_ref, None, None, kc_ref, vc_ref, o_ref)


def _mla_attention(qm, km, vm, *, n_batch, seq, ctx, need_ctx):
    n_qt = seq // MLA_TQ
    cblk = n_batch * seq // ctx
    hv = MLA_HEADS * MLA_V
    y_lat = pl.pallas_call(
        _mla_kernel,
        out_shape=jax.ShapeDtypeStruct((n_batch * seq, hv), BF16),
        grid=(n_batch, MLA_HEADS // 2, n_qt),
        in_specs=[pl.BlockSpec((MLA_TQ, 2 * LANES), lambda b, p, i: (b * n_qt + i, p)),
                  pl.BlockSpec((seq, 2 * LANES), lambda b, p, i: (b, p)),
                  pl.BlockSpec((seq, 2 * LANES), lambda b, p, i: (b, p)),
                  pl.BlockSpec((ctx, 2 * LANES), lambda b, p, i: (cblk + b, p)),
                  pl.BlockSpec((ctx, 2 * LANES), lambda b, p, i: (cblk + b, p))],
        out_specs=pl.BlockSpec((MLA_TQ, LANES), lambda b, p, i: (b * n_qt + i, p)),
        compiler_params=_cparams(("arbitrary", "arbitrary", "arbitrary")),
        name="mla_attn",
    )(qm, km, vm, km, vm)
    if not need_ctx:
        return y_lat
    assert ctx % MLA_SUB == 0
    y_ctx = pl.pallas_call(
        _mla_ctx_kernel,
        out_shape=jax.ShapeDtypeStruct((n_batch * ctx, hv), BF16),
        grid=(n_batch, MLA_HEADS // 2),
        in_specs=[pl.BlockSpec((ctx, 2 * LANES), lambda b, p: (cblk + b, p)),
                  pl.BlockSpec((ctx, 2 * LANES), lambda b, p: (cblk + b, p)),
                  pl.BlockSpec((ctx, 2 * LANES), lambda b, p: (cblk + b, p))],
        out_specs=pl.BlockSpec((ctx, LANES), lambda b, p: (b, p)),
        compiler_params=_cparams(("arbitrary", "arbitrary")),
        name="mla_ctx_attn",
    )(qm, km, vm)
    return y_lat, y_ctx


def _s5_scan_kernel(uf0_ref, uf1_ref, uf2_ref, uf3_ref, ur0_ref, ur1_ref, ur2_ref, ur3_ref,
                    bmat_ref, cmat_ref, lam_ref, of_ref, or_ref,
                    h_ref, stage_ref, lhs_ref, bu_ref, st_ref, ysc_ref, ya_ref, yb_ref):
    c = pl.program_id(0)
    n_seq = SUBLANES
    half = n_seq // 2
    steps = S5_STEPS
    pitch = S5_PITCH
    rows = steps * n_seq
    nblk = h_ref.shape[0]
    sw = h_ref.shape[2] // 2
    ufs = (uf0_ref, uf1_ref, uf2_ref, uf3_ref)
    urs = (ur0_ref, ur1_ref, ur2_ref, ur3_ref)
    is_fwd = (lax.broadcasted_iota(jnp.int32, (rows, 1), 0) % n_seq) < half
    sub_fwd = lax.broadcasted_iota(jnp.int32, (n_seq, LANES), 0) < half

    @pl.when(c == 0)
    def _():
        h_ref[...] = jnp.zeros_like(h_ref)

    def project_in(k):
        ks = slice(k * LANES, (k + 1) * LANES)
        z = k % 2
        for b in range(half):
            stage_ref[k, b * pitch:b * pitch + steps, :] = ufs[b][:, ks]
            stage_ref[k, (half + b) * pitch:(half + b) * pitch + steps, :] = urs[b][:, ks]
        for t in range(steps):
            ga = stage_ref[k, pl.ds(t, n_seq, stride=pitch), :]
            gb = stage_ref[k, pl.ds(steps - 1 - t, n_seq, stride=pitch), :]
            lhs_ref[z, t * n_seq:(t + 1) * n_seq, 0:LANES] = jnp.where(sub_fwd, ga, 0.0)
            lhs_ref[z, t * n_seq:(t + 1) * n_seq, LANES:2 * LANES] = jnp.where(sub_fwd, 0.0, gb)
        bu_ref[k] = _dot(lhs_ref[z].astype(BF16), bmat_ref[k])

    def recur(k):
        z = k % 2
        lre = lam_ref[k, :, 0:sw]
        lim = lam_ref[k, :, sw:2 * sw]
        hr = h_ref[k, :, 0:sw]
        hi = h_ref[k, :, sw:2 * sw]
        for t in range(steps):
            rs = slice(t * n_seq, (t + 1) * n_seq)
            hr, hi = (lre * hr - lim * hi + bu_ref[k, rs, 0:sw],
                      lre * hi + lim * hr + bu_ref[k, rs, sw:2 * sw])
            st_ref[z, rs, 0:sw] = hr
            st_ref[z, rs, sw:2 * sw] = hi
        h_ref[k, :, 0:sw] = hr
        h_ref[k, :, sw:2 * sw] = hi

    def read_out(k):
        ks = slice(k * LANES, (k + 1) * LANES)
        z = k % 2
        hr_rows = rows // 2
        for part in range(2):
            rs = slice(part * hr_rows, (part + 1) * hr_rows)
            y2 = _dot(st_ref[z, rs, :].astype(BF16), cmat_ref[k])
            ysc_ref[z, rs, :] = jnp.where(is_fwd[rs], y2[:, 0:LANES], y2[:, LANES:2 * LANES])
        for t in range(steps):
            g = ysc_ref[z, t * n_seq:(t + 1) * n_seq, :]
            ya_ref[k, pl.ds(t, n_seq, stride=pitch), :] = g
            yb_ref[k, pl.ds(steps - 1 - t, n_seq, stride=pitch), :] = g
        for b in range(half):
            of_ref[b, :, ks] = ya_ref[k, b * pitch:b * pitch + steps, :]
            or_ref[b, :, ks] = yb_ref[k, (half + b) * pitch:(half + b) * pitch + steps, :]

    for k in range(nblk):
        project_in(k)
    recur(0)
    for k in range(1, nblk):
        recur(k)
        read_out(k - 1)
    read_out(nblk - 1)


def _s5_scan(pf, bmat, cmat, lam, *, n_batch, seq, ctx, width):
    steps = S5_STEPS
    nblk = width // LANES
    rows = steps * SUBLANES
    sw2 = bmat.shape[-1]
    ncc = ctx // steps
    nlc = seq // steps
    ctx0 = n_batch * nlc

    def fwd_map(b):
        return lambda c: (jnp.where(c < ncc, ctx0 + b * ncc + c, b * nlc + (c - ncc)), 0)

    def rev_map(b):
        return lambda c: (jnp.where(c < ncc, ctx0 + b * ncc + (ncc - 1 - c), b * nlc + (nlc - 1 - (c - ncc))), 0)

    of_map = lambda c: (0, jnp.where(c < ncc, nlc + c, c - ncc), 0)
    or_map = lambda c: (0, jnp.where(c < ncc, nlc + (ncc - 1 - c), nlc - 1 - (c - ncc)), 0)
    u_specs = ([pl.BlockSpec((steps, width), fwd_map(b)) for b in range(n_batch)]
               + [pl.BlockSpec((steps, width), rev_map(b)) for b in range(n_batch)])
    out_sds = jax.ShapeDtypeStruct((n_batch, seq + ctx, width), F32)
    stage = pltpu.VMEM((nblk, SUBLANES * S5_PITCH, LANES), F32)
    return pl.pallas_call(
        _s5_scan_kernel,
        out_shape=(out_sds, out_sds),
        grid=(ncc + nlc,),
        in_specs=u_specs + [_const_spec(bmat.shape), _const_spec(cmat.shape), _const_spec(lam.shape)],
        out_specs=(pl.BlockSpec((n_batch, steps, width), of_map),
                   pl.BlockSpec((n_batch, steps, width), or_map)),
        scratch_shapes=[pltpu.VMEM((nblk, SUBLANES, sw2), F32),
                        stage, pltpu.VMEM((2, rows, 2 * LANES), F32),
                        pltpu.VMEM((nblk, rows, sw2), F32), pltpu.VMEM((2, rows, sw2), F32),
                        pltpu.VMEM((2, rows, LANES), F32), stage, stage],
        compiler_params=_cparams(("arbitrary",)),
        name="s5_scan",
    )(*([pf] * (2 * n_batch)), bmat, cmat, lam)


def _s5_post_kernel(pf_ref, yf_ref, yr_ref, d_ref, w_ref, b_ref, o_ref, *, width):
    u = pf_ref[:, 0:width]
    y = d_ref[...] * u + yf_ref[0] + yr_ref[0]
    k0 = math.sqrt(2.0 / math.pi)
    g = 0.5 * y * (1.0 + jnp.tanh(k0 * (y + 0.044715 * (y * y * y))))
    z = _dot(g.astype(BF16), w_ref[...]) + b_ref[...]
    o_ref[...] = (g * jax.nn.sigmoid(z)).astype(BF16)


def _s5_post(pf, yf, yr, d_skip, glu_w, glu_b, *, n_batch, seq, ctx):
    n_rows = pf.shape[0]
    width = yf.shape[2]
    tp = ctx
    nlt = seq // tp
    kern = functools.partial(_s5_post_kernel, width=width)
    tok_map = lambda b, j: (jnp.where(j < nlt, b * nlt + j, n_batch * nlt + b), 0)
    return pl.pallas_call(
        kern,
        out_shape=jax.ShapeDtypeStruct((n_rows, width), BF16),
        grid=(n_batch, nlt + 1),
        in_specs=[pl.BlockSpec((tp, width), tok_map),
                  pl.BlockSpec((1, tp, width), lambda b, j: (b, j, 0)),
                  pl.BlockSpec((1, tp, width), lambda b, j: (b, j, 0)),
                  _const_spec((1, width)), _const_spec(glu_w.shape), _const_spec((1, width))],
        out_specs=pl.BlockSpec((tp, width), tok_map),
        compiler_params=_cparams(("arbitrary", "arbitrary")),
        name="s5_post",
    )(pf, yf, yr, d_skip, glu_w, glu_b)


def _merge_kernel(x_ref, mod_ref, nw_ref, *rest, has_tail, n_first):
    y_refs = rest[:N_BRANCH + sum(has_tail)]
    wg_ref, wb_ref, wo_ref, o_ref, acc_ref = rest[len(y_refs):]
    x = x_ref[...]
    d = x.shape[1]
    m = mod_ref[0]
    n = _rms_mod(x, nw_ref[...], m[3:4], m[4:5]).astype(BF16)
    ys = []
    pos = 0
    for b in range(N_BRANCH):
        y = y_refs[pos][...]
        pos += 1
        if has_tail[b]:
            y = jnp.where(pl.program_id(0) < n_first, y, y_refs[pos][...])
            pos += 1
        ys.append(y)
    for b in range(N_BRANCH):
        gate = jax.nn.sigmoid(_dot(n, wg_ref[:, b * d:(b + 1) * d]))
        contrib = gate * _dot(ys[b], wb_ref[b])
        if b == 0:
            acc_ref[...] = contrib
        else:
            acc_ref[...] += contrib
    o_ref[...] = x + m[5:6] * _dot(acc_ref[...].astype(BF16), wo_ref[...])


def _merge(h, n_rows, mod, norm_w, ys, wg, wb, wo, *, n_lat_tiles, tiles_per_batch, n_batch):
    d = h.shape[1]
    bw = wb.shape[1]
    n_first = n_lat_tiles
    has_tail = tuple(isinstance(y, tuple) for y in ys)
    y_specs, y_ops = [], []
    for y in ys:
        if isinstance(y, tuple):
            y_specs += [pl.BlockSpec((TM, bw), lambda i: (jnp.minimum(i, n_first - 1), 0)),
                        pl.BlockSpec((TM, bw), lambda i: (jnp.maximum(i - n_first, 0), 0))]
            y_ops += list(y)
        else:
            y_specs.append(pl.BlockSpec((TM, bw), lambda i: (i, 0)))
            y_ops.append(y)
    kern = functools.partial(_merge_kernel, has_tail=has_tail, n_first=n_first)
    return pl.pallas_call(
        kern,
        out_shape=jax.ShapeDtypeStruct((n_rows, d), F32),
        grid=(n_rows // TM,),
        in_specs=[pl.BlockSpec((TM, d), lambda i: (i, 0)),
                  pl.BlockSpec((1, N_MOD, d), lambda i: (_mod_row_map(n_lat_tiles, tiles_per_batch, n_batch)(i), 0, 0)),
                  _const_spec((1, d))]
                 + y_specs
                 + [_const_spec(wg.shape), _const_spec(wb.shape), _const_spec(wo.shape)],
        out_specs=pl.BlockSpec((TM, d), lambda i: (i, 0)),
        scratch_shapes=[pltpu.VMEM((TM, d), F32)],
        compiler_params=_cparams(("arbitrary",)),
        name="merge",
    )(h, mod, norm_w, *y_ops, wg, wb, wo)


def _swap_rot_pairs(w, nf):
    lead = w.shape[:-1]
    n = w.shape[-1]
    return w.reshape(lead + (n // (2 * nf), 2, nf))[..., ::-1, :].reshape(lead + (n,))


def _rope_tables(seq, dim, lane_off, width, period):
    nf = dim // 4
    pos = jnp.arange(seq)
    rows = (pos // GRID_W).astype(F32)
    cols = (pos % GRID_W).astype(F32)
    inv_freq = ROPE_BASE ** (-jnp.arange(nf, dtype=F32) / nf)
    ang_r = rows[:, None] * inv_freq[None, :]
    ang_c = cols[:, None] * inv_freq[None, :]
    cos = jnp.concatenate([jnp.cos(ang_r)] * 2 + [jnp.cos(ang_c)] * 2, axis=1)
    sin = jnp.concatenate([-jnp.sin(ang_r), jnp.sin(ang_r), -jnp.sin(ang_c), jnp.sin(ang_c)], axis=1)
    c_per = jnp.ones((seq, period), F32).at[:, lane_off:lane_off + dim].set(cos)
    s_per = jnp.zeros((seq, period), F32).at[:, lane_off:lane_off + dim].set(sin)
    reps = width // period
    c_tab = jnp.concatenate([jnp.tile(c_per, (1, reps)), jnp.ones((TM, width), F32)], axis=0)
    s_tab = jnp.concatenate([jnp.tile(s_per, (1, reps)), jnp.zeros((TM, width), F32)], axis=0)
    return c_tab, s_tab


def _inproj_weights(w_in):
    d = w_in.shape[0]
    hw = NA_HEADS * HEAD_DIM
    o_sq = 3 * hw
    o_sk = o_sq + SWA_HEADS * HEAD_DIM
    o_sv = o_sk + SWA_KV_HEADS * HEAD_DIM
    o_s5 = o_sv + SWA_KV_HEADS * HEAD_DIM
    s5w = 512
    o_cq = o_s5 + s5w
    o_ckv = o_cq + 256
    o_kr = o_ckv + 128
    o_g = o_kr + MLA_ROPE
    qscale = LOG2E * HEAD_DIM ** -0.5
    na = jnp.concatenate([w_in[:, :hw] * qscale, w_in[:, hw:3 * hw]], axis=1)
    sq = w_in[:, o_sq:o_sk] * qscale
    sk = w_in[:, o_sk:o_sv]
    sv = w_in[:, o_sv:o_s5]
    def dup_heads(a):
        return jnp.concatenate([a[:, kv * HEAD_DIM:(kv + 1) * HEAD_DIM]
                                for kv in range(SWA_KV_HEADS) for _ in range(2)], axis=1)

    sk_dup = dup_heads(sk)
    sv_dup = dup_heads(sv)
    wr = jnp.concatenate([sq, sk_dup], axis=1)
    kr = w_in[:, o_kr:o_g]
    lpad = jnp.zeros((d, MLA_NOPE), F32)
    rpad = jnp.zeros((d, LANES - MLA_NOPE - MLA_ROPE), F32)
    wf = jnp.concatenate([w_in[:, o_s5:o_kr], lpad, kr, rpad], axis=1)
    wg = w_in[:, o_g:]
    proj = tuple(a.astype(BF16) for a in (na, wr, sv_dup, wf))
    return proj, wg.astype(BF16)


def _mla_weights(w_uq, w_ukv):
    ql = w_uq.shape[0]
    kvl = w_ukv.shape[0]
    dq = MLA_NOPE + MLA_ROPE
    wq3 = w_uq.reshape(ql, MLA_HEADS, dq)
    pad = jnp.zeros((ql, MLA_HEADS, LANES - dq), F32)
    qscale = math.log2(math.e) * dq ** -0.5
    wq = jnp.concatenate([wq3 * qscale, pad], axis=2).reshape(ql, MLA_HEADS * LANES)
    wkv3 = w_ukv.reshape(kvl, MLA_HEADS, MLA_NOPE + MLA_V)
    wk = jnp.concatenate([wkv3[:, :, :MLA_NOPE], jnp.zeros((kvl, MLA_HEADS, LANES - MLA_NOPE), F32)], axis=2)
    wk = wk.reshape(kvl, MLA_HEADS * LANES)
    wv = jnp.concatenate([wkv3[:, :, MLA_NOPE:], jnp.zeros((kvl, MLA_HEADS, LANES - MLA_V), F32)], axis=2)
    wv = wv.reshape(kvl, MLA_HEADS * LANES)
    rope_sw = _swap_rot_pairs(wq3[:, :, MLA_NOPE:], MLA_ROPE // 4) * qscale
    wqs = jnp.concatenate([jnp.zeros((ql, MLA_HEADS, MLA_NOPE), F32), rope_sw, pad], axis=2)
    wqs = wqs.reshape(ql, MLA_HEADS * LANES)
    return tuple(a.astype(BF16) for a in (wq, wqs, wk, wv))


def _s5_params(lam_re, lam_im, log_dt, b_re, b_im, c_re, c_im):
    a = lam_re.astype(F32)
    w = lam_im.astype(F32)
    dt = jnp.exp(log_dt.astype(F32))[..., None]
    mag = jnp.exp(a * dt)
    lb_re = mag * jnp.cos(w * dt)
    lb_im = mag * jnp.sin(w * dt)
    den = a * a + w * w
    cf_re = ((lb_re - 1.0) * a + lb_im * w) / den
    cf_im = (lb_im * a - (lb_re - 1.0) * w) / den
    bb_re = cf_re[..., None] * b_re - cf_im[..., None] * b_im
    bb_im = cf_re[..., None] * b_im + cf_im[..., None] * b_re
    n_dir, g, p, cg = b_re.shape
    gpb = LANES // cg
    nblk = g // gpb
    eye = jnp.eye(gpb, dtype=F32)

    def in_map(x):
        x5 = jnp.swapaxes(x.reshape(n_dir, nblk, gpb, p, cg), 3, 4)
        full = x5[:, :, :, :, None, :] * eye[None, None, :, None, :, None]
        return full.reshape(n_dir, nblk, gpb * cg, gpb * p)

    def out_map(x):
        x5 = jnp.swapaxes(x.reshape(n_dir, nblk, gpb, cg, p), 3, 4)
        full = x5[:, :, :, :, None, :] * eye[None, None, :, None, :, None]
        return full.reshape(n_dir, nblk, gpb * p, gpb * cg)

    b_in = jnp.concatenate([in_map(bb_re), in_map(bb_im)], axis=3)
    bmat = jnp.concatenate([b_in[0], b_in[1]], axis=1).astype(BF16)
    c_out = jnp.concatenate([out_map(c_re.astype(F32)), out_map(-c_im.astype(F32))], axis=2)
    cmat = jnp.concatenate([c_out[0], c_out[1]], axis=2).astype(BF16)
    half = SUBLANES // 2
    lam2 = jnp.concatenate([lb_re.reshape(n_dir, nblk, gpb * p), lb_im.reshape(n_dir, nblk, gpb * p)], axis=2)
    lam = jnp.concatenate([jnp.broadcast_to(lam2[0][:, None, :], (nblk, half, 2 * gpb * p)),
                           jnp.broadcast_to(lam2[1][:, None, :], (nblk, half, 2 * gpb * p))], axis=1)
    return bmat, cmat, lam


def _ffn_weights(wg, wu, wd):
    d, ff = wg.shape
    wg3 = wg.astype(BF16)
    wu3 = wu.astype(BF16)
    wd3 = wd.astype(BF16)
    return wg3, wu3, wd3


def kernel(x, c, ctx, c_ctx, ada_w, ada_b, ffn1_norm, ffn1_w_gate, ffn1_w_up, ffn1_w_down, mix_norm, w_in, na_rpb, swa_sink, s5_lambda_re, s5_lambda_im, s5_log_dt, s5_b_re, s5_b_im, s5_c_re, s5_c_im, s5_d, s5_glu_w, s5_glu_b, mla_q_norm, mla_w_uq, mla_kv_norm, mla_w_ukv, w_branch, w_out, ffn2_norm, ffn2_w_gate, ffn2_w_up, ffn2_w_down, final_norm):
    n_batch, seq, d = x.shape
    n_ctx = ctx.shape[1]
    depth = ada_w.shape[0]
    assert 2 * n_batch == SUBLANES and seq % TM == 0 and (n_batch * n_ctx) % TM == 0
    n_lat = n_batch * seq
    n_all = n_lat + n_batch * n_ctx
    tiles_per_batch = seq // TM
    n_lat_tiles = n_lat // TM
    geo = dict(n_lat_tiles=n_lat_tiles, tiles_per_batch=tiles_per_batch, n_batch=n_batch)

    h = x.reshape(n_lat, d)
    h_ctx = ctx.reshape(n_batch * n_ctx, d)
    cc = jnp.concatenate([c, c_ctx[None, :], jnp.zeros((SUBLANES - n_batch - 1, d), F32)], axis=0)
    mod = _ada_mod(cc, ada_w, ada_b)

    cs_sw, sn_sw = _rope_tables(seq, HEAD_DIM, 0, 2 * LANES, HEAD_DIM)
    ck_kr, sk_kr = _rope_tables(seq, MLA_ROPE, LANES + MLA_NOPE, 2 * LANES, 2 * LANES)
    cq_ml, sq_ml = _rope_tables(seq, MLA_ROPE, MLA_NOPE, 2 * LANES, LANES)
    rows_n = seq // GRID_W
    s5w = s5_d.shape[1]
    offs = (s5w, s5w + mla_w_uq.shape[1], s5w + mla_w_uq.shape[1] + mla_w_ukv.shape[1])
    sw_col0 = 3 * NA_HEADS * HEAD_DIM

    w_ffn1 = _ffn_weights(ffn1_w_gate[0], ffn1_w_up[0], ffn1_w_down[0])
    for l in range(depth):
        need_ctx = l < depth - 1
        last = l == depth - 1
        ml = mod[l]
        ones = jnp.ones((1, d), F32)
        h, *w_ffn2 = _ffn(h, n_all, ml, ffn1_norm[l][None, :], *w_ffn1, ones, base=0, final=False,
                          h_tail=h_ctx if l == 0 else None,
                          cast_next=(ffn2_w_gate[l], ffn2_w_up[l], ffn2_w_down[l]), **geo)
        proj_w, gate_w = _inproj_weights(w_in[l])
        pb, pf = _inproj(h, ml, mix_norm[l][None, :], proj_w, (cs_sw, sn_sw, ck_kr, sk_kr), **geo)
        bias = _na_bias_table(na_rpb[l].astype(F32), rows_n)
        y_na = _na_attention(pb, bias, n_batch=n_batch, seq=seq, ctx=n_ctx, need_ctx=need_ctx)
        y_sw_l, y_sw_c = _swa_attention(pb, swa_sink[l].astype(F32) * LOG2E, n_batch=n_batch, seq=seq, ctx=n_ctx,
                                        need_ctx=need_ctx, col0=sw_col0)
        y_sw = (y_sw_l, y_sw_c) if need_ctx else y_sw_l
        wq, wqs, wk, wv = _mla_weights(mla_w_uq[l], mla_w_ukv[l])
        qm, km, vm = _mla_prep(pf, mla_q_norm[l][None, :], mla_kv_norm[l][None, :], wq, wqs, wk, wv, cq_ml, sq_ml,
                               n_lat_tiles=n_lat_tiles, tiles_per_batch=tiles_per_batch, offs=offs)
        y_mla = _mla_attention(qm, km, vm, n_batch=n_batch, seq=seq, ctx=n_ctx, need_ctx=need_ctx)
        s5p = _s5_params(s5_lambda_re[l], s5_lambda_im[l], s5_log_dt[l], s5_b_re[l], s5_b_im[l],
                         s5_c_re[l], s5_c_im[l])
        yf, yr = _s5_scan(pf, *s5p, n_batch=n_batch, seq=seq, ctx=n_ctx, width=s5w)
        y_s5 = _s5_post(pf, yf, yr, s5_d[l][None, :].astype(F32), s5_glu_w[l].astype(BF16),
                        s5_glu_b[l][None, :].astype(F32), n_batch=n_batch, seq=seq, ctx=n_ctx)
        n_rows = n_all if need_ctx else n_lat
        h = _merge(h, n_rows, ml, mix_norm[l][None, :], (y_na, y_sw, y_s5, y_mla), gate_w,
                   w_branch[l].astype(BF16), w_out[l].astype(BF16), **geo)
        if last:
            h = _ffn(h, n_rows, ml, ffn2_norm[l][None, :], *w_ffn2, final_norm[None, :], base=6, final=True, **geo)
        else:
            h, *w_ffn1 = _ffn(h, n_rows, ml, ffn2_norm[l][None, :], *w_ffn2, final_norm[None, :], base=6, final=False,
                              cast_next=(ffn1_w_gate[l + 1], ffn1_w_up[l + 1], ffn1_w_down[l + 1]), **geo)
    return h.reshape(n_batch, seq, d)
```

```python
import functools
import math

import numpy as np
import jax
import jax.numpy as jnp
from jax import lax
from jax.experimental import pallas as pl
from jax.experimental.pallas import tpu as pltpu

F32 = jnp.float32
BF16 = jnp.bfloat16

GRID_W = 64
HEAD_DIM = 64
N_BRANCH = 4
NA_HEADS = 8
NA_WIN_ROWS = 8
NA_WIN_COLS = 16
SWA_HEADS = 8
SWA_KV_HEADS = 2
SWA_WINDOW = 128
S5_GROUP = 16
S5_STATE = 64
MLA_HEADS = 8
MLA_NOPE = 64
MLA_ROPE = 32
MLA_V = 64
MACARON_WEIGHT = 0.5
ROPE_BASE = 10000.0
EPS = 1e-6
N_MOD = 9

LANES = 128
SUBLANES = 8
TM = 512
FF_CHUNK = 256
NA_QROWS = 4
NA_KROWS = NA_QROWS + NA_WIN_ROWS - 1
SWA_BLK = 128
SWA_UNROLL = 2
MLA_TQ = 512
MLA_SUB = 256
MLA_AHEAD = 1
S5_STEPS = 128
S5_POST_ROWS = 1024
S5_PITCH = S5_STEPS + SUBLANES
NEG = -1e30
LOG2E = math.log2(math.e)
VMEM_LIMIT = 56 * 1024 * 1024


def _cparams(sem, flags=None):
    return pltpu.CompilerParams(dimension_semantics=sem, vmem_limit_bytes=VMEM_LIMIT, flags=flags)


def _const_spec(shape):
    nd = len(shape)
    return pl.BlockSpec(shape, lambda *_: (0,) * nd, pipeline_mode=pl.Buffered(1))


def _dot(a, b):
    return jnp.dot(a, b, preferred_element_type=F32)


def _dot_t(a, b):
    return lax.dot_general(a, b, (((1,), (1,)), ((), ())), preferred_element_type=F32)


def _rms(x, w):
    return x * lax.rsqrt(jnp.mean(x * x, axis=-1, keepdims=True) + EPS) * w


def _rms_mod(x, w, shift, scale):
    return _rms(x, w) * (1.0 + scale) + shift


def _mod_row_map(n_lat_tiles, tiles_per_batch, n_batch):
    def f(i):
        return jnp.where(i < n_lat_tiles, i // tiles_per_batch, n_batch)
    return f


def _ada_kernel(c_ref, w_ref, b_ref, o_ref):
    c = c_ref[...]
    s = c * jax.nn.sigmoid(c)
    w = w_ref[0]
    s_hi = s.astype(BF16)
    s_lo = (s - s_hi.astype(F32)).astype(BF16)
    w_hi = w.astype(BF16)
    w_lo = (w - w_hi.astype(F32)).astype(BF16)
    o_ref[0] = _dot(s_hi, w_hi) + (_dot(s_hi, w_lo) + _dot(s_lo, w_hi)) + b_ref[0]


def _ada_mod(cc, ada_w, ada_b):
    depth, d, nd = ada_w.shape
    tn = 1024
    out = pl.pallas_call(
        _ada_kernel,
        out_shape=jax.ShapeDtypeStruct((depth, SUBLANES, nd), F32),
        grid=(depth, nd // tn),
        in_specs=[pl.BlockSpec((SUBLANES, d), lambda l, j: (0, 0)),
                  pl.BlockSpec((1, d, tn), lambda l, j: (l, 0, j)),
                  pl.BlockSpec((1, 1, tn), lambda l, j: (l, 0, j))],
        out_specs=pl.BlockSpec((1, SUBLANES, tn), lambda l, j: (l, 0, j)),
        compiler_params=_cparams(("arbitrary", "arbitrary")),
        name="ada_mod",
    )(cc, ada_w, ada_b.reshape(depth, 1, nd))
    return out.reshape(depth, SUBLANES, N_MOD, d)


def _ffn_kernel(x_ref, xc_ref, mod_ref, nw_ref, wg_ref, wu_ref, wd_ref, fw_ref, *rest, base, final, n_first):
    if len(rest) == 1:
        (o_ref,) = rest
    else:
        ng_ref, nu_ref, nd_ref, o_ref, cg_ref, cu_ref, cd_ref = rest
        cg_ref[...] = ng_ref[...].astype(BF16)
        cu_ref[...] = nu_ref[...].astype(BF16)
        cd_ref[...] = nd_ref[...].astype(BF16)
    x = x_ref[...]
    if n_first is not None:
        x = jnp.where(pl.program_id(0) < n_first, x, xc_ref[...])
    m = mod_ref[0]
    n = _rms_mod(x, nw_ref[...], m[base:base + 1], m[base + 1:base + 2]).astype(BF16)
    ff = wg_ref.shape[1]
    fc = FF_CHUNK if ff % FF_CHUNK == 0 else ff
    acc = None
    for c in range(ff // fc):
        cs = slice(c * fc, (c + 1) * fc)
        g = _dot(n, wg_ref[:, cs])
        u = _dot(n, wu_ref[:, cs])
        a = (g * jax.nn.sigmoid(g) * u).astype(BF16)
        y = _dot(a, wd_ref[cs, :])
        acc = y if acc is None else acc + y
    out = x + MACARON_WEIGHT * m[base + 2:base + 3] * acc
    if final:
        out = _rms(out, fw_ref[...])
    o_ref[...] = out


def _ffn(h, n_rows, mod, norm_w, wg, wu, wd, final_w, *, base, final, n_lat_tiles, tiles_per_batch, n_batch,
         h_tail=None, cast_next=None):
    d = h.shape[1]
    ff = wg.shape[1]
    steps = n_rows // TM
    if h_tail is None:
        n_first = None
        h_tail = h
        x_map = lambda i: (i, 0)
        t_map = lambda i: (0, 0)
    else:
        n_first = h.shape[0] // TM
        x_map = lambda i: (jnp.minimum(i, n_first - 1), 0)
        t_map = lambda i: (jnp.maximum(i - n_first, 0), 0)
    kern = functools.partial(_ffn_kernel, base=base, final=final, n_first=n_first)
    in_specs = [pl.BlockSpec((TM, d), x_map),
                pl.BlockSpec((TM, d) if n_first is not None else (SUBLANES, d), t_map),
                pl.BlockSpec((1, N_MOD, d), lambda i: (_mod_row_map(n_lat_tiles, tiles_per_batch, n_batch)(i), 0, 0)),
                _const_spec((1, d)),
                _const_spec((d, ff)), _const_spec((d, ff)), _const_spec((ff, d)),
                _const_spec((1, d))]
    out_shape = jax.ShapeDtypeStruct((n_rows, d), F32)
    out_specs = pl.BlockSpec((TM, d), lambda i: (i, 0))
    operands = [h, h_tail, mod, norm_w, wg, wu, wd, final_w]
    if cast_next is not None:
        up_rows = min(r for r in range(2 * SUBLANES, d + 1, 2 * SUBLANES) if d % r == 0 and d // r <= steps)
        dn_rows = min(r for r in range(2 * SUBLANES, ff + 1, 2 * SUBLANES) if ff % r == 0 and ff // r <= steps)
        n_up, n_dn = d // up_rows, ff // dn_rows
        up_spec = pl.BlockSpec((up_rows, ff), lambda i: (jnp.minimum(i, n_up - 1), 0))
        dn_spec = pl.BlockSpec((dn_rows, d), lambda i: (jnp.minimum(i, n_dn - 1), 0))
        nl, *next_w = cast_next
        in_specs += [pl.BlockSpec((None, up_rows, ff), lambda i: (nl, jnp.minimum(i, n_up - 1), 0))] * 2
        in_specs += [pl.BlockSpec((None, dn_rows, d), lambda i: (nl, jnp.minimum(i, n_dn - 1), 0))]
        operands += next_w
        out_shape = (out_shape, jax.ShapeDtypeStruct((d, ff), BF16), jax.ShapeDtypeStruct((d, ff), BF16),
                     jax.ShapeDtypeStruct((ff, d), BF16))
        out_specs = (out_specs, up_spec, up_spec, dn_spec)
    return pl.pallas_call(
        kern,
        out_shape=out_shape,
        grid=(steps,),
        in_specs=in_specs,
        out_specs=out_specs,
        compiler_params=_cparams(("arbitrary",)),
        name="ffn_final" if final else "ffn",
    )(*operands)


def _rope_apply(y, cs, sn, nf):
    w = y.shape[1]
    first = (lax.broadcasted_iota(jnp.int32, y.shape, 1) & nf) == 0
    ysw = jnp.where(first, pltpu.roll(y, w - nf, 1), pltpu.roll(y, nf, 1))
    return y * cs + ysw * sn


def _inproj_kernel(x_ref, mod_ref, nw_ref, wa_ref, wr_ref, wv_ref, wf_ref,
                   cs_ref, sn_ref, ck_ref, sk_ref,
                   qn_ref, kn_ref, wq_ref, wqs_ref, wk_ref, wvm_ref, cq_ref, sq_ref,
                   ob_ref, of_ref, q_ref, k_ref, v_ref):
    x = x_ref[...]
    m = mod_ref[0]
    n = _rms_mod(x, nw_ref[...], m[3:4], m[4:5]).astype(BF16)
    na = wa_ref.shape[1]
    nr = wr_ref.shape[1]
    nv = wv_ref.shape[1]
    nf = wf_ref.shape[1]
    cw = 2 * LANES
    cq = _dot(n, wf_ref[:, nf - 2 * cw:nf - cw])
    y = _rope_apply(_dot(n, wf_ref[:, nf - cw:nf]), ck_ref[...], sk_ref[...], MLA_ROPE // 4)
    _mla_project(cq, y[:, :LANES], y[:, LANES:], qn_ref, kn_ref, wq_ref, wqs_ref, wk_ref, wvm_ref,
                 cq_ref, sq_ref, q_ref, k_ref, v_ref)
    for c in range(nf // cw - 2):
        of_ref[:, c * cw:(c + 1) * cw] = _dot(n, wf_ref[:, c * cw:(c + 1) * cw])
    for c in range(na // cw):
        ob_ref[:, c * cw:(c + 1) * cw] = _dot(n, wa_ref[:, c * cw:(c + 1) * cw]).astype(BF16)
    for c in range(nr // cw):
        y = _dot(n, wr_ref[:, c * cw:(c + 1) * cw])
        ob_ref[:, na + c * cw:na + (c + 1) * cw] = _rope_apply(y, cs_ref[...], sn_ref[...], HEAD_DIM // 4).astype(BF16)
    for c in range(nv // cw):
        ob_ref[:, na + nr + c * cw:na + nr + (c + 1) * cw] = _dot(n, wv_ref[:, c * cw:(c + 1) * cw]).astype(BF16)


def _inproj(h, mod, norm_w, w, tabs, mla, *, n_lat_tiles, tiles_per_batch, n_batch):
    n_rows, d = h.shape
    wa, wr, wv, wf = w
    cs, sn, ck, sk = tabs
    qn, kn, wq, wqs, wk, wvm, cq_tab, sq_tab = mla
    cw = 2 * LANES
    assert wq.shape[0] == cw and wk.shape[0] == LANES
    nb = wa.shape[1] + wr.shape[1] + wv.shape[1]
    ns5 = wf.shape[1] - 2 * cw
    hq = MLA_HEADS * LANES
    tab_map = lambda i: (jnp.where(i < n_lat_tiles, i % tiles_per_batch, tiles_per_batch), 0)
    row = lambda width: pl.BlockSpec((TM, width), lambda i: (i, 0))
    return pl.pallas_call(
        _inproj_kernel,
        out_shape=(jax.ShapeDtypeStruct((n_rows, nb), BF16), jax.ShapeDtypeStruct((n_rows, ns5), F32))
                  + (jax.ShapeDtypeStruct((n_rows, hq), BF16),) * 3,
        grid=(n_rows // TM,),
        in_specs=[pl.BlockSpec((TM, d), lambda i: (i, 0)),
                  pl.BlockSpec((1, N_MOD, d), lambda i: (_mod_row_map(n_lat_tiles, tiles_per_batch, n_batch)(i), 0, 0)),
                  _const_spec((1, d))]
                 + [_const_spec(a.shape) for a in w]
                 + [pl.BlockSpec((TM, cw), tab_map)] * 4
                 + [_const_spec(a.shape) for a in (qn, kn, wq, wqs, wk, wvm)]
                 + [pl.BlockSpec((TM, cw), tab_map)] * 2,
        out_specs=(row(nb), row(ns5), row(hq), row(hq), row(hq)),
        compiler_params=_cparams(("arbitrary",)),
        name="inproj",
    )(h, mod, norm_w, *w, cs, sn, ck, sk, qn, kn, wq, wqs, wk, wvm, cq_tab, sq_tab)


def _lane_half_masks(rows):
    lane = lax.broadcasted_iota(jnp.int32, (rows, LANES), 1)
    return lane < HEAD_DIM


def _with_ones(v):
    return jnp.concatenate([v, jnp.ones_like(v)], axis=1)


def _na_kernel(q_ref, k_ref, v_ref, kc_ref, vc_ref, bias_ref, o_ref, *, n_blk, rows_n):
    blk = pl.program_id(1)
    nq = NA_QROWS * GRID_W
    nk = NA_KROWS * GRID_W
    ws = jnp.clip(NA_QROWS * blk - NA_WIN_ROWS // 2, 0, rows_n - NA_KROWS)
    r0 = pl.multiple_of(ws * GRID_W, GRID_W)
    lo = _lane_half_masks(nq)
    for j in range(NA_HEADS // 2):
        sl = slice(j * LANES, (j + 1) * LANES)
        qp = q_ref[:, sl]
        kw = k_ref[pl.ds(r0, nk), sl]
        vw = v_ref[pl.ds(r0, nk), sl]
        kc = kc_ref[:, sl]
        vc = vc_ref[:, sl]
        zero = jnp.zeros_like(qp)
        qs = jnp.concatenate([jnp.where(lo, qp, zero), jnp.where(lo, zero, qp)], axis=0)
        s1 = _dot_t(qs, kw) + bias_ref[0, j]
        s2 = _dot_t(qs, kc)
        m = jnp.maximum(jnp.max(s1, axis=-1, keepdims=True), jnp.max(s2, axis=-1, keepdims=True))
        o2 = (_dot(jnp.exp2(s1 - m).astype(BF16), _with_ones(vw))
              + _dot(jnp.exp2(s2 - m).astype(BF16), _with_ones(vc)))
        o = o2[:, :LANES] / o2[:, LANES:]
        o_ref[:, sl] = jnp.where(lo, o[:nq], o[nq:]).astype(BF16)


def _na_attention(pb, bias, *, n_batch, seq, ctx, need_ctx):
    rows_n = seq // GRID_W
    n_blk = rows_n // NA_QROWS
    nq = NA_QROWS * GRID_W
    hw = NA_HEADS * HEAD_DIM
    n_q = n_blk + (1 if need_ctx else 0)
    assert ctx == nq
    lat_blocks = n_batch * seq // nq
    qmap = lambda b, i: (jnp.where(i < n_blk, b * n_blk + i, lat_blocks + b), 0)

    def bias_map(b, i):
        t = jnp.where(i == 0, 0, jnp.where(i == 1, 1, jnp.where(i == n_blk - 1, 3, jnp.where(i == n_blk, 4, 2))))
        return (t, 0, 0, 0)

    kern = functools.partial(_na_kernel, n_blk=n_blk, rows_n=rows_n)
    return pl.pallas_call(
        kern,
        out_shape=jax.ShapeDtypeStruct((n_batch * n_q * nq, hw), BF16),
        grid=(n_batch, n_q),
        in_specs=[pl.BlockSpec((nq, hw), qmap),
                  pl.BlockSpec((seq, hw), lambda b, i: (b, 1)),
                  pl.BlockSpec((seq, hw), lambda b, i: (b, 2)),
                  pl.BlockSpec((ctx, hw), lambda b, i: (n_batch * seq // ctx + b, 1)),
                  pl.BlockSpec((ctx, hw), lambda b, i: (n_batch * seq // ctx + b, 2)),
                  pl.BlockSpec((1, NA_HEADS // 2, 2 * nq, NA_KROWS * GRID_W), bias_map)],
        out_specs=pl.BlockSpec((nq, hw), qmap),
        compiler_params=_cparams(("arbitrary", "arbitrary")),
        name="na_attn",
    )(pb, pb, pb, pb, pb, bias)


def _na_bias_table(rpb, rows_n):
    n_blk = rows_n // NA_QROWS
    kr_n = min(NA_WIN_ROWS, rows_n)
    n_heads = rpb.shape[0]
    col = np.arange(GRID_W)
    c0 = np.clip(col - NA_WIN_COLS // 2, 0, GRID_W - NA_WIN_COLS)
    col_ok = (col[None, :] >= c0[:, None]) & (col[None, :] < c0[:, None] + NA_WIN_COLS)
    padded = jnp.pad(rpb * LOG2E, ((0, 0), (0, 0), (GRID_W, GRID_W)))
    off = GRID_W + NA_WIN_COLS - 1
    toep = jnp.stack([padded[:, :, off - qc:off - qc + GRID_W] for qc in range(GRID_W)], axis=2)
    toep = jnp.where(col_ok[None, None], toep, NEG)
    masked = jnp.full((n_heads, 1, GRID_W, GRID_W), NEG, F32)
    ext = jnp.concatenate([masked, toep, masked], axis=1)
    pair2 = jnp.concatenate([ext[:, :-1], ext[:, 1:]], axis=-1)
    n_dr = 2 * NA_WIN_ROWS - 1
    plans = []
    for blk in (0, 1, 2, n_blk - 1):
        ws = int(np.clip(NA_QROWS * blk - NA_WIN_ROWS // 2, 0, rows_n - NA_KROWS))
        plan = []
        for qr in range(NA_QROWS):
            r = NA_QROWS * blk + qr
            r0 = int(np.clip(r - kr_n // 2, 0, rows_n - kr_n))
            row = []
            for kk in range(0, NA_KROWS, 2):
                oks = tuple(r0 <= ws + kk + i < r0 + kr_n and kk + i < NA_KROWS for i in range(2))
                row.append((ws + kk - r + NA_WIN_ROWS, oks))
            plan.append(tuple(row))
        plans.append(tuple(plan))
    plans.append(None)
    nq = NA_QROWS * GRID_W
    nk = NA_KROWS * GRID_W
    kern = functools.partial(_na_bias_kernel, plans=tuple(plans))
    return pl.pallas_call(
        kern,
        out_shape=jax.ShapeDtypeStruct((len(plans), n_heads // 2, 2 * nq, nk), F32),
        grid=(n_heads // 2,),
        in_specs=[pl.BlockSpec((2, n_dr + 1, GRID_W, 2 * GRID_W), lambda j: (j, 0, 0, 0))],
        out_specs=pl.BlockSpec((len(plans), 1, 2 * nq, nk), lambda j: (0, j, 0, 0)),
        compiler_params=_cparams(("arbitrary",)),
        name="na_bias",
    )(pair2)


def _na_bias_kernel(p2_ref, o_ref, *, plans):
    nq = NA_QROWS * GRID_W
    lo = lax.broadcasted_iota(jnp.int32, (GRID_W, 2 * GRID_W), 1) < GRID_W
    neg = jnp.full((GRID_W, 2 * GRID_W), NEG, F32)
    for t, plan in enumerate(plans):
        if plan is None:
            o_ref[t, 0] = jnp.full(o_ref.shape[2:], NEG, F32)
            continue
        for half in range(2):
            for qr, row in enumerate(plan):
                rs = slice(half * nq + qr * GRID_W, half * nq + (qr + 1) * GRID_W)
                for kp, (e, (ok_a, ok_b)) in enumerate(row):
                    width = min(2 * GRID_W, o_ref.shape[3] - kp * 2 * GRID_W)
                    if ok_a or ok_b:
                        tile = p2_ref[half, e]
                        if not ok_a:
                            tile = jnp.where(lo, neg, tile)
                        if not ok_b:
                            tile = jnp.where(lo, tile, neg)
                    else:
                        tile = neg
                    o_ref[t, 0, rs, kp * 2 * GRID_W:kp * 2 * GRID_W + width] = tile[:, :width]


def _swa_kernel(sink_ref, q_ref, k_ref, v_ref, qc_ref, kc_ref, vc_ref, o_ref, oc_ref, *, seq, need_ctx):
    n_blk = seq // SWA_BLK
    band = 3 * SWA_BLK
    group = SWA_HEADS // SWA_KV_HEADS
    pairs = group // 2

    def group_attn(q_slabs, kv, kb, vb, kc, vc, mask_bias):
        r = q_slabs[0].shape[0]
        lo = _lane_half_masks(r)
        parts = []
        for qp in q_slabs:
            zero = jnp.zeros_like(qp)
            parts += [jnp.where(lo, qp, zero), jnp.where(lo, zero, qp)]
        qs = jnp.concatenate(parts, axis=0)
        row = lax.broadcasted_iota(jnp.int32, (group * r, 1), 0)
        sink = jnp.full((group * r, 1), sink_ref[kv * group + group - 1], F32)
        for g in range(group - 2, -1, -1):
            sink = jnp.where(row < (g + 1) * r, sink_ref[kv * group + g], sink)
        s2 = _dot_t(qs, kc)
        if kb is None:
            m = jnp.maximum(jnp.max(s2, axis=-1, keepdims=True), sink)
            o2 = _dot(jnp.exp2(s2 - m).astype(BF16), _with_ones(vc))
        else:
            s1 = _dot_t(qs, kb) + mask_bias
            m = jnp.maximum(jnp.maximum(jnp.max(s1, axis=-1, keepdims=True), jnp.max(s2, axis=-1, keepdims=True)), sink)
            o2 = (_dot(jnp.exp2(s1 - m).astype(BF16), _with_ones(vb))
                  + _dot(jnp.exp2(s2 - m).astype(BF16), _with_ones(vc)))
        o = o2[:, :LANES] / (o2[:, LANES:] + jnp.exp2(sink - m))
        return [jnp.where(lo, o[(2 * i) * r:(2 * i + 1) * r], o[(2 * i + 1) * r:(2 * i + 2) * r]) for i in range(pairs)]

    qi = lax.broadcasted_iota(jnp.int32, (group * SWA_BLK, band), 0) & (SWA_BLK - 1)
    ki = lax.broadcasted_iota(jnp.int32, (group * SWA_BLK, band), 1)

    def one_block(n):
        start = jnp.clip((n - 1) * SWA_BLK, 0, seq - band)
        start = pl.multiple_of(start, SWA_BLK)
        q0 = pl.multiple_of(n * SWA_BLK, SWA_BLK)
        delta = (start + ki) - (q0 + qi)
        mask_bias = jnp.where(jnp.abs(delta) <= SWA_WINDOW, 0.0, NEG).astype(F32)
        for kv in range(SWA_KV_HEADS):
            ksl = slice(kv * LANES, (kv + 1) * LANES)
            slabs = [q_ref[pl.ds(q0, SWA_BLK), (kv * pairs + i) * LANES:(kv * pairs + i + 1) * LANES]
                     for i in range(pairs)]
            outs = group_attn(slabs, kv, k_ref[pl.ds(start, band), ksl], v_ref[pl.ds(start, band), ksl],
                              kc_ref[:, ksl], vc_ref[:, ksl], mask_bias)
            for i in range(pairs):
                o_ref[pl.ds(q0, SWA_BLK), (kv * pairs + i) * LANES:(kv * pairs + i + 1) * LANES] = outs[i].astype(BF16)

    def blk_body(n2, carry):
        for j in range(SWA_UNROLL):
            one_block(n2 * SWA_UNROLL + j)
        return carry

    lax.fori_loop(0, n_blk // SWA_UNROLL, blk_body, 0)

    if need_ctx:
        for kv in range(SWA_KV_HEADS):
            ksl = slice(kv * LANES, (kv + 1) * LANES)
            slabs = [qc_ref[:, (kv * pairs + i) * LANES:(kv * pairs + i + 1) * LANES] for i in range(pairs)]
            outs = group_attn(slabs, kv, None, None, kc_ref[:, ksl], vc_ref[:, ksl], None)
            for i in range(pairs):
                oc_ref[:, (kv * pairs + i) * LANES:(kv * pairs + i + 1) * LANES] = outs[i].astype(BF16)
    else:
        oc_ref[...] = jnp.zeros_like(oc_ref)


def _swa_attention(pb, sink, *, n_batch, seq, ctx, need_ctx, col0):
    hw = SWA_HEADS * HEAD_DIM
    kw = SWA_KV_HEADS * LANES
    qcol = col0 // hw
    kcol = (col0 + hw) // kw
    vcol = kcol + 1
    cblk = n_batch * seq // ctx
    kern = functools.partial(_swa_kernel, seq=seq, need_ctx=need_ctx)
    return pl.pallas_call(
        kern,
        out_shape=(jax.ShapeDtypeStruct((n_batch * seq, hw), BF16),
                   jax.ShapeDtypeStruct((n_batch * ctx, hw), BF16)),
        grid=(n_batch,),
        in_specs=[pl.BlockSpec(memory_space=pltpu.SMEM),
                  pl.BlockSpec((seq, hw), lambda b: (b, qcol)),
                  pl.BlockSpec((seq, kw), lambda b: (b, kcol)),
                  pl.BlockSpec((seq, kw), lambda b: (b, vcol)),
                  pl.BlockSpec((ctx, hw), lambda b: (cblk + b, qcol)),
                  pl.BlockSpec((ctx, kw), lambda b: (cblk + b, kcol)),
                  pl.BlockSpec((ctx, kw), lambda b: (cblk + b, vcol))],
        out_specs=(pl.BlockSpec((seq, hw), lambda b: (b, 0)),
                   pl.BlockSpec((ctx, hw), lambda b: (b, 0))),
        compiler_params=_cparams(("arbitrary",)),
        name="swa_attn",
    )(sink, pb, pb, pb, pb, pb, pb)


def _mla_project(cq, ckv, kr, qn_ref, kn_ref, wq_ref, wqs_ref, wk_ref, wv_ref, cq_ref, sq_ref, q_ref, k_ref, v_ref):
    kr2 = jnp.concatenate([kr, kr], axis=1)
    cqn = _rms(cq, qn_ref[...]).astype(BF16)
    ckvn = _rms(ckv, kn_ref[...]).astype(BF16)
    cw = 2 * LANES
    for c in range(MLA_HEADS * LANES // cw):
        sl = slice(c * cw, (c + 1) * cw)
        q = _dot(cqn, wq_ref[:, sl]) * cq_ref[...] + _dot(cqn, wqs_ref[:, sl]) * sq_ref[...]
        q_ref[:, sl] = q.astype(BF16)
        k_ref[:, sl] = (_dot(ckvn, wk_ref[:, sl]) + kr2).astype(BF16)
    ones_hi = jnp.where(_lane_half_masks(1), 0.0, 1.0).astype(F32)
    ones_hi = jnp.concatenate([ones_hi, ones_hi], axis=1)
    for c in range(MLA_HEADS * LANES // cw):
        sl = slice(c * cw, (c + 1) * cw)
        v_ref[:, sl] = (_dot(ckvn, wv_ref[:, sl]) + ones_hi).astype(BF16)


def _mla_body(q_ref, k_ref, v_ref, kc_ref, vc_ref, o_ref):
    sub = MLA_SUB
    n_sub = q_ref.shape[0] // sub
    lo = _lane_half_masks(sub)
    units = [(r, half) for r in range(n_sub) for half in range(2)]

    def scores(r, half):
        sl = slice(half * LANES, (half + 1) * LANES)
        q = q_ref[r * sub:(r + 1) * sub, sl]
        s1 = None if k_ref is None else _dot_t(q, k_ref[:, sl])
        return s1, _dot_t(q, kc_ref[:, sl])

    def finish(half, s1, s2):
        sl = slice(half * LANES, (half + 1) * LANES)
        if s1 is None:
            m = jnp.max(s2, axis=-1, keepdims=True)
            return _dot(jnp.exp2(s2 - m).astype(BF16), vc_ref[:, sl])
        m = jnp.maximum(jnp.max(s1, axis=-1, keepdims=True), jnp.max(s2, axis=-1, keepdims=True))
        p1 = jnp.exp2(s1 - m)
        p2 = jnp.exp2(s2 - m)
        return _dot(p1.astype(BF16), v_ref[:, sl]) + _dot(p2.astype(BF16), vc_ref[:, sl])

    outs = {}
    ahead = MLA_AHEAD
    sc = {u: scores(*u) for u in units[:ahead]}
    for idx, u in enumerate(units):
        if idx + ahead < len(units):
            sc[units[idx + ahead]] = scores(*units[idx + ahead])
        outs[u] = finish(u[1], *sc.pop(u))
    for r in range(n_sub):
        o0, o1 = outs[(r, 0)], outs[(r, 1)]
        r0 = pltpu.roll(o0, HEAD_DIM, 1)
        r1 = pltpu.roll(o1, HEAD_DIM, 1)
        o_ref[r * sub:(r + 1) * sub, :] = jnp.where(lo, o0 / r0, r1 / o1).astype(BF16)


def _mla_kernel(q_ref, k_ref, v_ref, kc_ref, vc_ref, o_ref):
    _mla_body(q_ref, k_ref, v_ref, kc_ref, vc_ref, o_ref)


def _mla_ctx_kernel(q_ref, kc_ref, vc_ref, o_ref):
    _mla_body(q_ref, None, None, kc_ref, vc_ref, o_ref)


def _mla_attention(qm, km, vm, *, n_batch, seq, ctx, need_ctx):
    n_qt = seq // MLA_TQ
    cblk = n_batch * seq // ctx
    hv = MLA_HEADS * MLA_V
    y_lat = pl.pallas_call(
        _mla_kernel,
        out_shape=jax.ShapeDtypeStruct((n_batch * seq, hv), BF16),
        grid=(n_batch, MLA_HEADS // 2, n_qt),
        in_specs=[pl.BlockSpec((MLA_TQ, 2 * LANES), lambda b, p, i: (b * n_qt + i, p)),
                  pl.BlockSpec((seq, 2 * LANES), lambda b, p, i: (b, p)),
                  pl.BlockSpec((seq, 2 * LANES), lambda b, p, i: (b, p)),
                  pl.BlockSpec((ctx, 2 * LANES), lambda b, p, i: (cblk + b, p)),
                  pl.BlockSpec((ctx, 2 * LANES), lambda b, p, i: (cblk + b, p))],
        out_specs=pl.BlockSpec((MLA_TQ, LANES), lambda b, p, i: (b * n_qt + i, p)),
        compiler_params=_cparams(("arbitrary", "arbitrary", "arbitrary")),
        name="mla_attn",
    )(qm, km, vm, km, vm)
    if not need_ctx:
        return y_lat
    assert ctx % MLA_SUB == 0
    y_ctx = pl.pallas_call(
        _mla_ctx_kernel,
        out_shape=jax.ShapeDtypeStruct((n_batch * ctx, hv), BF16),
        grid=(n_batch, MLA_HEADS // 2),
        in_specs=[pl.BlockSpec((ctx, 2 * LANES), lambda b, p: (cblk + b, p)),
                  pl.BlockSpec((ctx, 2 * LANES), lambda b, p: (cblk + b, p)),
                  pl.BlockSpec((ctx, 2 * LANES), lambda b, p: (cblk + b, p))],
        out_specs=pl.BlockSpec((ctx, LANES), lambda b, p: (b, p)),
        compiler_params=_cparams(("arbitrary", "arbitrary")),
        name="mla_ctx_attn",
    )(qm, km, vm)
    return y_lat, y_ctx


def _s5_scan_kernel(uf0_ref, uf1_ref, uf2_ref, uf3_ref, ur0_ref, ur1_ref, ur2_ref, ur3_ref,
                    bmat_ref, cmat_ref, lam_ref, of_ref, or_ref,
                    h_ref, stage_ref, lhs_ref, bu_ref, st_ref, ysc_ref, ya_ref, yb_ref):
    c = pl.program_id(0)
    n_seq = SUBLANES
    half = n_seq // 2
    steps = S5_STEPS
    pitch = S5_PITCH
    rows = steps * n_seq
    nblk = h_ref.shape[0]
    sw = h_ref.shape[2] // 2
    ufs = (uf0_ref, uf1_ref, uf2_ref, uf3_ref)
    urs = (ur0_ref, ur1_ref, ur2_ref, ur3_ref)
    is_fwd = (lax.broadcasted_iota(jnp.int32, (rows, 1), 0) % n_seq) < half
    sub_fwd = lax.broadcasted_iota(jnp.int32, (n_seq, LANES), 0) < half

    @pl.when(c == 0)
    def _():
        h_ref[...] = jnp.zeros_like(h_ref)

    def project_in(k):
        ks = slice(k * LANES, (k + 1) * LANES)
        z = k % 2
        for b in range(half):
            stage_ref[k, b * pitch:b * pitch + steps, :] = ufs[b][:, ks]
            stage_ref[k, (half + b) * pitch:(half + b) * pitch + steps, :] = urs[b][:, ks]
        for t in range(steps):
            ga = stage_ref[k, pl.ds(t, n_seq, stride=pitch), :]
            gb = stage_ref[k, pl.ds(steps - 1 - t, n_seq, stride=pitch), :]
            lhs_ref[z, t * n_seq:(t + 1) * n_seq, 0:LANES] = jnp.where(sub_fwd, ga, 0.0)
            lhs_ref[z, t * n_seq:(t + 1) * n_seq, LANES:2 * LANES] = jnp.where(sub_fwd, 0.0, gb)
        bu_ref[k] = _dot(lhs_ref[z].astype(BF16), bmat_ref[k])

    def recur(k):
        z = k % 2
        lre = lam_ref[k, :, 0:sw]
        lim = lam_ref[k, :, sw:2 * sw]
        hr = h_ref[k, :, 0:sw]
        hi = h_ref[k, :, sw:2 * sw]
        for t in range(steps):
            rs = slice(t * n_seq, (t + 1) * n_seq)
            hr, hi = (lre * hr - lim * hi + bu_ref[k, rs, 0:sw],
                      lre * hi + lim * hr + bu_ref[k, rs, sw:2 * sw])
            st_ref[z, rs, 0:sw] = hr
            st_ref[z, rs, sw:2 * sw] = hi
        h_ref[k, :, 0:sw] = hr
        h_ref[k, :, sw:2 * sw] = hi

    def read_out(k):
        ks = slice(k * LANES, (k + 1) * LANES)
        z = k % 2
        hr_rows = rows // 2
        for part in range(2):
            rs = slice(part * hr_rows, (part + 1) * hr_rows)
            y2 = _dot(st_ref[z, rs, :].astype(BF16), cmat_ref[k])
            ysc_ref[z, rs, :] = jnp.where(is_fwd[rs], y2[:, 0:LANES], y2[:, LANES:2 * LANES])
        for t in range(steps):
            g = ysc_ref[z, t * n_seq:(t + 1) * n_seq, :]
            ya_ref[k, pl.ds(t, n_seq, stride=pitch), :] = g
            yb_ref[k, pl.ds(steps - 1 - t, n_seq, stride=pitch), :] = g
        for b in range(half):
            of_ref[b, :, ks] = ya_ref[k, b * pitch:b * pitch + steps, :]
            or_ref[b, :, ks] = yb_ref[k, (half + b) * pitch:(half + b) * pitch + steps, :]

    for k in range(nblk):
        project_in(k)
    recur(0)
    for k in range(1, nblk):
        recur(k)
        read_out(k - 1)
    read_out(nblk - 1)


def _s5_scan(pf, bmat, cmat, lam, *, n_batch, seq, ctx, width):
    steps = S5_STEPS
    nblk = width // LANES
    rows = steps * SUBLANES
    sw2 = bmat.shape[-1]
    ncc = ctx // steps
    nlc = seq // steps
    ctx0 = n_batch * nlc

    def fwd_map(b):
        return lambda c: (jnp.where(c < ncc, ctx0 + b * ncc + c, b * nlc + (c - ncc)), 0)

    def rev_map(b):
        return lambda c: (jnp.where(c < ncc, ctx0 + b * ncc + (ncc - 1 - c), b * nlc + (nlc - 1 - (c - ncc))), 0)

    of_map = lambda c: (0, jnp.where(c < ncc, nlc + c, c - ncc), 0)
    or_map = lambda c: (0, jnp.where(c < ncc, nlc + (ncc - 1 - c), nlc - 1 - (c - ncc)), 0)
    u_specs = ([pl.BlockSpec((steps, width), fwd_map(b)) for b in range(n_batch)]
               + [pl.BlockSpec((steps, width), rev_map(b)) for b in range(n_batch)])
    out_sds = jax.ShapeDtypeStruct((n_batch, seq + ctx, width), F32)
    stage = pltpu.VMEM((nblk, SUBLANES * S5_PITCH, LANES), F32)
    return pl.pallas_call(
        _s5_scan_kernel,
        out_shape=(out_sds, out_sds),
        grid=(ncc + nlc,),
        in_specs=u_specs + [_const_spec(bmat.shape), _const_spec(cmat.shape), _const_spec(lam.shape)],
        out_specs=(pl.BlockSpec((n_batch, steps, width), of_map),
                   pl.BlockSpec((n_batch, steps, width), or_map)),
        scratch_shapes=[pltpu.VMEM((nblk, SUBLANES, sw2), F32),
                        stage, pltpu.VMEM((2, rows, 2 * LANES), F32),
                        pltpu.VMEM((nblk, rows, sw2), F32), pltpu.VMEM((2, rows, sw2), F32),
                        pltpu.VMEM((2, rows, LANES), F32), stage, stage],
        compiler_params=_cparams(("arbitrary",)),
        name="s5_scan",
    )(*([pf] * (2 * n_batch)), bmat, cmat, lam)


def _s5_post_kernel(pf_ref, yf_ref, yr_ref, d_ref, w_ref, b_ref, o_ref, *, width):
    u = pf_ref[:, 0:width]
    y = d_ref[...] * u + yf_ref[0] + yr_ref[0]
    k0 = math.sqrt(2.0 / math.pi)
    g = 0.5 * y * (1.0 + jnp.tanh(k0 * (y + 0.044715 * (y * y * y))))
    z = _dot(g.astype(BF16), w_ref[...]) + b_ref[...]
    o_ref[...] = (g * jax.nn.sigmoid(z)).astype(BF16)


def _s5_post(pf, yf, yr, d_skip, glu_w, glu_b, *, n_batch, seq, ctx, need_ctx):
    width = yf.shape[2]
    kern = functools.partial(_s5_post_kernel, width=width)
    consts = [_const_spec((1, width)), _const_spec(glu_w.shape), _const_spec((1, width))]

    def call(tp, n_tiles, row_block0, seq_block0, n_out, name):
        return pl.pallas_call(
            kern,
            out_shape=jax.ShapeDtypeStruct((n_out, width), BF16),
            grid=(n_batch, n_tiles),
            in_specs=[pl.BlockSpec((tp, width), lambda b, j: (row_block0 + b * n_tiles + j, 0)),
                      pl.BlockSpec((1, tp, width), lambda b, j: (b, seq_block0 + j, 0)),
                      pl.BlockSpec((1, tp, width), lambda b, j: (b, seq_block0 + j, 0))] + consts,
            out_specs=pl.BlockSpec((tp, width), lambda b, j: (b * n_tiles + j, 0)),
            compiler_params=_cparams(("arbitrary", "arbitrary")),
            name=name,
        )(pf, yf, yr, d_skip, glu_w, glu_b)

    tp = min(S5_POST_ROWS, seq)
    y_lat = call(tp, seq // tp, 0, 0, n_batch * seq, "s5_post")
    if not need_ctx:
        return y_lat
    return y_lat, call(ctx, 1, n_batch * seq // ctx, seq // ctx, n_batch * ctx, "s5_post_ctx")


def _merge_kernel(x_ref, mod_ref, nw_ref, *rest, has_tail, n_first):
    y_refs = rest[:N_BRANCH + sum(has_tail)]
    wg_ref, wb_ref, wo_ref, o_ref, acc_ref = rest[len(y_refs):]
    x = x_ref[...]
    d = x.shape[1]
    m = mod_ref[0]
    n = _rms_mod(x, nw_ref[...], m[3:4], m[4:5]).astype(BF16)
    ys = []
    pos = 0
    for b in range(N_BRANCH):
        y = y_refs[pos][...]
        pos += 1
        if has_tail[b]:
            y = jnp.where(pl.program_id(0) < n_first, y, y_refs[pos][...])
            pos += 1
        ys.append(y)
    for b in range(N_BRANCH):
        gate = jax.nn.sigmoid(_dot(n, wg_ref[:, b * d:(b + 1) * d]))
        contrib = gate * _dot(ys[b], wb_ref[b])
        if b == 0:
            acc_ref[...] = contrib
        else:
            acc_ref[...] += contrib
    o_ref[...] = x + m[5:6] * _dot(acc_ref[...].astype(BF16), wo_ref[...])


def _merge(h, n_rows, mod, norm_w, ys, wg, wb, wo, *, n_lat_tiles, tiles_per_batch, n_batch):
    d = h.shape[1]
    bw = wb.shape[1]
    n_first = n_lat_tiles
    has_tail = tuple(isinstance(y, tuple) for y in ys)
    y_specs, y_ops = [], []
    for y in ys:
        if isinstance(y, tuple):
            y_specs += [pl.BlockSpec((TM, bw), lambda i: (jnp.minimum(i, n_first - 1), 0)),
                        pl.BlockSpec((TM, bw), lambda i: (jnp.maximum(i - n_first, 0), 0))]
            y_ops += list(y)
        else:
            y_specs.append(pl.BlockSpec((TM, bw), lambda i: (i, 0)))
            y_ops.append(y)
    kern = functools.partial(_merge_kernel, has_tail=has_tail, n_first=n_first)
    return pl.pallas_call(
        kern,
        out_shape=jax.ShapeDtypeStruct((n_rows, d), F32),
        grid=(n_rows // TM,),
        in_specs=[pl.BlockSpec((TM, d), lambda i: (i, 0)),
                  pl.BlockSpec((1, N_MOD, d), lambda i: (_mod_row_map(n_lat_tiles, tiles_per_batch, n_batch)(i), 0, 0)),
                  _const_spec((1, d))]
                 + y_specs
                 + [_const_spec(wg.shape), _const_spec(wb.shape), _const_spec(wo.shape)],
        out_specs=pl.BlockSpec((TM, d), lambda i: (i, 0)),
        scratch_shapes=[pltpu.VMEM((TM, d), F32)],
        compiler_params=_cparams(("arbitrary",)),
        name="merge",
    )(h, mod, norm_w, *y_ops, wg, wb, wo)


def _swap_rot_pairs(w, nf):
    lead = w.shape[:-1]
    n = w.shape[-1]
    return w.reshape(lead + (n // (2 * nf), 2, nf))[..., ::-1, :].reshape(lead + (n,))


def _rope_tables(seq, dim, lane_off, width, period):
    nf = dim // 4
    pos = jnp.arange(seq)
    rows = (pos // GRID_W).astype(F32)
    cols = (pos % GRID_W).astype(F32)
    inv_freq = ROPE_BASE ** (-jnp.arange(nf, dtype=F32) / nf)
    ang_r = rows[:, None] * inv_freq[None, :]
    ang_c = cols[:, None] * inv_freq[None, :]
    cos = jnp.concatenate([jnp.cos(ang_r)] * 2 + [jnp.cos(ang_c)] * 2, axis=1)
    sin = jnp.concatenate([-jnp.sin(ang_r), jnp.sin(ang_r), -jnp.sin(ang_c), jnp.sin(ang_c)], axis=1)
    c_per = jnp.ones((seq, period), F32).at[:, lane_off:lane_off + dim].set(cos)
    s_per = jnp.zeros((seq, period), F32).at[:, lane_off:lane_off + dim].set(sin)
    reps = width // period
    c_tab = jnp.concatenate([jnp.tile(c_per, (1, reps)), jnp.ones((TM, width), F32)], axis=0)
    s_tab = jnp.concatenate([jnp.tile(s_per, (1, reps)), jnp.zeros((TM, width), F32)], axis=0)
    return c_tab, s_tab


def _inproj_weights(w_in):
    d = w_in.shape[0]
    hw = NA_HEADS * HEAD_DIM
    o_sq = 3 * hw
    o_sk = o_sq + SWA_HEADS * HEAD_DIM
    o_sv = o_sk + SWA_KV_HEADS * HEAD_DIM
    o_s5 = o_sv + SWA_KV_HEADS * HEAD_DIM
    s5w = 512
    o_cq = o_s5 + s5w
    o_ckv = o_cq + 256
    o_kr = o_ckv + 128
    o_g = o_kr + MLA_ROPE
    qscale = LOG2E * HEAD_DIM ** -0.5
    na = jnp.concatenate([w_in[:, :hw] * qscale, w_in[:, hw:3 * hw]], axis=1)
    sq = w_in[:, o_sq:o_sk] * qscale
    sk = w_in[:, o_sk:o_sv]
    sv = w_in[:, o_sv:o_s5]
    def dup_heads(a):
        return jnp.concatenate([a[:, kv * HEAD_DIM:(kv + 1) * HEAD_DIM]
                                for kv in range(SWA_KV_HEADS) for _ in range(2)], axis=1)

    sk_dup = dup_heads(sk)
    sv_dup = dup_heads(sv)
    wr = jnp.concatenate([sq, sk_dup], axis=1)
    kr = w_in[:, o_kr:o_g]
    lpad = jnp.zeros((d, MLA_NOPE), F32)
    rpad = jnp.zeros((d, LANES - MLA_NOPE - MLA_ROPE), F32)
    wf = jnp.concatenate([w_in[:, o_s5:o_kr], lpad, kr, rpad], axis=1)
    wg = w_in[:, o_g:]
    proj = tuple(a.astype(BF16) for a in (na, wr, sv_dup, wf))
    return proj, wg.astype(BF16)


def _mla_weights(w_uq, w_ukv):
    ql = w_uq.shape[0]
    kvl = w_ukv.shape[0]
    dq = MLA_NOPE + MLA_ROPE
    wq3 = w_uq.reshape(ql, MLA_HEADS, dq)
    pad = jnp.zeros((ql, MLA_HEADS, LANES - dq), F32)
    qscale = math.log2(math.e) * dq ** -0.5
    wq = jnp.concatenate([wq3 * qscale, pad], axis=2).reshape(ql, MLA_HEADS * LANES)
    wkv3 = w_ukv.reshape(kvl, MLA_HEADS, MLA_NOPE + MLA_V)
    wk = jnp.concatenate([wkv3[:, :, :MLA_NOPE], jnp.zeros((kvl, MLA_HEADS, LANES - MLA_NOPE), F32)], axis=2)
    wk = wk.reshape(kvl, MLA_HEADS * LANES)
    wv = jnp.concatenate([wkv3[:, :, MLA_NOPE:], jnp.zeros((kvl, MLA_HEADS, LANES - MLA_V), F32)], axis=2)
    wv = wv.reshape(kvl, MLA_HEADS * LANES)
    rope_sw = _swap_rot_pairs(wq3[:, :, MLA_NOPE:], MLA_ROPE // 4) * qscale
    wqs = jnp.concatenate([jnp.zeros((ql, MLA_HEADS, MLA_NOPE), F32), rope_sw, pad], axis=2)
    wqs = wqs.reshape(ql, MLA_HEADS * LANES)
    return tuple(a.astype(BF16) for a in (wq, wqs, wk, wv))


def _s5_params(lam_re, lam_im, log_dt, b_re, b_im, c_re, c_im):
    a = lam_re.astype(F32)
    w = lam_im.astype(F32)
    dt = jnp.exp(log_dt.astype(F32))[..., None]
    mag = jnp.exp(a * dt)
    lb_re = mag * jnp.cos(w * dt)
    lb_im = mag * jnp.sin(w * dt)
    den = a * a + w * w
    cf_re = ((lb_re - 1.0) * a + lb_im * w) / den
    cf_im = (lb_im * a - (lb_re - 1.0) * w) / den
    bb_re = cf_re[..., None] * b_re - cf_im[..., None] * b_im
    bb_im = cf_re[..., None] * b_im + cf_im[..., None] * b_re
    n_dir, g, p, cg = b_re.shape
    gpb = LANES // cg
    nblk = g // gpb
    eye = jnp.eye(gpb, dtype=F32)

    def in_map(x):
        x5 = jnp.swapaxes(x.reshape(n_dir, nblk, gpb, p, cg), 3, 4)
        full = x5[:, :, :, :, None, :] * eye[None, None, :, None, :, None]
        return full.reshape(n_dir, nblk, gpb * cg, gpb * p)

    def out_map(x):
        x5 = jnp.swapaxes(x.reshape(n_dir, nblk, gpb, cg, p), 3, 4)
        full = x5[:, :, :, :, None, :] * eye[None, None, :, None, :, None]
        return full.reshape(n_dir, nblk, gpb * p, gpb * cg)

    b_in = jnp.concatenate([in_map(bb_re), in_map(bb_im)], axis=3)
    bmat = jnp.concatenate([b_in[0], b_in[1]], axis=1).astype(BF16)
    c_out = jnp.concatenate([out_map(c_re.astype(F32)), out_map(-c_im.astype(F32))], axis=2)
    cmat = jnp.concatenate([c_out[0], c_out[1]], axis=2).astype(BF16)
    half = SUBLANES // 2
    lam2 = jnp.concatenate([lb_re.reshape(n_dir, nblk, gpb * p), lb_im.reshape(n_dir, nblk, gpb * p)], axis=2)
    lam = jnp.concatenate([jnp.broadcast_to(lam2[0][:, None, :], (nblk, half, 2 * gpb * p)),
                           jnp.broadcast_to(lam2[1][:, None, :], (nblk, half, 2 * gpb * p))], axis=1)
    return bmat, cmat, lam


def _ffn_weights(wg, wu, wd):
    d, ff = wg.shape
    wg3 = wg.astype(BF16)
    wu3 = wu.astype(BF16)
    wd3 = wd.astype(BF16)
    return wg3, wu3, wd3


def kernel(x, c, ctx, c_ctx, ada_w, ada_b, ffn1_norm, ffn1_w_gate, ffn1_w_up, ffn1_w_down, mix_norm, w_in, na_rpb, swa_sink, s5_lambda_re, s5_lambda_im, s5_log_dt, s5_b_re, s5_b_im, s5_c_re, s5_c_im, s5_d, s5_glu_w, s5_glu_b, mla_q_norm, mla_w_uq, mla_kv_norm, mla_w_ukv, w_branch, w_out, ffn2_norm, ffn2_w_gate, ffn2_w_up, ffn2_w_down, final_norm):
    n_batch, seq, d = x.shape
    n_ctx = ctx.shape[1]
    depth = ada_w.shape[0]
    assert 2 * n_batch == SUBLANES and seq % TM == 0 and (n_batch * n_ctx) % TM == 0
    n_lat = n_batch * seq
    n_all = n_lat + n_batch * n_ctx
    tiles_per_batch = seq // TM
    n_lat_tiles = n_lat // TM
    geo = dict(n_lat_tiles=n_lat_tiles, tiles_per_batch=tiles_per_batch, n_batch=n_batch)

    h = x.reshape(n_lat, d)
    h_ctx = ctx.reshape(n_batch * n_ctx, d)
    cc = jnp.concatenate([c, c_ctx[None, :], jnp.zeros((SUBLANES - n_batch - 1, d), F32)], axis=0)
    mod = _ada_mod(cc, ada_w, ada_b)

    cs_sw, sn_sw = _rope_tables(seq, HEAD_DIM, 0, 2 * LANES, HEAD_DIM)
    ck_kr, sk_kr = _rope_tables(seq, MLA_ROPE, LANES + MLA_NOPE, 2 * LANES, 2 * LANES)
    cq_ml, sq_ml = _rope_tables(seq, MLA_ROPE, MLA_NOPE, 2 * LANES, LANES)
    rows_n = seq // GRID_W
    s5w = s5_d.shape[1]
    sw_col0 = 3 * NA_HEADS * HEAD_DIM

    w_ffn1 = _ffn_weights(ffn1_w_gate[0], ffn1_w_up[0], ffn1_w_down[0])
    for l in range(depth):
        need_ctx = l < depth - 1
        last = l == depth - 1
        ml = mod[l]
        ones = jnp.ones((1, d), F32)
        h, *w_ffn2 = _ffn(h, n_all, ml, ffn1_norm[l][None, :], *w_ffn1, ones, base=0, final=False,
                          h_tail=h_ctx if l == 0 else None,
                          cast_next=(l, ffn2_w_gate, ffn2_w_up, ffn2_w_down), **geo)
        proj_w, gate_w = _inproj_weights(w_in[l])
        mla_w = _mla_weights(mla_w_uq[l], mla_w_ukv[l])
        pb, pf, qm, km, vm = _inproj(h, ml, mix_norm[l][None, :], proj_w, (cs_sw, sn_sw, ck_kr, sk_kr),
                                     (mla_q_norm[l][None, :], mla_kv_norm[l][None, :], *mla_w, cq_ml, sq_ml), **geo)
        bias = _na_bias_table(na_rpb[l].astype(F32), rows_n)
        y_na = _na_attention(pb, bias, n_batch=n_batch, seq=seq, ctx=n_ctx, need_ctx=need_ctx)
        y_sw_l, y_sw_c = _swa_attention(pb, swa_sink[l].astype(F32) * LOG2E, n_batch=n_batch, seq=seq, ctx=n_ctx,
                                        need_ctx=need_ctx, col0=sw_col0)
        y_sw = (y_sw_l, y_sw_c) if need_ctx else y_sw_l
        y_mla = _mla_attention(qm, km, vm, n_batch=n_batch, seq=seq, ctx=n_ctx, need_ctx=need_ctx)
        s5p = _s5_params(s5_lambda_re[l], s5_lambda_im[l], s5_log_dt[l], s5_b_re[l], s5_b_im[l],
                         s5_c_re[l], s5_c_im[l])
        yf, yr = _s5_scan(pf, *s5p, n_batch=n_batch, seq=seq, ctx=n_ctx, width=s5w)
        y_s5 = _s5_post(pf, yf, yr, s5_d[l][None, :].astype(F32), s5_glu_w[l].astype(BF16),
                        s5_glu_b[l][None, :].astype(F32), n_batch=n_batch, seq=seq, ctx=n_ctx, need_ctx=need_ctx)
        n_rows = n_all if need_ctx else n_lat
        h = _merge(h, n_rows, ml, mix_norm[l][None, :], (y_na, y_sw, y_s5, y_mla), gate_w,
                   w_branch[l].astype(BF16), w_out[l].astype(BF16), **geo)
        if last:
            h = _ffn(h, n_rows, ml, ffn2_norm[l][None, :], *w_ffn2, final_norm[None, :], base=6, final=True, **geo)
        else:
            h, *w_ffn1 = _ffn(h, n_rows, ml, ffn2_norm[l][None, :], *w_ffn2, final_norm[None, :], base=6, final=False,
                              cast_next=(l + 1, ffn1_w_gate, ffn1_w_up, ffn1_w_down), **geo)
    return h.reshape(n_batch, seq, d)
```

```python
import functools
import math

import numpy as np
import jax
import jax.numpy as jnp
from jax import lax
from jax.experimental import pallas as pl
from jax.experimental.pallas import tpu as pltpu

F32 = jnp.float32
BF16 = jnp.bfloat16

GRID_W = 64
HEAD_DIM = 64
N_BRANCH = 4
NA_HEADS = 8
NA_WIN_ROWS = 8
NA_WIN_COLS = 16
SWA_HEADS = 8
SWA_KV_HEADS = 2
SWA_WINDOW = 128
S5_GROUP = 16
S5_STATE = 64
MLA_HEADS = 8
MLA_NOPE = 64
MLA_ROPE = 32
MLA_V = 64
MACARON_WEIGHT = 0.5
ROPE_BASE = 10000.0
EPS = 1e-6
N_MOD = 9

LANES = 128
SUBLANES = 8
TM = 512
FF_CHUNK = 256
NA_QROWS = 4
NA_KROWS = NA_QROWS + NA_WIN_ROWS - 1
SWA_BLK = 128
SWA_UNROLL = 2
MLA_TQ = 512
MLA_SUB = 256
MLA_AHEAD = 1
S5_STEPS = 128
S5_POST_ROWS = 1024
S5_PITCH = S5_STEPS + SUBLANES
NEG = -1e30
LOG2E = math.log2(math.e)
VMEM_LIMIT = 56 * 1024 * 1024


def _cparams(sem, flags=None):
    return pltpu.CompilerParams(dimension_semantics=sem, vmem_limit_bytes=VMEM_LIMIT, flags=flags)


def _const_spec(shape):
    nd = len(shape)
    return pl.BlockSpec(shape, lambda *_: (0,) * nd, pipeline_mode=pl.Buffered(1))


def _dot(a, b):
    return jnp.dot(a, b, preferred_element_type=F32)


def _dot_t(a, b):
    return lax.dot_general(a, b, (((1,), (1,)), ((), ())), preferred_element_type=F32)


def _rms(x, w):
    return x * lax.rsqrt(jnp.mean(x * x, axis=-1, keepdims=True) + EPS) * w


def _rms_mod(x, w, shift, scale):
    return _rms(x, w) * (1.0 + scale) + shift


def _mod_row_map(n_lat_tiles, tiles_per_batch, n_batch):
    def f(i):
        return jnp.where(i < n_lat_tiles, i // tiles_per_batch, n_batch)
    return f


def _ada_kernel(c_ref, w_ref, b_ref, o_ref):
    c = c_ref[...]
    s = c * jax.nn.sigmoid(c)
    w = w_ref[0]
    s_hi = s.astype(BF16)
    s_lo = (s - s_hi.astype(F32)).astype(BF16)
    w_hi = w.astype(BF16)
    w_lo = (w - w_hi.astype(F32)).astype(BF16)
    o_ref[0] = _dot(s_hi, w_hi) + (_dot(s_hi, w_lo) + _dot(s_lo, w_hi)) + b_ref[0]


def _ada_mod(cc, ada_w, ada_b):
    depth, d, nd = ada_w.shape
    tn = 1024
    out = pl.pallas_call(
        _ada_kernel,
        out_shape=jax.ShapeDtypeStruct((depth, SUBLANES, nd), F32),
        grid=(depth, nd // tn),
        in_specs=[pl.BlockSpec((SUBLANES, d), lambda l, j: (0, 0)),
                  pl.BlockSpec((1, d, tn), lambda l, j: (l, 0, j)),
                  pl.BlockSpec((1, 1, tn), lambda l, j: (l, 0, j))],
        out_specs=pl.BlockSpec((1, SUBLANES, tn), lambda l, j: (l, 0, j)),
        compiler_params=_cparams(("arbitrary", "arbitrary")),
        name="ada_mod",
    )(cc, ada_w, ada_b.reshape(depth, 1, nd))
    return out.reshape(depth, SUBLANES, N_MOD, d)


def _ffn_kernel(x_ref, xc_ref, mod_ref, nw_ref, wg_ref, wu_ref, wd_ref, fw_ref, *rest, base, final, n_first):
    if len(rest) == 1:
        (o_ref,) = rest
    else:
        ng_ref, nu_ref, nd_ref, o_ref, cg_ref, cu_ref, cd_ref = rest
        cg_ref[...] = ng_ref[...].astype(BF16)
        cu_ref[...] = nu_ref[...].astype(BF16)
        cd_ref[...] = nd_ref[...].astype(BF16)
    x = x_ref[...]
    if n_first is not None:
        x = jnp.where(pl.program_id(0) < n_first, x, xc_ref[...])
    m = mod_ref[0]
    n = _rms_mod(x, nw_ref[...], m[base:base + 1], m[base + 1:base + 2]).astype(BF16)
    ff = wg_ref.shape[1]
    fc = FF_CHUNK if ff % FF_CHUNK == 0 else ff
    acc = None
    for c in range(ff // fc):
        cs = slice(c * fc, (c + 1) * fc)
        g = _dot(n, wg_ref[:, cs])
        u = _dot(n, wu_ref[:, cs])
        a = (g * jax.nn.sigmoid(g) * u).astype(BF16)
        y = _dot(a, wd_ref[cs, :])
        acc = y if acc is None else acc + y
    out = x + MACARON_WEIGHT * m[base + 2:base + 3] * acc
    if final:
        out = _rms(out, fw_ref[...])
    o_ref[...] = out


def _ffn(h, n_rows, mod, norm_w, wg, wu, wd, final_w, *, base, final, n_lat_tiles, tiles_per_batch, n_batch,
         h_tail=None, cast_next=None):
    d = h.shape[1]
    ff = wg.shape[1]
    steps = n_rows // TM
    if h_tail is None:
        n_first = None
        h_tail = h
        x_map = lambda i: (i, 0)
        t_map = lambda i: (0, 0)
    else:
        n_first = h.shape[0] // TM
        x_map = lambda i: (jnp.minimum(i, n_first - 1), 0)
        t_map = lambda i: (jnp.maximum(i - n_first, 0), 0)
    kern = functools.partial(_ffn_kernel, base=base, final=final, n_first=n_first)
    in_specs = [pl.BlockSpec((TM, d), x_map),
                pl.BlockSpec((TM, d) if n_first is not None else (SUBLANES, d), t_map),
                pl.BlockSpec((1, N_MOD, d), lambda i: (_mod_row_map(n_lat_tiles, tiles_per_batch, n_batch)(i), 0, 0)),
                _const_spec((1, d)),
                _const_spec((d, ff)), _const_spec((d, ff)), _const_spec((ff, d)),
                _const_spec((1, d))]
    out_shape = jax.ShapeDtypeStruct((n_rows, d), F32)
    out_specs = pl.BlockSpec((TM, d), lambda i: (i, 0))
    operands = [h, h_tail, mod, norm_w, wg, wu, wd, final_w]
    if cast_next is not None:
        up_rows = min(r for r in range(2 * SUBLANES, d + 1, 2 * SUBLANES) if d % r == 0 and d // r <= steps)
        dn_rows = min(r for r in range(2 * SUBLANES, ff + 1, 2 * SUBLANES) if ff % r == 0 and ff // r <= steps)
        n_up, n_dn = d // up_rows, ff // dn_rows
        up_spec = pl.BlockSpec((up_rows, ff), lambda i: (jnp.minimum(i, n_up - 1), 0))
        dn_spec = pl.BlockSpec((dn_rows, d), lambda i: (jnp.minimum(i, n_dn - 1), 0))
        nl, *next_w = cast_next
        in_specs += [pl.BlockSpec((None, up_rows, ff), lambda i: (nl, jnp.minimum(i, n_up - 1), 0))] * 2
        in_specs += [pl.BlockSpec((None, dn_rows, d), lambda i: (nl, jnp.minimum(i, n_dn - 1), 0))]
        operands += next_w
        out_shape = (out_shape, jax.ShapeDtypeStruct((d, ff), BF16), jax.ShapeDtypeStruct((d, ff), BF16),
                     jax.ShapeDtypeStruct((ff, d), BF16))
        out_specs = (out_specs, up_spec, up_spec, dn_spec)
    return pl.pallas_call(
        kern,
        out_shape=out_shape,
        grid=(steps,),
        in_specs=in_specs,
        out_specs=out_specs,
        compiler_params=_cparams(("arbitrary",)),
        name="ffn_final" if final else "ffn",
    )(*operands)


def _rope_apply(y, cs, sn, nf):
    w = y.shape[1]
    first = (lax.broadcasted_iota(jnp.int32, y.shape, 1) & nf) == 0
    ysw = jnp.where(first, pltpu.roll(y, w - nf, 1), pltpu.roll(y, nf, 1))
    return y * cs + ysw * sn


def _inproj_kernel(x_ref, mod_ref, nw_ref, wa_ref, wr_ref, wv_ref, wf_ref,
                   cs_ref, sn_ref, ck_ref, sk_ref,
                   qn_ref, kn_ref, wq_ref, wqs_ref, wk_ref, wvm_ref, cq_ref, sq_ref,
                   ob_ref, of_ref, q_ref, k_ref, v_ref):
    x = x_ref[...]
    m = mod_ref[0]
    n = _rms_mod(x, nw_ref[...], m[3:4], m[4:5]).astype(BF16)
    na = wa_ref.shape[1]
    nr = wr_ref.shape[1]
    nv = wv_ref.shape[1]
    nf = wf_ref.shape[1]
    cw = 2 * LANES
    cq = _dot(n, wf_ref[:, nf - 2 * cw:nf - cw])
    y = _rope_apply(_dot(n, wf_ref[:, nf - cw:nf]), ck_ref[...], sk_ref[...], MLA_ROPE // 4)
    _mla_project(cq, y[:, :LANES], y[:, LANES:], qn_ref, kn_ref, wq_ref, wqs_ref, wk_ref, wvm_ref,
                 cq_ref, sq_ref, q_ref, k_ref, v_ref)
    for c in range(nf // cw - 2):
        of_ref[:, c * cw:(c + 1) * cw] = _dot(n, wf_ref[:, c * cw:(c + 1) * cw])
    for c in range(na // cw):
        ob_ref[:, c * cw:(c + 1) * cw] = _dot(n, wa_ref[:, c * cw:(c + 1) * cw]).astype(BF16)
    for c in range(nr // cw):
        y = _dot(n, wr_ref[:, c * cw:(c + 1) * cw])
        ob_ref[:, na + c * cw:na + (c + 1) * cw] = _rope_apply(y, cs_ref[...], sn_ref[...], HEAD_DIM // 4).astype(BF16)
    for c in range(nv // cw):
        ob_ref[:, na + nr + c * cw:na + nr + (c + 1) * cw] = _dot(n, wv_ref[:, c * cw:(c + 1) * cw]).astype(BF16)


def _inproj(h, mod, norm_w, w, tabs, mla, *, n_lat_tiles, tiles_per_batch, n_batch):
    n_rows, d = h.shape
    wa, wr, wv, wf = w
    cs, sn, ck, sk = tabs
    qn, kn, wq, wqs, wk, wvm, cq_tab, sq_tab = mla
    cw = 2 * LANES
    assert wq.shape[0] == cw and wk.shape[0] == LANES
    nb = wa.shape[1] + wr.shape[1] + wv.shape[1]
    ns5 = wf.shape[1] - 2 * cw
    hq = MLA_HEADS * LANES
    tab_map = lambda i: (jnp.where(i < n_lat_tiles, i % tiles_per_batch, tiles_per_batch), 0)
    row = lambda width: pl.BlockSpec((TM, width), lambda i: (i, 0))
    return pl.pallas_call(
        _inproj_kernel,
        out_shape=(jax.ShapeDtypeStruct((n_rows, nb), BF16), jax.ShapeDtypeStruct((n_rows, ns5), F32))
                  + (jax.ShapeDtypeStruct((n_rows, hq), BF16),) * 3,
        grid=(n_rows // TM,),
        in_specs=[pl.BlockSpec((TM, d), lambda i: (i, 0)),
                  pl.BlockSpec((1, N_MOD, d), lambda i: (_mod_row_map(n_lat_tiles, tiles_per_batch, n_batch)(i), 0, 0)),
                  _const_spec((1, d))]
                 + [_const_spec(a.shape) for a in w]
                 + [pl.BlockSpec((TM, cw), tab_map)] * 4
                 + [_const_spec(a.shape) for a in (qn, kn, wq, wqs, wk, wvm)]
                 + [pl.BlockSpec((TM, cw), tab_map)] * 2,
        out_specs=(row(nb), row(ns5), row(hq), row(hq), row(hq)),
        compiler_params=_cparams(("arbitrary",)),
        name="inproj",
    )(h, mod, norm_w, *w, cs, sn, ck, sk, qn, kn, wq, wqs, wk, wvm, cq_tab, sq_tab)


def _lane_half_masks(rows):
    lane = lax.broadcasted_iota(jnp.int32, (rows, LANES), 1)
    return lane < HEAD_DIM


def _with_ones(v):
    return jnp.concatenate([v, jnp.ones_like(v)], axis=1)


def _na_kernel(q_ref, k_ref, v_ref, kc_ref, vc_ref, bias_ref, o_ref, *, n_blk, rows_n):
    blk = pl.program_id(1)
    nq = NA_QROWS * GRID_W
    nk = NA_KROWS * GRID_W
    ws = jnp.clip(NA_QROWS * blk - NA_WIN_ROWS // 2, 0, rows_n - NA_KROWS)
    r0 = pl.multiple_of(ws * GRID_W, GRID_W)
    lo = _lane_half_masks(nq)
    n_pairs = NA_HEADS // 2

    def scores(j):
        sl = slice(j * LANES, (j + 1) * LANES)
        qp = q_ref[:, sl]
        zero = jnp.zeros_like(qp)
        qs = jnp.concatenate([jnp.where(lo, qp, zero), jnp.where(lo, zero, qp)], axis=0)
        return _dot_t(qs, k_ref[pl.ds(r0, nk), sl]) + bias_ref[0, j], _dot_t(qs, kc_ref[:, sl])

    def finish(j, s1, s2):
        sl = slice(j * LANES, (j + 1) * LANES)
        m = jnp.maximum(jnp.max(s1, axis=-1, keepdims=True), jnp.max(s2, axis=-1, keepdims=True))
        o2 = (_dot(jnp.exp2(s1 - m).astype(BF16), _with_ones(v_ref[pl.ds(r0, nk), sl]))
              + _dot(jnp.exp2(s2 - m).astype(BF16), _with_ones(vc_ref[:, sl])))
        o = o2[:, :LANES] / o2[:, LANES:]
        o_ref[:, sl] = jnp.where(lo, o[:nq], o[nq:]).astype(BF16)

    pending = scores(0)
    for j in range(n_pairs):
        following = scores(j + 1) if j + 1 < n_pairs else None
        finish(j, *pending)
        pending = following


def _na_attention(pb, bias, *, n_batch, seq, ctx, need_ctx):
    rows_n = seq // GRID_W
    n_blk = rows_n // NA_QROWS
    nq = NA_QROWS * GRID_W
    hw = NA_HEADS * HEAD_DIM
    n_q = n_blk + (1 if need_ctx else 0)
    assert ctx == nq
    lat_blocks = n_batch * seq // nq
    qmap = lambda b, i: (jnp.where(i < n_blk, b * n_blk + i, lat_blocks + b), 0)

    def bias_map(b, i):
        t = jnp.where(i == 0, 0, jnp.where(i == 1, 1, jnp.where(i == n_blk - 1, 3, jnp.where(i == n_blk, 4, 2))))
        return (t, 0, 0, 0)

    kern = functools.partial(_na_kernel, n_blk=n_blk, rows_n=rows_n)
    return pl.pallas_call(
        kern,
        out_shape=jax.ShapeDtypeStruct((n_batch * n_q * nq, hw), BF16),
        grid=(n_batch, n_q),
        in_specs=[pl.BlockSpec((nq, hw), qmap),
                  pl.BlockSpec((seq, hw), lambda b, i: (b, 1)),
                  pl.BlockSpec((seq, hw), lambda b, i: (b, 2)),
                  pl.BlockSpec((ctx, hw), lambda b, i: (n_batch * seq // ctx + b, 1)),
                  pl.BlockSpec((ctx, hw), lambda b, i: (n_batch * seq // ctx + b, 2)),
                  pl.BlockSpec((1, NA_HEADS // 2, 2 * nq, NA_KROWS * GRID_W), bias_map)],
        out_specs=pl.BlockSpec((nq, hw), qmap),
        compiler_params=_cparams(("arbitrary", "arbitrary")),
        name="na_attn",
    )(pb, pb, pb, pb, pb, bias)


def _na_bias_table(rpb, rows_n):
    n_blk = rows_n // NA_QROWS
    kr_n = min(NA_WIN_ROWS, rows_n)
    n_heads = rpb.shape[0]
    col = np.arange(GRID_W)
    c0 = np.clip(col - NA_WIN_COLS // 2, 0, GRID_W - NA_WIN_COLS)
    col_ok = (col[None, :] >= c0[:, None]) & (col[None, :] < c0[:, None] + NA_WIN_COLS)
    n_dc = rpb.shape[2]
    skew = 2 * GRID_W
    vec = jnp.concatenate([rpb[:, :, NA_WIN_COLS - 1:],
                           jnp.zeros(rpb.shape[:2] + (skew - n_dc,), F32),
                           rpb[:, :, :NA_WIN_COLS - 1]], axis=2) * LOG2E
    tiled = jnp.tile(vec, (1, 1, GRID_W))[:, :, :GRID_W * (skew - 1)]
    toep = tiled.reshape(rpb.shape[:2] + (GRID_W, skew - 1))[:, :, :, :GRID_W]
    toep = jnp.where(col_ok[None, None], toep, NEG)
    masked = jnp.full((n_heads, 1, GRID_W, GRID_W), NEG, F32)
    ext = jnp.concatenate([masked, toep, masked], axis=1)
    pair2 = jnp.concatenate([ext[:, :-1], ext[:, 1:]], axis=-1)
    n_dr = 2 * NA_WIN_ROWS - 1
    plans = []
    for blk in (0, 1, 2, n_blk - 1):
        ws = int(np.clip(NA_QROWS * blk - NA_WIN_ROWS // 2, 0, rows_n - NA_KROWS))
        plan = []
        for qr in range(NA_QROWS):
            r = NA_QROWS * blk + qr
            r0 = int(np.clip(r - kr_n // 2, 0, rows_n - kr_n))
            row = []
            for kk in range(0, NA_KROWS, 2):
                oks = tuple(r0 <= ws + kk + i < r0 + kr_n and kk + i < NA_KROWS for i in range(2))
                row.append((ws + kk - r + NA_WIN_ROWS, oks))
            plan.append(tuple(row))
        plans.append(tuple(plan))
    plans.append(None)
    nq = NA_QROWS * GRID_W
    nk = NA_KROWS * GRID_W
    kern = functools.partial(_na_bias_kernel, plans=tuple(plans))
    return pl.pallas_call(
        kern,
        out_shape=jax.ShapeDtypeStruct((len(plans), n_heads // 2, 2 * nq, nk), F32),
        grid=(n_heads // 2,),
        in_specs=[pl.BlockSpec((2, n_dr + 1, GRID_W, 2 * GRID_W), lambda j: (j, 0, 0, 0))],
        out_specs=pl.BlockSpec((len(plans), 1, 2 * nq, nk), lambda j: (0, j, 0, 0)),
        compiler_params=_cparams(("arbitrary",)),
        name="na_bias",
    )(pair2)


def _na_bias_kernel(p2_ref, o_ref, *, plans):
    nq = NA_QROWS * GRID_W
    lo = lax.broadcasted_iota(jnp.int32, (GRID_W, 2 * GRID_W), 1) < GRID_W
    neg = jnp.full((GRID_W, 2 * GRID_W), NEG, F32)
    for t, plan in enumerate(plans):
        if plan is None:
            o_ref[t, 0] = jnp.full(o_ref.shape[2:], NEG, F32)
            continue
        for half in range(2):
            for qr, row in enumerate(plan):
                rs = slice(half * nq + qr * GRID_W, half * nq + (qr + 1) * GRID_W)
                for kp, (e, (ok_a, ok_b)) in enumerate(row):
                    width = min(2 * GRID_W, o_ref.shape[3] - kp * 2 * GRID_W)
                    if ok_a or ok_b:
                        tile = p2_ref[half, e]
                        if not ok_a:
                            tile = jnp.where(lo, neg, tile)
                        if not ok_b:
                            tile = jnp.where(lo, tile, neg)
                    else:
                        tile = neg
                    o_ref[t, 0, rs, kp * 2 * GRID_W:kp * 2 * GRID_W + width] = tile[:, :width]


def _swa_kernel(sink_ref, q_ref, k_ref, v_ref, qc_ref, kc_ref, vc_ref, o_ref, oc_ref, *, seq, need_ctx):
    n_blk = seq // SWA_BLK
    band = 3 * SWA_BLK
    group = SWA_HEADS // SWA_KV_HEADS
    pairs = group // 2

    def group_scores(q_slabs, kv, kb, kc, mask_bias):
        r = q_slabs[0].shape[0]
        lo = _lane_half_masks(r)
        parts = []
        for qp in q_slabs:
            zero = jnp.zeros_like(qp)
            parts += [jnp.where(lo, qp, zero), jnp.where(lo, zero, qp)]
        qs = jnp.concatenate(parts, axis=0)
        s1 = None if kb is None else _dot_t(qs, kb) + mask_bias
        return s1, _dot_t(qs, kc)

    def group_finish(kv, s1, s2, vb, vc):
        r = s2.shape[0] // group
        lo = _lane_half_masks(r)
        row = lax.broadcasted_iota(jnp.int32, (group * r, 1), 0)
        sink = jnp.full((group * r, 1), sink_ref[kv * group + group - 1], F32)
        for g in range(group - 2, -1, -1):
            sink = jnp.where(row < (g + 1) * r, sink_ref[kv * group + g], sink)
        if s1 is None:
            m = jnp.maximum(jnp.max(s2, axis=-1, keepdims=True), sink)
            o2 = _dot(jnp.exp2(s2 - m).astype(BF16), _with_ones(vc))
        else:
            m = jnp.maximum(jnp.maximum(jnp.max(s1, axis=-1, keepdims=True), jnp.max(s2, axis=-1, keepdims=True)), sink)
            o2 = (_dot(jnp.exp2(s1 - m).astype(BF16), _with_ones(vb))
                  + _dot(jnp.exp2(s2 - m).astype(BF16), _with_ones(vc)))
        o = o2[:, :LANES] / (o2[:, LANES:] + jnp.exp2(sink - m))
        return [jnp.where(lo, o[(2 * i) * r:(2 * i + 1) * r], o[(2 * i + 1) * r:(2 * i + 2) * r]) for i in range(pairs)]

    qi = lax.broadcasted_iota(jnp.int32, (group * SWA_BLK, band), 0) & (SWA_BLK - 1)
    ki = lax.broadcasted_iota(jnp.int32, (group * SWA_BLK, band), 1)

    def blk_body(n2, carry):
        geo = []
        for j in range(SWA_UNROLL):
            n = n2 * SWA_UNROLL + j
            start = pl.multiple_of(jnp.clip((n - 1) * SWA_BLK, 0, seq - band), SWA_BLK)
            q0 = pl.multiple_of(n * SWA_BLK, SWA_BLK)
            delta = (start + ki) - (q0 + qi)
            geo.append((start, q0, jnp.where(jnp.abs(delta) <= SWA_WINDOW, 0.0, NEG).astype(F32)))
        units = [(j, kv) for j in range(SWA_UNROLL) for kv in range(SWA_KV_HEADS)]

        def scores(j, kv):
            start, q0, mask_bias = geo[j]
            ksl = slice(kv * LANES, (kv + 1) * LANES)
            slabs = [q_ref[pl.ds(q0, SWA_BLK), (kv * pairs + i) * LANES:(kv * pairs + i + 1) * LANES]
                     for i in range(pairs)]
            return group_scores(slabs, kv, k_ref[pl.ds(start, band), ksl], kc_ref[:, ksl], mask_bias)

        def finish(j, kv, s1, s2):
            start, q0, _ = geo[j]
            ksl = slice(kv * LANES, (kv + 1) * LANES)
            outs = group_finish(kv, s1, s2, v_ref[pl.ds(start, band), ksl], vc_ref[:, ksl])
            for i in range(pairs):
                o_ref[pl.ds(q0, SWA_BLK), (kv * pairs + i) * LANES:(kv * pairs + i + 1) * LANES] = outs[i].astype(BF16)

        pending = scores(*units[0])
        for idx, u in enumerate(units):
            following = scores(*units[idx + 1]) if idx + 1 < len(units) else None
            finish(*u, *pending)
            pending = following
        return carry

    lax.fori_loop(0, n_blk // SWA_UNROLL, blk_body, 0)

    if need_ctx:
        for kv in range(SWA_KV_HEADS):
            ksl = slice(kv * LANES, (kv + 1) * LANES)
            slabs = [qc_ref[:, (kv * pairs + i) * LANES:(kv * pairs + i + 1) * LANES] for i in range(pairs)]
            _, s2 = group_scores(slabs, kv, None, kc_ref[:, ksl], None)
            outs = group_finish(kv, None, s2, None, vc_ref[:, ksl])
            for i in range(pairs):
                oc_ref[:, (kv * pairs + i) * LANES:(kv * pairs + i + 1) * LANES] = outs[i].astype(BF16)
    else:
        oc_ref[...] = jnp.zeros_like(oc_ref)


def _swa_attention(pb, sink, *, n_batch, seq, ctx, need_ctx, col0):
    hw = SWA_HEADS * HEAD_DIM
    kw = SWA_KV_HEADS * LANES
    qcol = col0 // hw
    kcol = (col0 + hw) // kw
    vcol = kcol + 1
    cblk = n_batch * seq // ctx
    kern = functools.partial(_swa_kernel, seq=seq, need_ctx=need_ctx)
    return pl.pallas_call(
        kern,
        out_shape=(jax.ShapeDtypeStruct((n_batch * seq, hw), BF16),
                   jax.ShapeDtypeStruct((n_batch * ctx, hw), BF16)),
        grid=(n_batch,),
        in_specs=[pl.BlockSpec(memory_space=pltpu.SMEM),
                  pl.BlockSpec((seq, hw), lambda b: (b, qcol)),
                  pl.BlockSpec((seq, kw), lambda b: (b, kcol)),
                  pl.BlockSpec((seq, kw), lambda b: (b, vcol)),
                  pl.BlockSpec((ctx, hw), lambda b: (cblk + b, qcol)),
                  pl.BlockSpec((ctx, kw), lambda b: (cblk + b, kcol)),
                  pl.BlockSpec((ctx, kw), lambda b: (cblk + b, vcol))],
        out_specs=(pl.BlockSpec((seq, hw), lambda b: (b, 0)),
                   pl.BlockSpec((ctx, hw), lambda b: (b, 0))),
        compiler_params=_cparams(("arbitrary",)),
        name="swa_attn",
    )(sink, pb, pb, pb, pb, pb, pb)


def _mla_project(cq, ckv, kr, qn_ref, kn_ref, wq_ref, wqs_ref, wk_ref, wv_ref, cq_ref, sq_ref, q_ref, k_ref, v_ref):
    kr2 = jnp.concatenate([kr, kr], axis=1)
    cqn = _rms(cq, qn_ref[...]).astype(BF16)
    ckvn = _rms(ckv, kn_ref[...]).astype(BF16)
    cw = 2 * LANES
    for c in range(MLA_HEADS * LANES // cw):
        sl = slice(c * cw, (c + 1) * cw)
        q = _dot(cqn, wq_ref[:, sl]) * cq_ref[...] + _dot(cqn, wqs_ref[:, sl]) * sq_ref[...]
        q_ref[:, sl] = q.astype(BF16)
        k_ref[:, sl] = (_dot(ckvn, wk_ref[:, sl]) + kr2).astype(BF16)
    ones_hi = jnp.where(_lane_half_masks(1), 0.0, 1.0).astype(F32)
    ones_hi = jnp.concatenate([ones_hi, ones_hi], axis=1)
    for c in range(MLA_HEADS * LANES // cw):
        sl = slice(c * cw, (c + 1) * cw)
        v_ref[:, sl] = (_dot(ckvn, wv_ref[:, sl]) + ones_hi).astype(BF16)


def _mla_body(q_ref, k_ref, v_ref, kc_ref, vc_ref, o_ref):
    sub = MLA_SUB
    n_sub = q_ref.shape[0] // sub
    lo = _lane_half_masks(sub)
    units = [(r, half) for r in range(n_sub) for half in range(2)]

    def scores(r, half):
        sl = slice(half * LANES, (half + 1) * LANES)
        q = q_ref[r * sub:(r + 1) * sub, sl]
        s1 = None if k_ref is None else _dot_t(q, k_ref[:, sl])
        return s1, _dot_t(q, kc_ref[:, sl])

    def finish(half, s1, s2):
        sl = slice(half * LANES, (half + 1) * LANES)
        if s1 is None:
            m = jnp.max(s2, axis=-1, keepdims=True)
            return _dot(jnp.exp2(s2 - m).astype(BF16), vc_ref[:, sl])
        m = jnp.maximum(jnp.max(s1, axis=-1, keepdims=True), jnp.max(s2, axis=-1, keepdims=True))
        p1 = jnp.exp2(s1 - m)
        p2 = jnp.exp2(s2 - m)
        return _dot(p1.astype(BF16), v_ref[:, sl]) + _dot(p2.astype(BF16), vc_ref[:, sl])

    outs = {}
    ahead = MLA_AHEAD
    sc = {u: scores(*u) for u in units[:ahead]}
    for idx, u in enumerate(units):
        if idx + ahead < len(units):
            sc[units[idx + ahead]] = scores(*units[idx + ahead])
        outs[u] = finish(u[1], *sc.pop(u))
    for r in range(n_sub):
        o0, o1 = outs[(r, 0)], outs[(r, 1)]
        r0 = pltpu.roll(o0, HEAD_DIM, 1)
        r1 = pltpu.roll(o1, HEAD_DIM, 1)
        o_ref[r * sub:(r + 1) * sub, :] = jnp.where(lo, o0 / r0, r1 / o1).astype(BF16)


def _mla_kernel(q_ref, k_ref, v_ref, kc_ref, vc_ref, o_ref):
    _mla_body(q_ref, k_ref, v_ref, kc_ref, vc_ref, o_ref)


def _mla_ctx_kernel(q_ref, kc_ref, vc_ref, o_ref):
    _mla_body(q_ref, None, None, kc_ref, vc_ref, o_ref)


def _mla_attention(qm, km, vm, *, n_batch, seq, ctx, need_ctx):
    n_qt = seq // MLA_TQ
    cblk = n_batch * seq // ctx
    hv = MLA_HEADS * MLA_V
    y_lat = pl.pallas_call(
        _mla_kernel,
        out_shape=jax.ShapeDtypeStruct((n_batch * seq, hv), BF16),
        grid=(n_batch, MLA_HEADS // 2, n_qt),
        in_specs=[pl.BlockSpec((MLA_TQ, 2 * LANES), lambda b, p, i: (b * n_qt + i, p)),
                  pl.BlockSpec((seq, 2 * LANES), lambda b, p, i: (b, p)),
                  pl.BlockSpec((seq, 2 * LANES), lambda b, p, i: (b, p)),
                  pl.BlockSpec((ctx, 2 * LANES), lambda b, p, i: (cblk + b, p)),
                  pl.BlockSpec((ctx, 2 * LANES), lambda b, p, i: (cblk + b, p))],
        out_specs=pl.BlockSpec((MLA_TQ, LANES), lambda b, p, i: (b * n_qt + i, p)),
        compiler_params=_cparams(("arbitrary", "arbitrary", "arbitrary")),
        name="mla_attn",
    )(qm, km, vm, km, vm)
    if not need_ctx:
        return y_lat
    assert ctx % MLA_SUB == 0
    y_ctx = pl.pallas_call(
        _mla_ctx_kernel,
        out_shape=jax.ShapeDtypeStruct((n_batch * ctx, hv), BF16),
        grid=(n_batch, MLA_HEADS // 2),
        in_specs=[pl.BlockSpec((ctx, 2 * LANES), lambda b, p: (cblk + b, p)),
                  pl.BlockSpec((ctx, 2 * LANES), lambda b, p: (cblk + b, p)),
                  pl.BlockSpec((ctx, 2 * LANES), lambda b, p: (cblk + b, p))],
        out_specs=pl.BlockSpec((ctx, LANES), lambda b, p: (b, p)),
        compiler_params=_cparams(("arbitrary", "arbitrary")),
        name="mla_ctx_attn",
    )(qm, km, vm)
    return y_lat, y_ctx


def _s5_scan_kernel(uf0_ref, uf1_ref, uf2_ref, uf3_ref, ur0_ref, ur1_ref, ur2_ref, ur3_ref,
                    bmat_ref, cmat_ref, lam_ref, of_ref, or_ref,
                    h_ref, stage_ref, lhs_ref, bu_ref, st_ref, ysc_ref, ya_ref, yb_ref):
    c = pl.program_id(0)
    n_seq = SUBLANES
    half = n_seq // 2
    steps = S5_STEPS
    pitch = S5_PITCH
    rows = steps * n_seq
    nblk = h_ref.shape[0]
    sw = h_ref.shape[2] // 2
    ufs = (uf0_ref, uf1_ref, uf2_ref, uf3_ref)
    urs = (ur0_ref, ur1_ref, ur2_ref, ur3_ref)
    is_fwd = (lax.broadcasted_iota(jnp.int32, (rows, 1), 0) % n_seq) < half
    sub_fwd = lax.broadcasted_iota(jnp.int32, (n_seq, LANES), 0) < half

    @pl.when(c == 0)
    def _():
        h_ref[...] = jnp.zeros_like(h_ref)

    def project_in(k):
        ks = slice(k * LANES, (k + 1) * LANES)
        z = k % 2
        for b in range(half):
            stage_ref[k, b * pitch:b * pitch + steps, :] = ufs[b][:, ks]
            stage_ref[k, (half + b) * pitch:(half + b) * pitch + steps, :] = urs[b][:, ks]
        for t in range(steps):
            ga = stage_ref[k, pl.ds(t, n_seq, stride=pitch), :]
            gb = stage_ref[k, pl.ds(steps - 1 - t, n_seq, stride=pitch), :]
            lhs_ref[z, t * n_seq:(t + 1) * n_seq, 0:LANES] = jnp.where(sub_fwd, ga, 0.0)
            lhs_ref[z, t * n_seq:(t + 1) * n_seq, LANES:2 * LANES] = jnp.where(sub_fwd, 0.0, gb)
        bu_ref[k] = _dot(lhs_ref[z].astype(BF16), bmat_ref[k])

    def recur(k):
        z = k % 2
        lre = lam_ref[k, :, 0:sw]
        lim = lam_ref[k, :, sw:2 * sw]
        hr = h_ref[k, :, 0:sw]
        hi = h_ref[k, :, sw:2 * sw]
        for t in range(steps):
            rs = slice(t * n_seq, (t + 1) * n_seq)
            hr, hi = (lre * hr - lim * hi + bu_ref[k, rs, 0:sw],
                      lre * hi + lim * hr + bu_ref[k, rs, sw:2 * sw])
            st_ref[z, rs, 0:sw] = hr
            st_ref[z, rs, sw:2 * sw] = hi
        h_ref[k, :, 0:sw] = hr
        h_ref[k, :, sw:2 * sw] = hi

    def read_out(k):
        ks = slice(k * LANES, (k + 1) * LANES)
        z = k % 2
        hr_rows = rows // 2
        for part in range(2):
            rs = slice(part * hr_rows, (part + 1) * hr_rows)
            y2 = _dot(st_ref[z, rs, :].astype(BF16), cmat_ref[k])
            ysc_ref[z, rs, :] = jnp.where(is_fwd[rs], y2[:, 0:LANES], y2[:, LANES:2 * LANES])
        for t in range(steps):
            g = ysc_ref[z, t * n_seq:(t + 1) * n_seq, :]
            ya_ref[k, pl.ds(t, n_seq, stride=pitch), :] = g
            yb_ref[k, pl.ds(steps - 1 - t, n_seq, stride=pitch), :] = g
        for b in range(half):
            of_ref[b, :, ks] = ya_ref[k, b * pitch:b * pitch + steps, :]
            or_ref[b, :, ks] = yb_ref[k, (half + b) * pitch:(half + b) * pitch + steps, :]

    for k in range(nblk):
        project_in(k)
    recur(0)
    for k in range(1, nblk):
        recur(k)
        read_out(k - 1)
    read_out(nblk - 1)


def _s5_scan(pf, bmat, cmat, lam, *, n_batch, seq, ctx, width):
    steps = S5_STEPS
    nblk = width // LANES
    rows = steps * SUBLANES
    sw2 = bmat.shape[-1]
    ncc = ctx // steps
    nlc = seq // steps
    ctx0 = n_batch * nlc

    def fwd_map(b):
        return lambda c: (jnp.where(c < ncc, ctx0 + b * ncc + c, b * nlc + (c - ncc)), 0)

    def rev_map(b):
        return lambda c: (jnp.where(c < ncc, ctx0 + b * ncc + (ncc - 1 - c), b * nlc + (nlc - 1 - (c - ncc))), 0)

    of_map = lambda c: (0, jnp.where(c < ncc, nlc + c, c - ncc), 0)
    or_map = lambda c: (0, jnp.where(c < ncc, nlc + (ncc - 1 - c), nlc - 1 - (c - ncc)), 0)
    u_specs = ([pl.BlockSpec((steps, width), fwd_map(b)) for b in range(n_batch)]
               + [pl.BlockSpec((steps, width), rev_map(b)) for b in range(n_batch)])
    out_sds = jax.ShapeDtypeStruct((n_batch, seq + ctx, width), F32)
    stage = pltpu.VMEM((nblk, SUBLANES * S5_PITCH, LANES), F32)
    return pl.pallas_call(
        _s5_scan_kernel,
        out_shape=(out_sds, out_sds),
        grid=(ncc + nlc,),
        in_specs=u_specs + [_const_spec(bmat.shape), _const_spec(cmat.shape), _const_spec(lam.shape)],
        out_specs=(pl.BlockSpec((n_batch, steps, width), of_map),
                   pl.BlockSpec((n_batch, steps, width), or_map)),
        scratch_shapes=[pltpu.VMEM((nblk, SUBLANES, sw2), F32),
                        stage, pltpu.VMEM((2, rows, 2 * LANES), F32),
                        pltpu.VMEM((nblk, rows, sw2), F32), pltpu.VMEM((2, rows, sw2), F32),
                        pltpu.VMEM((2, rows, LANES), F32), stage, stage],
        compiler_params=_cparams(("arbitrary",)),
        name="s5_scan",
    )(*([pf] * (2 * n_batch)), bmat, cmat, lam)


def _s5_post_kernel(pf_ref, yf_ref, yr_ref, d_ref, w_ref, b_ref, o_ref, *, width):
    u = pf_ref[:, 0:width]
    y = d_ref[...] * u + yf_ref[0] + yr_ref[0]
    k0 = math.sqrt(2.0 / math.pi)
    g = 0.5 * y * (1.0 + jnp.tanh(k0 * (y + 0.044715 * (y * y * y))))
    z = _dot(g.astype(BF16), w_ref[...]) + b_ref[...]
    o_ref[...] = (g * jax.nn.sigmoid(z)).astype(BF16)


def _s5_post(pf, yf, yr, d_skip, glu_w, glu_b, *, n_batch, seq, ctx, need_ctx):
    width = yf.shape[2]
    kern = functools.partial(_s5_post_kernel, width=width)
    consts = [_const_spec((1, width)), _const_spec(glu_w.shape), _const_spec((1, width))]

    def call(tp, n_tiles, row_block0, seq_block0, n_out, name):
        return pl.pallas_call(
            kern,
            out_shape=jax.ShapeDtypeStruct((n_out, width), BF16),
            grid=(n_batch, n_tiles),
            in_specs=[pl.BlockSpec((tp, width), lambda b, j: (row_block0 + b * n_tiles + j, 0)),
                      pl.BlockSpec((1, tp, width), lambda b, j: (b, seq_block0 + j, 0)),
                      pl.BlockSpec((1, tp, width), lambda b, j: (b, seq_block0 + j, 0))] + consts,
            out_specs=pl.BlockSpec((tp, width), lambda b, j: (b * n_tiles + j, 0)),
            compiler_params=_cparams(("arbitrary", "arbitrary")),
            name=name,
        )(pf, yf, yr, d_skip, glu_w, glu_b)

    tp = min(S5_POST_ROWS, seq)
    y_lat = call(tp, seq // tp, 0, 0, n_batch * seq, "s5_post")
    if not need_ctx:
        return y_lat
    return y_lat, call(ctx, 1, n_batch * seq // ctx, seq // ctx, n_batch * ctx, "s5_post_ctx")


def _merge_kernel(x_ref, mod_ref, nw_ref, *rest, has_tail, n_first):
    y_refs = rest[:N_BRANCH + sum(has_tail)]
    wg_ref, wb_ref, wo_ref, o_ref, acc_ref = rest[len(y_refs):]
    x = x_ref[...]
    d = x.shape[1]
    m = mod_ref[0]
    n = _rms_mod(x, nw_ref[...], m[3:4], m[4:5]).astype(BF16)
    ys = []
    pos = 0
    for b in range(N_BRANCH):
        y = y_refs[pos][...]
        pos += 1
        if has_tail[b]:
            y = jnp.where(pl.program_id(0) < n_first, y, y_refs[pos][...])
            pos += 1
        ys.append(y)
    for b in range(N_BRANCH):
        gate = jax.nn.sigmoid(_dot(n, wg_ref[:, b * d:(b + 1) * d]))
        contrib = gate * _dot(ys[b], wb_ref[b])
        if b == 0:
            acc_ref[...] = contrib
        else:
            acc_ref[...] += contrib
    o_ref[...] = x + m[5:6] * _dot(acc_ref[...].astype(BF16), wo_ref[...])


def _merge(h, n_rows, mod, norm_w, ys, wg, wb, wo, *, n_lat_tiles, tiles_per_batch, n_batch):
    d = h.shape[1]
    bw = wb.shape[1]
    n_first = n_lat_tiles
    has_tail = tuple(isinstance(y, tuple) for y in ys)
    y_specs, y_ops = [], []
    for y in ys:
        if isinstance(y, tuple):
            y_specs += [pl.BlockSpec((TM, bw), lambda i: (jnp.minimum(i, n_first - 1), 0)),
                        pl.BlockSpec((TM, bw), lambda i: (jnp.maximum(i - n_first, 0), 0))]
            y_ops += list(y)
        else:
            y_specs.append(pl.BlockSpec((TM, bw), lambda i: (i, 0)))
            y_ops.append(y)
    kern = functools.partial(_merge_kernel, has_tail=has_tail, n_first=n_first)
    return pl.pallas_call(
        kern,
        out_shape=jax.ShapeDtypeStruct((n_rows, d), F32),
        grid=(n_rows // TM,),
        in_specs=[pl.BlockSpec((TM, d), lambda i: (i, 0)),
                  pl.BlockSpec((1, N_MOD, d), lambda i: (_mod_row_map(n_lat_tiles, tiles_per_batch, n_batch)(i), 0, 0)),
                  _const_spec((1, d))]
                 + y_specs
                 + [_const_spec(wg.shape), _const_spec(wb.shape), _const_spec(wo.shape)],
        out_specs=pl.BlockSpec((TM, d), lambda i: (i, 0)),
        scratch_shapes=[pltpu.VMEM((TM, d), F32)],
        compiler_params=_cparams(("arbitrary",)),
        name="merge",
    )(h, mod, norm_w, *y_ops, wg, wb, wo)


def _swap_rot_pairs(w, nf):
    lead = w.shape[:-1]
    n = w.shape[-1]
    return w.reshape(lead + (n // (2 * nf), 2, nf))[..., ::-1, :].reshape(lead + (n,))


def _rope_tables(seq, dim, lane_off, width, period):
    nf = dim // 4
    pos = jnp.arange(seq)
    rows = (pos // GRID_W).astype(F32)
    cols = (pos % GRID_W).astype(F32)
    inv_freq = ROPE_BASE ** (-jnp.arange(nf, dtype=F32) / nf)
    ang_r = rows[:, None] * inv_freq[None, :]
    ang_c = cols[:, None] * inv_freq[None, :]
    cos = jnp.concatenate([jnp.cos(ang_r)] * 2 + [jnp.cos(ang_c)] * 2, axis=1)
    sin = jnp.concatenate([-jnp.sin(ang_r), jnp.sin(ang_r), -jnp.sin(ang_c), jnp.sin(ang_c)], axis=1)
    c_per = jnp.ones((seq, period), F32).at[:, lane_off:lane_off + dim].set(cos)
    s_per = jnp.zeros((seq, period), F32).at[:, lane_off:lane_off + dim].set(sin)
    reps = width // period
    c_tab = jnp.concatenate([jnp.tile(c_per, (1, reps)), jnp.ones((TM, width), F32)], axis=0)
    s_tab = jnp.concatenate([jnp.tile(s_per, (1, reps)), jnp.zeros((TM, width), F32)], axis=0)
    return c_tab, s_tab


def _inproj_weights(w_in):
    d = w_in.shape[0]
    hw = NA_HEADS * HEAD_DIM
    o_sq = 3 * hw
    o_sk = o_sq + SWA_HEADS * HEAD_DIM
    o_sv = o_sk + SWA_KV_HEADS * HEAD_DIM
    o_s5 = o_sv + SWA_KV_HEADS * HEAD_DIM
    s5w = 512
    o_cq = o_s5 + s5w
    o_ckv = o_cq + 256
    o_kr = o_ckv + 128
    o_g = o_kr + MLA_ROPE
    qscale = LOG2E * HEAD_DIM ** -0.5
    na = jnp.concatenate([w_in[:, :hw] * qscale, w_in[:, hw:3 * hw]], axis=1)
    sq = w_in[:, o_sq:o_sk] * qscale
    sk = w_in[:, o_sk:o_sv]
    sv = w_in[:, o_sv:o_s5]
    def dup_heads(a):
        return jnp.concatenate([a[:, kv * HEAD_DIM:(kv + 1) * HEAD_DIM]
                                for kv in range(SWA_KV_HEADS) for _ in range(2)], axis=1)

    sk_dup = dup_heads(sk)
    sv_dup = dup_heads(sv)
    wr = jnp.concatenate([sq, sk_dup], axis=1)
    kr = w_in[:, o_kr:o_g]
    lpad = jnp.zeros((d, MLA_NOPE), F32)
    rpad = jnp.zeros((d, LANES - MLA_NOPE - MLA_ROPE), F32)
    wf = jnp.concatenate([w_in[:, o_s5:o_kr], lpad, kr, rpad], axis=1)
    wg = w_in[:, o_g:]
    proj = tuple(a.astype(BF16) for a in (na, wr, sv_dup, wf))
    return proj, wg.astype(BF16)


def _mla_weights(w_uq, w_ukv):
    ql = w_uq.shape[0]
    kvl = w_ukv.shape[0]
    dq = MLA_NOPE + MLA_ROPE
    wq3 = w_uq.reshape(ql, MLA_HEADS, dq)
    pad = jnp.zeros((ql, MLA_HEADS, LANES - dq), F32)
    qscale = math.log2(math.e) * dq ** -0.5
    wq = jnp.concatenate([wq3 * qscale, pad], axis=2).reshape(ql, MLA_HEADS * LANES)
    wkv3 = w_ukv.reshape(kvl, MLA_HEADS, MLA_NOPE + MLA_V)
    wk = jnp.concatenate([wkv3[:, :, :MLA_NOPE], jnp.zeros((kvl, MLA_HEADS, LANES - MLA_NOPE), F32)], axis=2)
    wk = wk.reshape(kvl, MLA_HEADS * LANES)
    wv = jnp.concatenate([wkv3[:, :, MLA_NOPE:], jnp.zeros((kvl, MLA_HEADS, LANES - MLA_V), F32)], axis=2)
    wv = wv.reshape(kvl, MLA_HEADS * LANES)
    rope_sw = _swap_rot_pairs(wq3[:, :, MLA_NOPE:], MLA_ROPE // 4) * qscale
    wqs = jnp.concatenate([jnp.zeros((ql, MLA_HEADS, MLA_NOPE), F32), rope_sw, pad], axis=2)
    wqs = wqs.reshape(ql, MLA_HEADS * LANES)
    return tuple(a.astype(BF16) for a in (wq, wqs, wk, wv))


def _s5_params(lam_re, lam_im, log_dt, b_re, b_im, c_re, c_im):
    a = lam_re.astype(F32)
    w = lam_im.astype(F32)
    dt = jnp.exp(log_dt.astype(F32))[..., None]
    mag = jnp.exp(a * dt)
    lb_re = mag * jnp.cos(w * dt)
    lb_im = mag * jnp.sin(w * dt)
    den = a * a + w * w
    cf_re = ((lb_re - 1.0) * a + lb_im * w) / den
    cf_im = (lb_im * a - (lb_re - 1.0) * w) / den
    bb_re = cf_re[..., None] * b_re - cf_im[..., None] * b_im
    bb_im = cf_re[..., None] * b_im + cf_im[..., None] * b_re
    n_dir, g, p, cg = b_re.shape
    gpb = LANES // cg
    nblk = g // gpb
    eye = jnp.eye(gpb, dtype=F32)

    def in_map(x):
        x5 = jnp.swapaxes(x.reshape(n_dir, nblk, gpb, p, cg), 3, 4)
        full = x5[:, :, :, :, None, :] * eye[None, None, :, None, :, None]
        return full.reshape(n_dir, nblk, gpb * cg, gpb * p)

    def out_map(x):
        x5 = jnp.swapaxes(x.reshape(n_dir, nblk, gpb, cg, p), 3, 4)
        full = x5[:, :, :, :, None, :] * eye[None, None, :, None, :, None]
        return full.reshape(n_dir, nblk, gpb * p, gpb * cg)

    b_in = jnp.concatenate([in_map(bb_re), in_map(bb_im)], axis=3)
    bmat = jnp.concatenate([b_in[0], b_in[1]], axis=1).astype(BF16)
    c_out = jnp.concatenate([out_map(c_re.astype(F32)), out_map(-c_im.astype(F32))], axis=2)
    cmat = jnp.concatenate([c_out[0], c_out[1]], axis=2).astype(BF16)
    half = SUBLANES // 2
    lam2 = jnp.concatenate([lb_re.reshape(n_dir, nblk, gpb * p), lb_im.reshape(n_dir, nblk, gpb * p)], axis=2)
    lam = jnp.concatenate([jnp.broadcast_to(lam2[0][:, None, :], (nblk, half, 2 * gpb * p)),
                           jnp.broadcast_to(lam2[1][:, None, :], (nblk, half, 2 * gpb * p))], axis=1)
    return bmat, cmat, lam


def _ffn_weights(wg, wu, wd):
    d, ff = wg.shape
    wg3 = wg.astype(BF16)
    wu3 = wu.astype(BF16)
    wd3 = wd.astype(BF16)
    return wg3, wu3, wd3


def kernel(x, c, ctx, c_ctx, ada_w, ada_b, ffn1_norm, ffn1_w_gate, ffn1_w_up, ffn1_w_down, mix_norm, w_in, na_rpb, swa_sink, s5_lambda_re, s5_lambda_im, s5_log_dt, s5_b_re, s5_b_im, s5_c_re, s5_c_im, s5_d, s5_glu_w, s5_glu_b, mla_q_norm, mla_w_uq, mla_kv_norm, mla_w_ukv, w_branch, w_out, ffn2_norm, ffn2_w_gate, ffn2_w_up, ffn2_w_down, final_norm):
    n_batch, seq, d = x.shape
    n_ctx = ctx.shape[1]
    depth = ada_w.shape[0]
    assert 2 * n_batch == SUBLANES and seq % TM == 0 and (n_batch * n_ctx) % TM == 0
    n_lat = n_batch * seq
    n_all = n_lat + n_batch * n_ctx
    tiles_per_batch = seq // TM
    n_lat_tiles = n_lat // TM
    geo = dict(n_lat_tiles=n_lat_tiles, tiles_per_batch=tiles_per_batch, n_batch=n_batch)

    h = x.reshape(n_lat, d)
    h_ctx = ctx.reshape(n_batch * n_ctx, d)
    cc = jnp.concatenate([c, c_ctx[None, :], jnp.zeros((SUBLANES - n_batch - 1, d), F32)], axis=0)
    mod = _ada_mod(cc, ada_w, ada_b)

    cs_sw, sn_sw = _rope_tables(seq, HEAD_DIM, 0, 2 * LANES, HEAD_DIM)
    ck_kr, sk_kr = _rope_tables(seq, MLA_ROPE, LANES + MLA_NOPE, 2 * LANES, 2 * LANES)
    cq_ml, sq_ml = _rope_tables(seq, MLA_ROPE, MLA_NOPE, 2 * LANES, LANES)
    rows_n = seq // GRID_W
    s5w = s5_d.shape[1]
    sw_col0 = 3 * NA_HEADS * HEAD_DIM

    w_ffn1 = _ffn_weights(ffn1_w_gate[0], ffn1_w_up[0], ffn1_w_down[0])
    for l in range(depth):
        need_ctx = l < depth - 1
        last = l == depth - 1
        ml = mod[l]
        ones = jnp.ones((1, d), F32)
        h, *w_ffn2 = _ffn(h, n_all, ml, ffn1_norm[l][None, :], *w_ffn1, ones, base=0, final=False,
                          h_tail=h_ctx if l == 0 else None,
                          cast_next=(l, ffn2_w_gate, ffn2_w_up, ffn2_w_down), **geo)
        proj_w, gate_w = _inproj_weights(w_in[l])
        mla_w = _mla_weights(mla_w_uq[l], mla_w_ukv[l])
        pb, pf, qm, km, vm = _inproj(h, ml, mix_norm[l][None, :], proj_w, (cs_sw, sn_sw, ck_kr, sk_kr),
                                     (mla_q_norm[l][None, :], mla_kv_norm[l][None, :], *mla_w, cq_ml, sq_ml), **geo)
        bias = _na_bias_table(na_rpb[l].astype(F32), rows_n)
        y_na = _na_attention(pb, bias, n_batch=n_batch, seq=seq, ctx=n_ctx, need_ctx=need_ctx)
        y_sw_l, y_sw_c = _swa_attention(pb, swa_sink[l].astype(F32) * LOG2E, n_batch=n_batch, seq=seq, ctx=n_ctx,
                                        need_ctx=need_ctx, col0=sw_col0)
        y_sw = (y_sw_l, y_sw_c) if need_ctx else y_sw_l
        y_mla = _mla_attention(qm, km, vm, n_batch=n_batch, seq=seq, ctx=n_ctx, need_ctx=need_ctx)
        s5p = _s5_params(s5_lambda_re[l], s5_lambda_im[l], s5_log_dt[l], s5_b_re[l], s5_b_im[l],
                         s5_c_re[l], s5_c_im[l])
        yf, yr = _s5_scan(pf, *s5p, n_batch=n_batch, seq=seq, ctx=n_ctx, width=s5w)
        y_s5 = _s5_post(pf, yf, yr, s5_d[l][None, :].astype(F32), s5_glu_w[l].astype(BF16),
                        s5_glu_b[l][None, :].astype(F32), n_batch=n_batch, seq=seq, ctx=n_ctx, need_ctx=need_ctx)
        n_rows = n_all if need_ctx else n_lat
        h = _merge(h, n_rows, ml, mix_norm[l][None, :], (y_na, y_sw, y_s5, y_mla), gate_w,
                   w_branch[l].astype(BF16), w_out[l].astype(BF16), **geo)
        if last:
            h = _ffn(h, n_rows, ml, ffn2_norm[l][None, :], *w_ffn2, final_norm[None, :], base=6, final=True, **geo)
        else:
            h, *w_ffn1 = _ffn(h, n_rows, ml, ffn2_norm[l][None, :], *w_ffn2, final_norm[None, :], base=6, final=False,
                              cast_next=(l + 1, ffn1_w_gate, ffn1_w_up, ffn1_w_down), **geo)
    return h.reshape(n_batch, seq, d)
```

```python
import functools
import math

import numpy as np
import jax
import jax.numpy as jnp
from jax import lax
from jax.experimental import pallas as pl
from jax.experimental.pallas import tpu as pltpu

F32 = jnp.float32
BF16 = jnp.bfloat16

GRID_W = 64
HEAD_DIM = 64
N_BRANCH = 4
NA_HEADS = 8
NA_WIN_ROWS = 8
NA_WIN_COLS = 16
SWA_HEADS = 8
SWA_KV_HEADS = 2
SWA_WINDOW = 128
S5_GROUP = 16
S5_STATE = 64
MLA_HEADS = 8
MLA_NOPE = 64
MLA_ROPE = 32
MLA_V = 64
MACARON_WEIGHT = 0.5
ROPE_BASE = 10000.0
EPS = 1e-6
N_MOD = 9

LANES = 128
SUBLANES = 8
TM = 512
FF_CHUNK = 256
NA_QROWS = 4
NA_KROWS = NA_QROWS + NA_WIN_ROWS - 1
SWA_BLK = 128
SWA_UNROLL = 4
MLA_TQ = 1024
MLA_SUB = 256
MLA_AHEAD = 1
S5_STEPS = 128
S5_POST_ROWS = 1024
S5_PITCH = S5_STEPS + SUBLANES
NEG = -1e30
LOG2E = math.log2(math.e)
VMEM_LIMIT = 56 * 1024 * 1024


def _cparams(sem, flags=None):
    return pltpu.CompilerParams(dimension_semantics=sem, vmem_limit_bytes=VMEM_LIMIT, flags=flags)


def _const_spec(shape):
    nd = len(shape)
    return pl.BlockSpec(shape, lambda *_: (0,) * nd, pipeline_mode=pl.Buffered(1))


def _dot(a, b):
    return jnp.dot(a, b, preferred_element_type=F32)


def _dot_t(a, b):
    return lax.dot_general(a, b, (((1,), (1,)), ((), ())), preferred_element_type=F32)


def _rms(x, w):
    return x * lax.rsqrt(jnp.mean(x * x, axis=-1, keepdims=True) + EPS) * w


def _rms_mod(x, w, shift, scale):
    return _rms(x, w) * (1.0 + scale) + shift


def _mod_row_map(n_lat_tiles, tiles_per_batch, n_batch):
    def f(i):
        return jnp.where(i < n_lat_tiles, i // tiles_per_batch, n_batch)
    return f


def _ada_kernel(c_ref, w_ref, b_ref, o_ref):
    c = c_ref[...]
    s = c * jax.nn.sigmoid(c)
    w = w_ref[0]
    s_hi = s.astype(BF16)
    s_lo = (s - s_hi.astype(F32)).astype(BF16)
    w_hi = w.astype(BF16)
    w_lo = (w - w_hi.astype(F32)).astype(BF16)
    o_ref[0] = _dot(s_hi, w_hi) + (_dot(s_hi, w_lo) + _dot(s_lo, w_hi)) + b_ref[0]


def _ada_mod(cc, ada_w, ada_b):
    depth, d, nd = ada_w.shape
    tn = 1024
    out = pl.pallas_call(
        _ada_kernel,
        out_shape=jax.ShapeDtypeStruct((depth, SUBLANES, nd), F32),
        grid=(depth, nd // tn),
        in_specs=[pl.BlockSpec((SUBLANES, d), lambda l, j: (0, 0)),
                  pl.BlockSpec((1, d, tn), lambda l, j: (l, 0, j)),
                  pl.BlockSpec((1, 1, tn), lambda l, j: (l, 0, j))],
        out_specs=pl.BlockSpec((1, SUBLANES, tn), lambda l, j: (l, 0, j)),
        compiler_params=_cparams(("arbitrary", "arbitrary")),
        name="ada_mod",
    )(cc, ada_w, ada_b.reshape(depth, 1, nd))
    return out.reshape(depth, SUBLANES, N_MOD, d)


def _ffn_kernel(x_ref, xc_ref, mod_ref, nw_ref, wg_ref, wu_ref, wd_ref, fw_ref, *rest, base, final, n_first):
    if len(rest) == 1:
        (o_ref,) = rest
    else:
        ng_ref, nu_ref, nd_ref, o_ref, cg_ref, cu_ref, cd_ref = rest
        cg_ref[...] = ng_ref[...].astype(BF16)
        cu_ref[...] = nu_ref[...].astype(BF16)
        cd_ref[...] = nd_ref[...].astype(BF16)
    x = x_ref[...]
    if n_first is not None:
        x = jnp.where(pl.program_id(0) < n_first, x, xc_ref[...])
    m = mod_ref[0]
    n = _rms_mod(x, nw_ref[...], m[base:base + 1], m[base + 1:base + 2]).astype(BF16)
    ff = wg_ref.shape[1]
    fc = FF_CHUNK if ff % FF_CHUNK == 0 else ff
    acc = None
    for c in range(ff // fc):
        cs = slice(c * fc, (c + 1) * fc)
        g = _dot(n, wg_ref[:, cs])
        u = _dot(n, wu_ref[:, cs])
        a = (g * jax.nn.sigmoid(g) * u).astype(BF16)
        y = _dot(a, wd_ref[cs, :])
        acc = y if acc is None else acc + y
    out = x + MACARON_WEIGHT * m[base + 2:base + 3] * acc
    if final:
        out = _rms(out, fw_ref[...])
    o_ref[...] = out


def _ffn(h, n_rows, mod, norm_w, wg, wu, wd, final_w, *, base, final, n_lat_tiles, tiles_per_batch, n_batch,
         h_tail=None, cast_next=None):
    d = h.shape[1]
    ff = wg.shape[1]
    steps = n_rows // TM
    if h_tail is None:
        n_first = None
        h_tail = h
        x_map = lambda i: (i, 0)
        t_map = lambda i: (0, 0)
    else:
        n_first = h.shape[0] // TM
        x_map = lambda i: (jnp.minimum(i, n_first - 1), 0)
        t_map = lambda i: (jnp.maximum(i - n_first, 0), 0)
    kern = functools.partial(_ffn_kernel, base=base, final=final, n_first=n_first)
    in_specs = [pl.BlockSpec((TM, d), x_map),
                pl.BlockSpec((TM, d) if n_first is not None else (SUBLANES, d), t_map),
                pl.BlockSpec((1, N_MOD, d), lambda i: (_mod_row_map(n_lat_tiles, tiles_per_batch, n_batch)(i), 0, 0)),
                _const_spec((1, d)),
                _const_spec((d, ff)), _const_spec((d, ff)), _const_spec((ff, d)),
                _const_spec((1, d))]
    out_shape = jax.ShapeDtypeStruct((n_rows, d), F32)
    out_specs = pl.BlockSpec((TM, d), lambda i: (i, 0))
    operands = [h, h_tail, mod, norm_w, wg, wu, wd, final_w]
    if cast_next is not None:
        up_rows = min(r for r in range(2 * SUBLANES, d + 1, 2 * SUBLANES) if d % r == 0 and d // r <= steps)
        dn_rows = min(r for r in range(2 * SUBLANES, ff + 1, 2 * SUBLANES) if ff % r == 0 and ff // r <= steps)
        n_up, n_dn = d // up_rows, ff // dn_rows
        up_spec = pl.BlockSpec((up_rows, ff), lambda i: (jnp.minimum(i, n_up - 1), 0))
        dn_spec = pl.BlockSpec((dn_rows, d), lambda i: (jnp.minimum(i, n_dn - 1), 0))
        nl, *next_w = cast_next
        in_specs += [pl.BlockSpec((None, up_rows, ff), lambda i: (nl, jnp.minimum(i, n_up - 1), 0))] * 2
        in_specs += [pl.BlockSpec((None, dn_rows, d), lambda i: (nl, jnp.minimum(i, n_dn - 1), 0))]
        operands += next_w
        out_shape = (out_shape, jax.ShapeDtypeStruct((d, ff), BF16), jax.ShapeDtypeStruct((d, ff), BF16),
                     jax.ShapeDtypeStruct((ff, d), BF16))
        out_specs = (out_specs, up_spec, up_spec, dn_spec)
    return pl.pallas_call(
        kern,
        out_shape=out_shape,
        grid=(steps,),
        in_specs=in_specs,
        out_specs=out_specs,
        compiler_params=_cparams(("arbitrary",)),
        name="ffn_final" if final else "ffn",
    )(*operands)


def _rope_apply(y, cs, sn, nf):
    w = y.shape[1]
    first = (lax.broadcasted_iota(jnp.int32, y.shape, 1) & nf) == 0
    ysw = jnp.where(first, pltpu.roll(y, w - nf, 1), pltpu.roll(y, nf, 1))
    return y * cs + ysw * sn


def _inproj_kernel(x_ref, mod_ref, nw_ref, wa_ref, wr_ref, wv_ref, wf_ref,
                   cs_ref, sn_ref, ck_ref, sk_ref,
                   qn_ref, kn_ref, wq_ref, wqs_ref, wk_ref, wvm_ref, cq_ref, sq_ref,
                   ob_ref, of_ref, q_ref, k_ref, v_ref):
    x = x_ref[...]
    m = mod_ref[0]
    n = _rms_mod(x, nw_ref[...], m[3:4], m[4:5]).astype(BF16)
    na = wa_ref.shape[1]
    nr = wr_ref.shape[1]
    nv = wv_ref.shape[1]
    nf = wf_ref.shape[1]
    cw = 2 * LANES
    cq = _dot(n, wf_ref[:, nf - 2 * cw:nf - cw])
    y = _rope_apply(_dot(n, wf_ref[:, nf - cw:nf]), ck_ref[...], sk_ref[...], MLA_ROPE // 4)
    _mla_project(cq, y[:, :LANES], y[:, LANES:], qn_ref, kn_ref, wq_ref, wqs_ref, wk_ref, wvm_ref,
                 cq_ref, sq_ref, q_ref, k_ref, v_ref)
    for c in range(nf // cw - 2):
        of_ref[:, c * cw:(c + 1) * cw] = _dot(n, wf_ref[:, c * cw:(c + 1) * cw])
    for c in range(na // cw):
        ob_ref[:, c * cw:(c + 1) * cw] = _dot(n, wa_ref[:, c * cw:(c + 1) * cw]).astype(BF16)
    for c in range(nr // cw):
        y = _dot(n, wr_ref[:, c * cw:(c + 1) * cw])
        ob_ref[:, na + c * cw:na + (c + 1) * cw] = _rope_apply(y, cs_ref[...], sn_ref[...], HEAD_DIM // 4).astype(BF16)
    for c in range(nv // cw):
        ob_ref[:, na + nr + c * cw:na + nr + (c + 1) * cw] = _dot(n, wv_ref[:, c * cw:(c + 1) * cw]).astype(BF16)


def _inproj(h, mod, norm_w, w, tabs, mla, *, n_lat_tiles, tiles_per_batch, n_batch):
    n_rows, d = h.shape
    wa, wr, wv, wf = w
    cs, sn, ck, sk = tabs
    qn, kn, wq, wqs, wk, wvm, cq_tab, sq_tab = mla
    cw = 2 * LANES
    assert wq.shape[0] == cw and wk.shape[0] == LANES
    nb = wa.shape[1] + wr.shape[1] + wv.shape[1]
    ns5 = wf.shape[1] - 2 * cw
    hq = MLA_HEADS * LANES
    tab_map = lambda i: (jnp.where(i < n_lat_tiles, i % tiles_per_batch, tiles_per_batch), 0)
    row = lambda width: pl.BlockSpec((TM, width), lambda i: (i, 0))
    return pl.pallas_call(
        _inproj_kernel,
        out_shape=(jax.ShapeDtypeStruct((n_rows, nb), BF16), jax.ShapeDtypeStruct((n_rows, ns5), F32))
                  + (jax.ShapeDtypeStruct((n_rows, hq), BF16),) * 3,
        grid=(n_rows // TM,),
        in_specs=[pl.BlockSpec((TM, d), lambda i: (i, 0)),
                  pl.BlockSpec((1, N_MOD, d), lambda i: (_mod_row_map(n_lat_tiles, tiles_per_batch, n_batch)(i), 0, 0)),
                  _const_spec((1, d))]
                 + [_const_spec(a.shape) for a in w]
                 + [pl.BlockSpec((TM, cw), tab_map)] * 4
                 + [_const_spec(a.shape) for a in (qn, kn, wq, wqs, wk, wvm)]
                 + [pl.BlockSpec((TM, cw), tab_map)] * 2,
        out_specs=(row(nb), row(ns5), row(hq), row(hq), row(hq)),
        compiler_params=_cparams(("arbitrary",)),
        name="inproj",
    )(h, mod, norm_w, *w, cs, sn, ck, sk, qn, kn, wq, wqs, wk, wvm, cq_tab, sq_tab)


def _lane_half_masks(rows):
    lane = lax.broadcasted_iota(jnp.int32, (rows, LANES), 1)
    return lane < HEAD_DIM


def _with_ones(v):
    return jnp.concatenate([v, jnp.ones_like(v)], axis=1)


def _na_kernel(q_ref, k_ref, v_ref, kc_ref, vc_ref, bias_ref, o_ref, *, n_blk, rows_n):
    blk = pl.program_id(1)
    nq = NA_QROWS * GRID_W
    nk = NA_KROWS * GRID_W
    ws = jnp.clip(NA_QROWS * blk - NA_WIN_ROWS // 2, 0, rows_n - NA_KROWS)
    r0 = pl.multiple_of(ws * GRID_W, GRID_W)
    lo = _lane_half_masks(nq)
    n_pairs = NA_HEADS // 2

    def scores(j):
        sl = slice(j * LANES, (j + 1) * LANES)
        qp = q_ref[:, sl]
        zero = jnp.zeros_like(qp)
        qs = jnp.concatenate([jnp.where(lo, qp, zero), jnp.where(lo, zero, qp)], axis=0)
        return _dot_t(qs, k_ref[pl.ds(r0, nk), sl]) + bias_ref[0, j], _dot_t(qs, kc_ref[:, sl])

    def finish(j, s1, s2):
        sl = slice(j * LANES, (j + 1) * LANES)
        m = jnp.maximum(jnp.max(s1, axis=-1, keepdims=True), jnp.max(s2, axis=-1, keepdims=True))
        o2 = (_dot(jnp.exp2(s1 - m).astype(BF16), _with_ones(v_ref[pl.ds(r0, nk), sl]))
              + _dot(jnp.exp2(s2 - m).astype(BF16), _with_ones(vc_ref[:, sl])))
        o = o2[:, :LANES] / o2[:, LANES:]
        o_ref[:, sl] = jnp.where(lo, o[:nq], o[nq:]).astype(BF16)

    pending = scores(0)
    for j in range(n_pairs):
        following = scores(j + 1) if j + 1 < n_pairs else None
        finish(j, *pending)
        pending = following


def _na_attention(pb, bias, *, n_batch, seq, ctx, need_ctx):
    rows_n = seq // GRID_W
    n_blk = rows_n // NA_QROWS
    nq = NA_QROWS * GRID_W
    hw = NA_HEADS * HEAD_DIM
    n_q = n_blk + (1 if need_ctx else 0)
    assert ctx == nq
    lat_blocks = n_batch * seq // nq
    qmap = lambda b, i: (jnp.where(i < n_blk, b * n_blk + i, lat_blocks + b), 0)

    def bias_map(b, i):
        t = jnp.where(i == 0, 0, jnp.where(i == 1, 1, jnp.where(i == n_blk - 1, 3, jnp.where(i == n_blk, 4, 2))))
        return (t, 0, 0, 0)

    kern = functools.partial(_na_kernel, n_blk=n_blk, rows_n=rows_n)
    return pl.pallas_call(
        kern,
        out_shape=jax.ShapeDtypeStruct((n_batch * n_q * nq, hw), BF16),
        grid=(n_batch, n_q),
        in_specs=[pl.BlockSpec((nq, hw), qmap),
                  pl.BlockSpec((seq, hw), lambda b, i: (b, 1)),
                  pl.BlockSpec((seq, hw), lambda b, i: (b, 2)),
                  pl.BlockSpec((ctx, hw), lambda b, i: (n_batch * seq // ctx + b, 1)),
                  pl.BlockSpec((ctx, hw), lambda b, i: (n_batch * seq // ctx + b, 2)),
                  pl.BlockSpec((1, NA_HEADS // 2, 2 * nq, NA_KROWS * GRID_W), bias_map)],
        out_specs=pl.BlockSpec((nq, hw), qmap),
        compiler_params=_cparams(("arbitrary", "arbitrary")),
        name="na_attn",
    )(pb, pb, pb, pb, pb, bias)


def _na_bias_table(rpb, rows_n):
    n_blk = rows_n // NA_QROWS
    kr_n = min(NA_WIN_ROWS, rows_n)
    n_heads = rpb.shape[0]
    col = np.arange(GRID_W)
    c0 = np.clip(col - NA_WIN_COLS // 2, 0, GRID_W - NA_WIN_COLS)
    col_ok = (col[None, :] >= c0[:, None]) & (col[None, :] < c0[:, None] + NA_WIN_COLS)
    n_dc = rpb.shape[2]
    skew = 2 * GRID_W
    vec = jnp.concatenate([rpb[:, :, NA_WIN_COLS - 1:],
                           jnp.zeros(rpb.shape[:2] + (skew - n_dc,), F32),
                           rpb[:, :, :NA_WIN_COLS - 1]], axis=2) * LOG2E
    tiled = jnp.tile(vec, (1, 1, GRID_W))[:, :, :GRID_W * (skew - 1)]
    toep = tiled.reshape(rpb.shape[:2] + (GRID_W, skew - 1))[:, :, :, :GRID_W]
    toep = jnp.where(col_ok[None, None], toep, NEG)
    masked = jnp.full((n_heads, 1, GRID_W, GRID_W), NEG, F32)
    ext = jnp.concatenate([masked, toep, masked], axis=1)
    pair2 = jnp.concatenate([ext[:, :-1], ext[:, 1:]], axis=-1)
    n_dr = 2 * NA_WIN_ROWS - 1
    plans = []
    for blk in (0, 1, 2, n_blk - 1):
        ws = int(np.clip(NA_QROWS * blk - NA_WIN_ROWS // 2, 0, rows_n - NA_KROWS))
        plan = []
        for qr in range(NA_QROWS):
            r = NA_QROWS * blk + qr
            r0 = int(np.clip(r - kr_n // 2, 0, rows_n - kr_n))
            row = []
            for kk in range(0, NA_KROWS, 2):
                oks = tuple(r0 <= ws + kk + i < r0 + kr_n and kk + i < NA_KROWS for i in range(2))
                row.append((ws + kk - r + NA_WIN_ROWS, oks))
            plan.append(tuple(row))
        plans.append(tuple(plan))
    plans.append(None)
    nq = NA_QROWS * GRID_W
    nk = NA_KROWS * GRID_W
    kern = functools.partial(_na_bias_kernel, plans=tuple(plans))
    return pl.pallas_call(
        kern,
        out_shape=jax.ShapeDtypeStruct((len(plans), n_heads // 2, 2 * nq, nk), F32),
        grid=(n_heads // 2,),
        in_specs=[pl.BlockSpec((2, n_dr + 1, GRID_W, 2 * GRID_W), lambda j: (j, 0, 0, 0))],
        out_specs=pl.BlockSpec((len(plans), 1, 2 * nq, nk), lambda j: (0, j, 0, 0)),
        compiler_params=_cparams(("arbitrary",)),
        name="na_bias",
    )(pair2)


def _na_bias_kernel(p2_ref, o_ref, *, plans):
    nq = NA_QROWS * GRID_W
    lo = lax.broadcasted_iota(jnp.int32, (GRID_W, 2 * GRID_W), 1) < GRID_W
    neg = jnp.full((GRID_W, 2 * GRID_W), NEG, F32)
    for t, plan in enumerate(plans):
        if plan is None:
            o_ref[t, 0] = jnp.full(o_ref.shape[2:], NEG, F32)
            continue
        for half in range(2):
            for qr, row in enumerate(plan):
                rs = slice(half * nq + qr * GRID_W, half * nq + (qr + 1) * GRID_W)
                for kp, (e, (ok_a, ok_b)) in enumerate(row):
                    width = min(2 * GRID_W, o_ref.shape[3] - kp * 2 * GRID_W)
                    if ok_a or ok_b:
                        tile = p2_ref[half, e]
                        if not ok_a:
                            tile = jnp.where(lo, neg, tile)
                        if not ok_b:
                            tile = jnp.where(lo, tile, neg)
                    else:
                        tile = neg
                    o_ref[t, 0, rs, kp * 2 * GRID_W:kp * 2 * GRID_W + width] = tile[:, :width]


def _swa_kernel(sink_ref, q_ref, k_ref, v_ref, qc_ref, kc_ref, vc_ref, o_ref, oc_ref, *, seq, need_ctx):
    n_blk = seq // SWA_BLK
    band = 3 * SWA_BLK
    group = SWA_HEADS // SWA_KV_HEADS
    pairs = group // 2

    def group_scores(q_slabs, kv, kb, kc, mask_bias):
        r = q_slabs[0].shape[0]
        lo = _lane_half_masks(r)
        parts = []
        for qp in q_slabs:
            zero = jnp.zeros_like(qp)
            parts += [jnp.where(lo, qp, zero), jnp.where(lo, zero, qp)]
        qs = jnp.concatenate(parts, axis=0)
        s1 = None if kb is None else _dot_t(qs, kb) + mask_bias
        return s1, _dot_t(qs, kc)

    def group_finish(kv, s1, s2, vb, vc):
        r = s2.shape[0] // group
        lo = _lane_half_masks(r)
        row = lax.broadcasted_iota(jnp.int32, (group * r, 1), 0)
        sink = jnp.full((group * r, 1), sink_ref[kv * group + group - 1], F32)
        for g in range(group - 2, -1, -1):
            sink = jnp.where(row < (g + 1) * r, sink_ref[kv * group + g], sink)
        if s1 is None:
            m = jnp.maximum(jnp.max(s2, axis=-1, keepdims=True), sink)
            o2 = _dot(jnp.exp2(s2 - m).astype(BF16), _with_ones(vc))
        else:
            m = jnp.maximum(jnp.maximum(jnp.max(s1, axis=-1, keepdims=True), jnp.max(s2, axis=-1, keepdims=True)), sink)
            o2 = (_dot(jnp.exp2(s1 - m).astype(BF16), _with_ones(vb))
                  + _dot(jnp.exp2(s2 - m).astype(BF16), _with_ones(vc)))
        o = o2[:, :LANES] / (o2[:, LANES:] + jnp.exp2(sink - m))
        return [jnp.where(lo, o[(2 * i) * r:(2 * i + 1) * r], o[(2 * i + 1) * r:(2 * i + 2) * r]) for i in range(pairs)]

    qi = lax.broadcasted_iota(jnp.int32, (group * SWA_BLK, band), 0) & (SWA_BLK - 1)
    ki = lax.broadcasted_iota(jnp.int32, (group * SWA_BLK, band), 1)

    def blk_body(n2, carry):
        geo = []
        for j in range(SWA_UNROLL):
            n = n2 * SWA_UNROLL + j
            start = pl.multiple_of(jnp.clip((n - 1) * SWA_BLK, 0, seq - band), SWA_BLK)
            q0 = pl.multiple_of(n * SWA_BLK, SWA_BLK)
            delta = (start + ki) - (q0 + qi)
            geo.append((start, q0, jnp.where(jnp.abs(delta) <= SWA_WINDOW, 0.0, NEG).astype(F32)))
        units = [(j, kv) for j in range(SWA_UNROLL) for kv in range(SWA_KV_HEADS)]

        def scores(j, kv):
            start, q0, mask_bias = geo[j]
            ksl = slice(kv * LANES, (kv + 1) * LANES)
            slabs = [q_ref[pl.ds(q0, SWA_BLK), (kv * pairs + i) * LANES:(kv * pairs + i + 1) * LANES]
                     for i in range(pairs)]
            return group_scores(slabs, kv, k_ref[pl.ds(start, band), ksl], kc_ref[:, ksl], mask_bias)

        def finish(j, kv, s1, s2):
            start, q0, _ = geo[j]
            ksl = slice(kv * LANES, (kv + 1) * LANES)
            outs = group_finish(kv, s1, s2, v_ref[pl.ds(start, band), ksl], vc_ref[:, ksl])
            for i in range(pairs):
                o_ref[pl.ds(q0, SWA_BLK), (kv * pairs + i) * LANES:(kv * pairs + i + 1) * LANES] = outs[i].astype(BF16)

        pending = scores(*units[0])
        for idx, u in enumerate(units):
            following = scores(*units[idx + 1]) if idx + 1 < len(units) else None
            finish(*u, *pending)
            pending = following
        return carry

    lax.fori_loop(0, n_blk // SWA_UNROLL, blk_body, 0)

    if need_ctx:
        for kv in range(SWA_KV_HEADS):
            ksl = slice(kv * LANES, (kv + 1) * LANES)
            slabs = [qc_ref[:, (kv * pairs + i) * LANES:(kv * pairs + i + 1) * LANES] for i in range(pairs)]
            _, s2 = group_scores(slabs, kv, None, kc_ref[:, ksl], None)
            outs = group_finish(kv, None, s2, None, vc_ref[:, ksl])
            for i in range(pairs):
                oc_ref[:, (kv * pairs + i) * LANES:(kv * pairs + i + 1) * LANES] = outs[i].astype(BF16)
    else:
        oc_ref[...] = jnp.zeros_like(oc_ref)


def _swa_attention(pb, sink, *, n_batch, seq, ctx, need_ctx, col0):
    hw = SWA_HEADS * HEAD_DIM
    kw = SWA_KV_HEADS * LANES
    qcol = col0 // hw
    kcol = (col0 + hw) // kw
    vcol = kcol + 1
    cblk = n_batch * seq // ctx
    kern = functools.partial(_swa_kernel, seq=seq, need_ctx=need_ctx)
    return pl.pallas_call(
        kern,
        out_shape=(jax.ShapeDtypeStruct((n_batch * seq, hw), BF16),
                   jax.ShapeDtypeStruct((n_batch * ctx, hw), BF16)),
        grid=(n_batch,),
        in_specs=[pl.BlockSpec(memory_space=pltpu.SMEM),
                  pl.BlockSpec((seq, hw), lambda b: (b, qcol)),
                  pl.BlockSpec((seq, kw), lambda b: (b, kcol)),
                  pl.BlockSpec((seq, kw), lambda b: (b, vcol)),
                  pl.BlockSpec((ctx, hw), lambda b: (cblk + b, qcol)),
                  pl.BlockSpec((ctx, kw), lambda b: (cblk + b, kcol)),
                  pl.BlockSpec((ctx, kw), lambda b: (cblk + b, vcol))],
        out_specs=(pl.BlockSpec((seq, hw), lambda b: (b, 0)),
                   pl.BlockSpec((ctx, hw), lambda b: (b, 0))),
        compiler_params=_cparams(("arbitrary",)),
        name="swa_attn",
    )(sink, pb, pb, pb, pb, pb, pb)


def _mla_project(cq, ckv, kr, qn_ref, kn_ref, wq_ref, wqs_ref, wk_ref, wv_ref, cq_ref, sq_ref, q_ref, k_ref, v_ref):
    kr2 = jnp.concatenate([kr, kr], axis=1)
    cqn = _rms(cq, qn_ref[...]).astype(BF16)
    ckvn = _rms(ckv, kn_ref[...]).astype(BF16)
    cw = 2 * LANES
    for c in range(MLA_HEADS * LANES // cw):
        sl = slice(c * cw, (c + 1) * cw)
        q = _dot(cqn, wq_ref[:, sl]) * cq_ref[...] + _dot(cqn, wqs_ref[:, sl]) * sq_ref[...]
        q_ref[:, sl] = q.astype(BF16)
        k_ref[:, sl] = (_dot(ckvn, wk_ref[:, sl]) + kr2).astype(BF16)
    ones_hi = jnp.where(_lane_half_masks(1), 0.0, 1.0).astype(F32)
    ones_hi = jnp.concatenate([ones_hi, ones_hi], axis=1)
    for c in range(MLA_HEADS * LANES // cw):
        sl = slice(c * cw, (c + 1) * cw)
        v_ref[:, sl] = (_dot(ckvn, wv_ref[:, sl]) + ones_hi).astype(BF16)


def _mla_body(q_ref, k_ref, v_ref, kc_ref, vc_ref, o_ref):
    sub = MLA_SUB
    n_sub = q_ref.shape[0] // sub
    lo = _lane_half_masks(sub)
    units = [(r, half) for r in range(n_sub) for half in range(2)]

    def scores(r, half):
        sl = slice(half * LANES, (half + 1) * LANES)
        q = q_ref[r * sub:(r + 1) * sub, sl]
        s1 = None if k_ref is None else _dot_t(q, k_ref[:, sl])
        return s1, _dot_t(q, kc_ref[:, sl])

    def finish(half, s1, s2):
        sl = slice(half * LANES, (half + 1) * LANES)
        if s1 is None:
            m = jnp.max(s2, axis=-1, keepdims=True)
            return _dot(jnp.exp2(s2 - m).astype(BF16), vc_ref[:, sl])
        m = jnp.maximum(jnp.max(s1, axis=-1, keepdims=True), jnp.max(s2, axis=-1, keepdims=True))
        p1 = jnp.exp2(s1 - m)
        p2 = jnp.exp2(s2 - m)
        return _dot(p1.astype(BF16), v_ref[:, sl]) + _dot(p2.astype(BF16), vc_ref[:, sl])

    outs = {}
    ahead = MLA_AHEAD
    sc = {u: scores(*u) for u in units[:ahead]}
    for idx, u in enumerate(units):
        if idx + ahead < len(units):
            sc[units[idx + ahead]] = scores(*units[idx + ahead])
        outs[u] = finish(u[1], *sc.pop(u))
    for r in range(n_sub):
        o0, o1 = outs[(r, 0)], outs[(r, 1)]
        r0 = pltpu.roll(o0, HEAD_DIM, 1)
        r1 = pltpu.roll(o1, HEAD_DIM, 1)
        o_ref[r * sub:(r + 1) * sub, :] = jnp.where(lo, o0 / r0, r1 / o1).astype(BF16)


def _mla_kernel(q_ref, k_ref, v_ref, kc_ref, vc_ref, o_ref):
    _mla_body(q_ref, k_ref, v_ref, kc_ref, vc_ref, o_ref)


def _mla_ctx_kernel(q_ref, kc_ref, vc_ref, o_ref):
    _mla_body(q_ref, None, None, kc_ref, vc_ref, o_ref)


def _mla_attention(qm, km, vm, *, n_batch, seq, ctx, need_ctx):
    n_qt = seq // MLA_TQ
    cblk = n_batch * seq // ctx
    hv = MLA_HEADS * MLA_V
    y_lat = pl.pallas_call(
        _mla_kernel,
        out_shape=jax.ShapeDtypeStruct((n_batch * seq, hv), BF16),
        grid=(n_batch, MLA_HEADS // 2, n_qt),
        in_specs=[pl.BlockSpec((MLA_TQ, 2 * LANES), lambda b, p, i: (b * n_qt + i, p)),
                  pl.BlockSpec((seq, 2 * LANES), lambda b, p, i: (b, p)),
                  pl.BlockSpec((seq, 2 * LANES), lambda b, p, i: (b, p)),
                  pl.BlockSpec((ctx, 2 * LANES), lambda b, p, i: (cblk + b, p)),
                  pl.BlockSpec((ctx, 2 * LANES), lambda b, p, i: (cblk + b, p))],
        out_specs=pl.BlockSpec((MLA_TQ, LANES), lambda b, p, i: (b * n_qt + i, p)),
        compiler_params=_cparams(("arbitrary", "arbitrary", "arbitrary")),
        name="mla_attn",
    )(qm, km, vm, km, vm)
    if not need_ctx:
        return y_lat
    assert ctx % MLA_SUB == 0
    y_ctx = pl.pallas_call(
        _mla_ctx_kernel,
        out_shape=jax.ShapeDtypeStruct((n_batch * ctx, hv), BF16),
        grid=(n_batch, MLA_HEADS // 2),
        in_specs=[pl.BlockSpec((ctx, 2 * LANES), lambda b, p: (cblk + b, p)),
                  pl.BlockSpec((ctx, 2 * LANES), lambda b, p: (cblk + b, p)),
                  pl.BlockSpec((ctx, 2 * LANES), lambda b, p: (cblk + b, p))],
        out_specs=pl.BlockSpec((ctx, LANES), lambda b, p: (b, p)),
        compiler_params=_cparams(("arbitrary", "arbitrary")),
        name="mla_ctx_attn",
    )(qm, km, vm)
    return y_lat, y_ctx


def _s5_scan_kernel(uf0_ref, uf1_ref, uf2_ref, uf3_ref, ur0_ref, ur1_ref, ur2_ref, ur3_ref,
                    bmat_ref, cmat_ref, lam_ref, of_ref, or_ref,
                    h_ref, stage_ref, lhs_ref, bu_ref, st_ref, ysc_ref, ya_ref, yb_ref):
    c = pl.program_id(0)
    n_seq = SUBLANES
    half = n_seq // 2
    steps = S5_STEPS
    pitch = S5_PITCH
    rows = steps * n_seq
    nblk = h_ref.shape[0]
    sw = h_ref.shape[2] // 2
    ufs = (uf0_ref, uf1_ref, uf2_ref, uf3_ref)
    urs = (ur0_ref, ur1_ref, ur2_ref, ur3_ref)
    is_fwd = (lax.broadcasted_iota(jnp.int32, (rows, 1), 0) % n_seq) < half
    sub_fwd = lax.broadcasted_iota(jnp.int32, (n_seq, LANES), 0) < half

    @pl.when(c == 0)
    def _():
        h_ref[...] = jnp.zeros_like(h_ref)

    def project_in(k):
        ks = slice(k * LANES, (k + 1) * LANES)
        z = k % 2
        for b in range(half):
            stage_ref[k, b * pitch:b * pitch + steps, :] = ufs[b][:, ks]
            stage_ref[k, (half + b) * pitch:(half + b) * pitch + steps, :] = urs[b][:, ks]
        for t in range(steps):
            ga = stage_ref[k, pl.ds(t, n_seq, stride=pitch), :]
            gb = stage_ref[k, pl.ds(steps - 1 - t, n_seq, stride=pitch), :]
            lhs_ref[z, t * n_seq:(t + 1) * n_seq, 0:LANES] = jnp.where(sub_fwd, ga, 0.0)
            lhs_ref[z, t * n_seq:(t + 1) * n_seq, LANES:2 * LANES] = jnp.where(sub_fwd, 0.0, gb)
        bu_ref[k] = _dot(lhs_ref[z].astype(BF16), bmat_ref[k])

    def recur(k):
        z = k % 2
        lre = lam_ref[k, :, 0:sw]
        lim = lam_ref[k, :, sw:2 * sw]
        hr = h_ref[k, :, 0:sw]
        hi = h_ref[k, :, sw:2 * sw]
        for t in range(steps):
            rs = slice(t * n_seq, (t + 1) * n_seq)
            hr, hi = (lre * hr - lim * hi + bu_ref[k, rs, 0:sw],
                      lre * hi + lim * hr + bu_ref[k, rs, sw:2 * sw])
            st_ref[z, rs, 0:sw] = hr
            st_ref[z, rs, sw:2 * sw] = hi
        h_ref[k, :, 0:sw] = hr
        h_ref[k, :, sw:2 * sw] = hi

    def read_out(k):
        ks = slice(k * LANES, (k + 1) * LANES)
        z = k % 2
        hr_rows = rows // 2
        for part in range(2):
            rs = slice(part * hr_rows, (part + 1) * hr_rows)
            y2 = _dot(st_ref[z, rs, :].astype(BF16), cmat_ref[k])
            ysc_ref[z, rs, :] = jnp.where(is_fwd[rs], y2[:, 0:LANES], y2[:, LANES:2 * LANES])
        for t in range(steps):
            g = ysc_ref[z, t * n_seq:(t + 1) * n_seq, :]
            ya_ref[k, pl.ds(t, n_seq, stride=pitch), :] = g
            yb_ref[k, pl.ds(steps - 1 - t, n_seq, stride=pitch), :] = g
        for b in range(half):
            of_ref[b, :, ks] = ya_ref[k, b * pitch:b * pitch + steps, :]
            or_ref[b, :, ks] = yb_ref[k, (half + b) * pitch:(half + b) * pitch + steps, :]

    for k in range(nblk):
        project_in(k)
    recur(0)
    for k in range(1, nblk):
        recur(k)
        read_out(k - 1)
    read_out(nblk - 1)


def _s5_scan(pf, bmat, cmat, lam, *, n_batch, seq, ctx, width):
    steps = S5_STEPS
    nblk = width // LANES
    rows = steps * SUBLANES
    sw2 = bmat.shape[-1]
    ncc = ctx // steps
    nlc = seq // steps
    ctx0 = n_batch * nlc

    def fwd_map(b):
        return lambda c: (jnp.where(c < ncc, ctx0 + b * ncc + c, b * nlc + (c - ncc)), 0)

    def rev_map(b):
        return lambda c: (jnp.where(c < ncc, ctx0 + b * ncc + (ncc - 1 - c), b * nlc + (nlc - 1 - (c - ncc))), 0)

    of_map = lambda c: (0, jnp.where(c < ncc, nlc + c, c - ncc), 0)
    or_map = lambda c: (0, jnp.where(c < ncc, nlc + (ncc - 1 - c), nlc - 1 - (c - ncc)), 0)
    u_specs = ([pl.BlockSpec((steps, width), fwd_map(b)) for b in range(n_batch)]
               + [pl.BlockSpec((steps, width), rev_map(b)) for b in range(n_batch)])
    out_sds = jax.ShapeDtypeStruct((n_batch, seq + ctx, width), F32)
    stage = pltpu.VMEM((nblk, SUBLANES * S5_PITCH, LANES), F32)
    return pl.pallas_call(
        _s5_scan_kernel,
        out_shape=(out_sds, out_sds),
        grid=(ncc + nlc,),
        in_specs=u_specs + [_const_spec(bmat.shape), _const_spec(cmat.shape), _const_spec(lam.shape)],
        out_specs=(pl.BlockSpec((n_batch, steps, width), of_map),
                   pl.BlockSpec((n_batch, steps, width), or_map)),
        scratch_shapes=[pltpu.VMEM((nblk, SUBLANES, sw2), F32),
                        stage, pltpu.VMEM((2, rows, 2 * LANES), F32),
                        pltpu.VMEM((nblk, rows, sw2), F32), pltpu.VMEM((2, rows, sw2), F32),
                        pltpu.VMEM((2, rows, LANES), F32), stage, stage],
        compiler_params=_cparams(("arbitrary",)),
        name="s5_scan",
    )(*([pf] * (2 * n_batch)), bmat, cmat, lam)


def _s5_post_kernel(pf_ref, yf_ref, yr_ref, d_ref, w_ref, b_ref, o_ref, *, width):
    u = pf_ref[:, 0:width]
    y = d_ref[...] * u + yf_ref[0] + yr_ref[0]
    k0 = math.sqrt(2.0 / math.pi)
    g = 0.5 * y * (1.0 + jnp.tanh(k0 * (y + 0.044715 * (y * y * y))))
    z = _dot(g.astype(BF16), w_ref[...]) + b_ref[...]
    o_ref[...] = (g * jax.nn.sigmoid(z)).astype(BF16)


def _s5_post(pf, yf, yr, d_skip, glu_w, glu_b, *, n_batch, seq, ctx, need_ctx):
    width = yf.shape[2]
    kern = functools.partial(_s5_post_kernel, width=width)
    consts = [_const_spec((1, width)), _const_spec(glu_w.shape), _const_spec((1, width))]

    def call(tp, n_tiles, row_block0, seq_block0, n_out, name):
        return pl.pallas_call(
            kern,
            out_shape=jax.ShapeDtypeStruct((n_out, width), BF16),
            grid=(n_batch, n_tiles),
            in_specs=[pl.BlockSpec((tp, width), lambda b, j: (row_block0 + b * n_tiles + j, 0)),
                      pl.BlockSpec((1, tp, width), lambda b, j: (b, seq_block0 + j, 0)),
                      pl.BlockSpec((1, tp, width), lambda b, j: (b, seq_block0 + j, 0))] + consts,
            out_specs=pl.BlockSpec((tp, width), lambda b, j: (b * n_tiles + j, 0)),
            compiler_params=_cparams(("arbitrary", "arbitrary")),
            name=name,
        )(pf, yf, yr, d_skip, glu_w, glu_b)

    tp = min(S5_POST_ROWS, seq)
    y_lat = call(tp, seq // tp, 0, 0, n_batch * seq, "s5_post")
    if not need_ctx:
        return y_lat
    return y_lat, call(ctx, 1, n_batch * seq // ctx, seq // ctx, n_batch * ctx, "s5_post_ctx")


def _merge_kernel(x_ref, mod_ref, nw_ref, *rest, has_tail, n_first):
    y_refs = rest[:N_BRANCH + sum(has_tail)]
    wg_ref, wb_ref, wo_ref, o_ref, acc_ref = rest[len(y_refs):]
    x = x_ref[...]
    d = x.shape[1]
    m = mod_ref[0]
    n = _rms_mod(x, nw_ref[...], m[3:4], m[4:5]).astype(BF16)
    ys = []
    pos = 0
    for b in range(N_BRANCH):
        y = y_refs[pos][...]
        pos += 1
        if has_tail[b]:
            y = jnp.where(pl.program_id(0) < n_first, y, y_refs[pos][...])
            pos += 1
        ys.append(y)
    for b in range(N_BRANCH):
        gate = jax.nn.sigmoid(_dot(n, wg_ref[:, b * d:(b + 1) * d]))
        contrib = gate * _dot(ys[b], wb_ref[b])
        if b == 0:
            acc_ref[...] = contrib
        else:
            acc_ref[...] += contrib
    o_ref[...] = x + m[5:6] * _dot(acc_ref[...].astype(BF16), wo_ref[...])


def _merge(h, n_rows, mod, norm_w, ys, wg, wb, wo, *, n_lat_tiles, tiles_per_batch, n_batch):
    d = h.shape[1]
    bw = wb.shape[1]
    n_first = n_lat_tiles
    has_tail = tuple(isinstance(y, tuple) for y in ys)
    y_specs, y_ops = [], []
    for y in ys:
        if isinstance(y, tuple):
            y_specs += [pl.BlockSpec((TM, bw), lambda i: (jnp.minimum(i, n_first - 1), 0)),
                        pl.BlockSpec((TM, bw), lambda i: (jnp.maximum(i - n_first, 0), 0))]
            y_ops += list(y)
        else:
            y_specs.append(pl.BlockSpec((TM, bw), lambda i: (i, 0)))
            y_ops.append(y)
    kern = functools.partial(_merge_kernel, has_tail=has_tail, n_first=n_first)
    return pl.pallas_call(
        kern,
        out_shape=jax.ShapeDtypeStruct((n_rows, d), F32),
        grid=(n_rows // TM,),
        in_specs=[pl.BlockSpec((TM, d), lambda i: (i, 0)),
                  pl.BlockSpec((1, N_MOD, d), lambda i: (_mod_row_map(n_lat_tiles, tiles_per_batch, n_batch)(i), 0, 0)),
                  _const_spec((1, d))]
                 + y_specs
                 + [_const_spec(wg.shape), _const_spec(wb.shape), _const_spec(wo.shape)],
        out_specs=pl.BlockSpec((TM, d), lambda i: (i, 0)),
        scratch_shapes=[pltpu.VMEM((TM, d), F32)],
        compiler_params=_cparams(("arbitrary",)),
        name="merge",
    )(h, mod, norm_w, *y_ops, wg, wb, wo)


def _swap_rot_pairs(w, nf):
    lead = w.shape[:-1]
    n = w.shape[-1]
    return w.reshape(lead + (n // (2 * nf), 2, nf))[..., ::-1, :].reshape(lead + (n,))


def _rope_tables(seq, dim, lane_off, width, period):
    nf = dim // 4
    pos = jnp.arange(seq)
    rows = (pos // GRID_W).astype(F32)
    cols = (pos % GRID_W).astype(F32)
    inv_freq = ROPE_BASE ** (-jnp.arange(nf, dtype=F32) / nf)
    ang_r = rows[:, None] * inv_freq[None, :]
    ang_c = cols[:, None] * inv_freq[None, :]
    cos = jnp.concatenate([jnp.cos(ang_r)] * 2 + [jnp.cos(ang_c)] * 2, axis=1)
    sin = jnp.concatenate([-jnp.sin(ang_r), jnp.sin(ang_r), -jnp.sin(ang_c), jnp.sin(ang_c)], axis=1)
    c_per = jnp.ones((seq, period), F32).at[:, lane_off:lane_off + dim].set(cos)
    s_per = jnp.zeros((seq, period), F32).at[:, lane_off:lane_off + dim].set(sin)
    reps = width // period
    c_tab = jnp.concatenate([jnp.tile(c_per, (1, reps)), jnp.ones((TM, width), F32)], axis=0)
    s_tab = jnp.concatenate([jnp.tile(s_per, (1, reps)), jnp.zeros((TM, width), F32)], axis=0)
    return c_tab, s_tab


def _inproj_weights(w_in):
    d = w_in.shape[0]
    hw = NA_HEADS * HEAD_DIM
    o_sq = 3 * hw
    o_sk = o_sq + SWA_HEADS * HEAD_DIM
    o_sv = o_sk + SWA_KV_HEADS * HEAD_DIM
    o_s5 = o_sv + SWA_KV_HEADS * HEAD_DIM
    s5w = 512
    o_cq = o_s5 + s5w
    o_ckv = o_cq + 256
    o_kr = o_ckv + 128
    o_g = o_kr + MLA_ROPE
    qscale = LOG2E * HEAD_DIM ** -0.5
    na = jnp.concatenate([w_in[:, :hw] * qscale, w_in[:, hw:3 * hw]], axis=1)
    sq = w_in[:, o_sq:o_sk] * qscale
    sk = w_in[:, o_sk:o_sv]
    sv = w_in[:, o_sv:o_s5]
    def dup_heads(a):
        return jnp.concatenate([a[:, kv * HEAD_DIM:(kv + 1) * HEAD_DIM]
                                for kv in range(SWA_KV_HEADS) for _ in range(2)], axis=1)

    sk_dup = dup_heads(sk)
    sv_dup = dup_heads(sv)
    wr = jnp.concatenate([sq, sk_dup], axis=1)
    kr = w_in[:, o_kr:o_g]
    lpad = jnp.zeros((d, MLA_NOPE), F32)
    rpad = jnp.zeros((d, LANES - MLA_NOPE - MLA_ROPE), F32)
    wf = jnp.concatenate([w_in[:, o_s5:o_kr], lpad, kr, rpad], axis=1)
    wg = w_in[:, o_g:]
    proj = tuple(a.astype(BF16) for a in (na, wr, sv_dup, wf))
    return proj, wg.astype(BF16)


def _mla_weights(w_uq, w_ukv):
    ql = w_uq.shape[0]
    kvl = w_ukv.shape[0]
    dq = MLA_NOPE + MLA_ROPE
    wq3 = w_uq.reshape(ql, MLA_HEADS, dq)
    pad = jnp.zeros((ql, MLA_HEADS, LANES - dq), F32)
    qscale = math.log2(math.e) * dq ** -0.5
    wq = jnp.concatenate([wq3 * qscale, pad], axis=2).reshape(ql, MLA_HEADS * LANES)
    wkv3 = w_ukv.reshape(kvl, MLA_HEADS, MLA_NOPE + MLA_V)
    wk = jnp.concatenate([wkv3[:, :, :MLA_NOPE], jnp.zeros((kvl, MLA_HEADS, LANES - MLA_NOPE), F32)], axis=2)
    wk = wk.reshape(kvl, MLA_HEADS * LANES)
    wv = jnp.concatenate([wkv3[:, :, MLA_NOPE:], jnp.zeros((kvl, MLA_HEADS, LANES - MLA_V), F32)], axis=2)
    wv = wv.reshape(kvl, MLA_HEADS * LANES)
    rope_sw = _swap_rot_pairs(wq3[:, :, MLA_NOPE:], MLA_ROPE // 4) * qscale
    wqs = jnp.concatenate([jnp.zeros((ql, MLA_HEADS, MLA_NOPE), F32), rope_sw, pad], axis=2)
    wqs = wqs.reshape(ql, MLA_HEADS * LANES)
    return tuple(a.astype(BF16) for a in (wq, wqs, wk, wv))


def _s5_params(lam_re, lam_im, log_dt, b_re, b_im, c_re, c_im):
    a = lam_re.astype(F32)
    w = lam_im.astype(F32)
    dt = jnp.exp(log_dt.astype(F32))[..., None]
    mag = jnp.exp(a * dt)
    lb_re = mag * jnp.cos(w * dt)
    lb_im = mag * jnp.sin(w * dt)
    den = a * a + w * w
    cf_re = ((lb_re - 1.0) * a + lb_im * w) / den
    cf_im = (lb_im * a - (lb_re - 1.0) * w) / den
    bb_re = cf_re[..., None] * b_re - cf_im[..., None] * b_im
    bb_im = cf_re[..., None] * b_im + cf_im[..., None] * b_re
    n_dir, g, p, cg = b_re.shape
    gpb = LANES // cg
    nblk = g // gpb
    eye = jnp.eye(gpb, dtype=F32)

    def in_map(x):
        x5 = jnp.swapaxes(x.reshape(n_dir, nblk, gpb, p, cg), 3, 4)
        full = x5[:, :, :, :, None, :] * eye[None, None, :, None, :, None]
        return full.reshape(n_dir, nblk, gpb * cg, gpb * p)

    def out_map(x):
        x5 = jnp.swapaxes(x.reshape(n_dir, nblk, gpb, cg, p), 3, 4)
        full = x5[:, :, :, :, None, :] * eye[None, None, :, None, :, None]
        return full.reshape(n_dir, nblk, gpb * p, gpb * cg)

    b_in = jnp.concatenate([in_map(bb_re), in_map(bb_im)], axis=3)
    bmat = jnp.concatenate([b_in[0], b_in[1]], axis=1).astype(BF16)
    c_out = jnp.concatenate([out_map(c_re.astype(F32)), out_map(-c_im.astype(F32))], axis=2)
    cmat = jnp.concatenate([c_out[0], c_out[1]], axis=2).astype(BF16)
    half = SUBLANES // 2
    lam2 = jnp.concatenate([lb_re.reshape(n_dir, nblk, gpb * p), lb_im.reshape(n_dir, nblk, gpb * p)], axis=2)
    lam = jnp.concatenate([jnp.broadcast_to(lam2[0][:, None, :], (nblk, half, 2 * gpb * p)),
                           jnp.broadcast_to(lam2[1][:, None, :], (nblk, half, 2 * gpb * p))], axis=1)
    return bmat, cmat, lam


def _ffn_weights(wg, wu, wd):
    d, ff = wg.shape
    wg3 = wg.astype(BF16)
    wu3 = wu.astype(BF16)
    wd3 = wd.astype(BF16)
    return wg3, wu3, wd3


def kernel(x, c, ctx, c_ctx, ada_w, ada_b, ffn1_norm, ffn1_w_gate, ffn1_w_up, ffn1_w_down, mix_norm, w_in, na_rpb, swa_sink, s5_lambda_re, s5_lambda_im, s5_log_dt, s5_b_re, s5_b_im, s5_c_re, s5_c_im, s5_d, s5_glu_w, s5_glu_b, mla_q_norm, mla_w_uq, mla_kv_norm, mla_w_ukv, w_branch, w_out, ffn2_norm, ffn2_w_gate, ffn2_w_up, ffn2_w_down, final_norm):
    n_batch, seq, d = x.shape
    n_ctx = ctx.shape[1]
    depth = ada_w.shape[0]
    assert 2 * n_batch == SUBLANES and seq % TM == 0 and (n_batch * n_ctx) % TM == 0
    n_lat = n_batch * seq
    n_all = n_lat + n_batch * n_ctx
    tiles_per_batch = seq // TM
    n_lat_tiles = n_lat // TM
    geo = dict(n_lat_tiles=n_lat_tiles, tiles_per_batch=tiles_per_batch, n_batch=n_batch)

    h = x.reshape(n_lat, d)
    h_ctx = ctx.reshape(n_batch * n_ctx, d)
    cc = jnp.concatenate([c, c_ctx[None, :], jnp.zeros((SUBLANES - n_batch - 1, d), F32)], axis=0)
    mod = _ada_mod(cc, ada_w, ada_b)

    cs_sw, sn_sw = _rope_tables(seq, HEAD_DIM, 0, 2 * LANES, HEAD_DIM)
    ck_kr, sk_kr = _rope_tables(seq, MLA_ROPE, LANES + MLA_NOPE, 2 * LANES, 2 * LANES)
    cq_ml, sq_ml = _rope_tables(seq, MLA_ROPE, MLA_NOPE, 2 * LANES, LANES)
    rows_n = seq // GRID_W
    s5w = s5_d.shape[1]
    sw_col0 = 3 * NA_HEADS * HEAD_DIM

    w_ffn1 = _ffn_weights(ffn1_w_gate[0], ffn1_w_up[0], ffn1_w_down[0])
    for l in range(depth):
        need_ctx = l < depth - 1
        last = l == depth - 1
        ml = mod[l]
        ones = jnp.ones((1, d), F32)
        h, *w_ffn2 = _ffn(h, n_all, ml, ffn1_norm[l][None, :], *w_ffn1, ones, base=0, final=False,
                          h_tail=h_ctx if l == 0 else None,
                          cast_next=(l, ffn2_w_gate, ffn2_w_up, ffn2_w_down), **geo)
        proj_w, gate_w = _inproj_weights(w_in[l])
        mla_w = _mla_weights(mla_w_uq[l], mla_w_ukv[l])
        pb, pf, qm, km, vm = _inproj(h, ml, mix_norm[l][None, :], proj_w, (cs_sw, sn_sw, ck_kr, sk_kr),
                                     (mla_q_norm[l][None, :], mla_kv_norm[l][None, :], *mla_w, cq_ml, sq_ml), **geo)
        bias = _na_bias_table(na_rpb[l].astype(F32), rows_n)
        y_na = _na_attention(pb, bias, n_batch=n_batch, seq=seq, ctx=n_ctx, need_ctx=need_ctx)
        y_sw_l, y_sw_c = _swa_attention(pb, swa_sink[l].astype(F32) * LOG2E, n_batch=n_batch, seq=seq, ctx=n_ctx,
                                        need_ctx=need_ctx, col0=sw_col0)
        y_sw = (y_sw_l, y_sw_c) if need_ctx else y_sw_l
        y_mla = _mla_attention(qm, km, vm, n_batch=n_batch, seq=seq, ctx=n_ctx, need_ctx=need_ctx)
        s5p = _s5_params(s5_lambda_re[l], s5_lambda_im[l], s5_log_dt[l], s5_b_re[l], s5_b_im[l],
                         s5_c_re[l], s5_c_im[l])
        yf, yr = _s5_scan(pf, *s5p, n_batch=n_batch, seq=seq, ctx=n_ctx, width=s5w)
        y_s5 = _s5_post(pf, yf, yr, s5_d[l][None, :].astype(F32), s5_glu_w[l].astype(BF16),
                        s5_glu_b[l][None, :].astype(F32), n_batch=n_batch, seq=seq, ctx=n_ctx, need_ctx=need_ctx)
        n_rows = n_all if need_ctx else n_lat
        h = _merge(h, n_rows, ml, mix_norm[l][None, :], (y_na, y_sw, y_s5, y_mla), gate_w,
                   w_branch[l].astype(BF16), w_out[l].astype(BF16), **geo)
        if last:
            h = _ffn(h, n_rows, ml, ffn2_norm[l][None, :], *w_ffn2, final_norm[None, :], base=6, final=True, **geo)
        else:
            h, *w_ffn1 = _ffn(h, n_rows, ml, ffn2_norm[l][None, :], *w_ffn2, final_norm[None, :], base=6, final=False,
                              cast_next=(l + 1, ffn1_w_gate, ffn1_w_up, ffn1_w_down), **geo)
    return h.reshape(n_batch, seq, d)
```

```python
import functools
import math

import numpy as np
import jax
import jax.numpy as jnp
from jax import lax
from jax.experimental import pallas as pl
from jax.experimental.pallas import tpu as pltpu

F32 = jnp.float32
BF16 = jnp.bfloat16

GRID_W = 64
HEAD_DIM = 64
N_BRANCH = 4
NA_HEADS = 8
NA_WIN_ROWS = 8
NA_WIN_COLS = 16
SWA_HEADS = 8
SWA_KV_HEADS = 2
SWA_WINDOW = 128
S5_GROUP = 16
S5_STATE = 64
MLA_HEADS = 8
MLA_NOPE = 64
MLA_ROPE = 32
MLA_V = 64
MACARON_WEIGHT = 0.5
ROPE_BASE = 10000.0
EPS = 1e-6
N_MOD = 9

LANES = 128
SUBLANES = 8
TM = 512
FF_CHUNK = 256
NA_QROWS = 4
NA_KROWS = NA_QROWS + NA_WIN_ROWS - 1
SWA_BLK = 128
SWA_UNROLL = 4
MLA_TQ = 1024
MLA_SUB = 256
MLA_AHEAD = 1
S5_STEPS = 128
S5_POST_ROWS = 1024
S5_PITCH = S5_STEPS + SUBLANES
NEG = -1e30
LOG2E = math.log2(math.e)
VMEM_LIMIT = 56 * 1024 * 1024


def _cparams(sem, flags=None):
    return pltpu.CompilerParams(dimension_semantics=sem, vmem_limit_bytes=VMEM_LIMIT, flags=flags)


def _const_spec(shape):
    nd = len(shape)
    return pl.BlockSpec(shape, lambda *_: (0,) * nd, pipeline_mode=pl.Buffered(1))


def _dot(a, b):
    return jnp.dot(a, b, preferred_element_type=F32)


def _dot_t(a, b):
    return lax.dot_general(a, b, (((1,), (1,)), ((), ())), preferred_element_type=F32)


def _rms(x, w):
    return x * lax.rsqrt(jnp.mean(x * x, axis=-1, keepdims=True) + EPS) * w


def _rms_mod(x, w, shift, scale):
    return _rms(x, w) * (1.0 + scale) + shift


def _mod_row_map(n_lat_tiles, tiles_per_batch, n_batch):
    def f(i):
        return jnp.where(i < n_lat_tiles, i // tiles_per_batch, n_batch)
    return f


def _ada_kernel(c_ref, w_ref, b_ref, o_ref):
    c = c_ref[...]
    s = c * jax.nn.sigmoid(c)
    w = w_ref[0]
    s_hi = s.astype(BF16)
    s_lo = (s - s_hi.astype(F32)).astype(BF16)
    w_hi = w.astype(BF16)
    w_lo = (w - w_hi.astype(F32)).astype(BF16)
    o_ref[0] = _dot(s_hi, w_hi) + (_dot(s_hi, w_lo) + _dot(s_lo, w_hi)) + b_ref[0]


def _ada_mod(cc, ada_w, ada_b):
    depth, d, nd = ada_w.shape
    tn = 1024
    out = pl.pallas_call(
        _ada_kernel,
        out_shape=jax.ShapeDtypeStruct((depth, SUBLANES, nd), F32),
        grid=(depth, nd // tn),
        in_specs=[pl.BlockSpec((SUBLANES, d), lambda l, j: (0, 0)),
                  pl.BlockSpec((1, d, tn), lambda l, j: (l, 0, j)),
                  pl.BlockSpec((1, 1, tn), lambda l, j: (l, 0, j))],
        out_specs=pl.BlockSpec((1, SUBLANES, tn), lambda l, j: (l, 0, j)),
        compiler_params=_cparams(("arbitrary", "arbitrary")),
        name="ada_mod",
    )(cc, ada_w, ada_b.reshape(depth, 1, nd))
    return out.reshape(depth, SUBLANES, N_MOD, d)


def _ffn_kernel(x_ref, xc_ref, mod_ref, nw_ref, wg_ref, wu_ref, wd_ref, fw_ref, *rest, base, final, n_first):
    if len(rest) == 1:
        (o_ref,) = rest
    else:
        ng_ref, nu_ref, nd_ref, o_ref, cg_ref, cu_ref, cd_ref = rest
        cg_ref[...] = ng_ref[...].astype(BF16)
        cu_ref[...] = nu_ref[...].astype(BF16)
        cd_ref[...] = nd_ref[...].astype(BF16)
    x = x_ref[...]
    if n_first is not None:
        x = jnp.where(pl.program_id(0) < n_first, x, xc_ref[...])
    m = mod_ref[0]
    n = _rms_mod(x, nw_ref[...], m[base:base + 1], m[base + 1:base + 2]).astype(BF16)
    ff = wg_ref.shape[1]
    fc = FF_CHUNK if ff % FF_CHUNK == 0 else ff
    acc = None
    for c in range(ff // fc):
        cs = slice(c * fc, (c + 1) * fc)
        g = _dot(n, wg_ref[:, cs])
        u = _dot(n, wu_ref[:, cs])
        a = (g * jax.nn.sigmoid(g) * u).astype(BF16)
        y = _dot(a, wd_ref[cs, :])
        acc = y if acc is None else acc + y
    out = x + MACARON_WEIGHT * m[base + 2:base + 3] * acc
    if final:
        out = _rms(out, fw_ref[...])
    o_ref[...] = out


def _ffn(h, n_rows, mod, norm_w, wg, wu, wd, final_w, *, base, final, n_lat_tiles, tiles_per_batch, n_batch,
         h_tail=None, cast_next=None):
    d = h.shape[1]
    ff = wg.shape[1]
    steps = n_rows // TM
    if h_tail is None:
        n_first = None
        h_tail = h
        x_map = lambda i: (i, 0)
        t_map = lambda i: (0, 0)
    else:
        n_first = h.shape[0] // TM
        x_map = lambda i: (jnp.minimum(i, n_first - 1), 0)
        t_map = lambda i: (jnp.maximum(i - n_first, 0), 0)
    kern = functools.partial(_ffn_kernel, base=base, final=final, n_first=n_first)
    in_specs = [pl.BlockSpec((TM, d), x_map),
                pl.BlockSpec((TM, d) if n_first is not None else (SUBLANES, d), t_map),
                pl.BlockSpec((1, N_MOD, d), lambda i: (_mod_row_map(n_lat_tiles, tiles_per_batch, n_batch)(i), 0, 0)),
                _const_spec((1, d)),
                _const_spec((d, ff)), _const_spec((d, ff)), _const_spec((ff, d)),
                _const_spec((1, d))]
    out_shape = jax.ShapeDtypeStruct((n_rows, d), F32)
    out_specs = pl.BlockSpec((TM, d), lambda i: (i, 0))
    operands = [h, h_tail, mod, norm_w, wg, wu, wd, final_w]
    if cast_next is not None:
        up_rows = min(r for r in range(2 * SUBLANES, d + 1, 2 * SUBLANES) if d % r == 0 and d // r <= steps)
        dn_rows = min(r for r in range(2 * SUBLANES, ff + 1, 2 * SUBLANES) if ff % r == 0 and ff // r <= steps)
        n_up, n_dn = d // up_rows, ff // dn_rows
        up_spec = pl.BlockSpec((up_rows, ff), lambda i: (jnp.minimum(i, n_up - 1), 0))
        dn_spec = pl.BlockSpec((dn_rows, d), lambda i: (jnp.minimum(i, n_dn - 1), 0))
        nl, *next_w = cast_next
        in_specs += [pl.BlockSpec((None, up_rows, ff), lambda i: (nl, jnp.minimum(i, n_up - 1), 0))] * 2
        in_specs += [pl.BlockSpec((None, dn_rows, d), lambda i: (nl, jnp.minimum(i, n_dn - 1), 0))]
        operands += next_w
        out_shape = (out_shape, jax.ShapeDtypeStruct((d, ff), BF16), jax.ShapeDtypeStruct((d, ff), BF16),
                     jax.ShapeDtypeStruct((ff, d), BF16))
        out_specs = (out_specs, up_spec, up_spec, dn_spec)
    return pl.pallas_call(
        kern,
        out_shape=out_shape,
        grid=(steps,),
        in_specs=in_specs,
        out_specs=out_specs,
        compiler_params=_cparams(("arbitrary",)),
        name="ffn_final" if final else "ffn",
    )(*operands)


def _rope_apply(y, cs, sn, nf):
    w = y.shape[1]
    first = (lax.broadcasted_iota(jnp.int32, y.shape, 1) & nf) == 0
    ysw = jnp.where(first, pltpu.roll(y, w - nf, 1), pltpu.roll(y, nf, 1))
    return y * cs + ysw * sn


def _inproj_kernel(x_ref, mod_ref, nw_ref, wa_ref, wr_ref, wv_ref, wf_ref,
                   cs_ref, sn_ref, ck_ref, sk_ref,
                   qn_ref, kn_ref, wq_ref, wqs_ref, wk_ref, wvm_ref, cq_ref, sq_ref,
                   ob_ref, of_ref, q_ref, k_ref, v_ref):
    x = x_ref[...]
    m = mod_ref[0]
    n = _rms_mod(x, nw_ref[...], m[3:4], m[4:5]).astype(BF16)
    na = wa_ref.shape[1]
    nr = wr_ref.shape[1]
    nv = wv_ref.shape[1]
    nf = wf_ref.shape[1]
    cw = 2 * LANES
    cq = _dot(n, wf_ref[:, nf - 2 * cw:nf - cw])
    y = _rope_apply(_dot(n, wf_ref[:, nf - cw:nf]), ck_ref[...], sk_ref[...], MLA_ROPE // 4)
    for c in range(nf // cw - 2):
        of_ref[:, c * cw:(c + 1) * cw] = _dot(n, wf_ref[:, c * cw:(c + 1) * cw])
    _mla_project(cq, y[:, :LANES], y[:, LANES:], qn_ref, kn_ref, wq_ref, wqs_ref, wk_ref, wvm_ref,
                 cq_ref, sq_ref, q_ref, k_ref, v_ref)
    for c in range(na // cw):
        ob_ref[:, c * cw:(c + 1) * cw] = _dot(n, wa_ref[:, c * cw:(c + 1) * cw]).astype(BF16)
    for c in range(nr // cw):
        y = _dot(n, wr_ref[:, c * cw:(c + 1) * cw])
        ob_ref[:, na + c * cw:na + (c + 1) * cw] = _rope_apply(y, cs_ref[...], sn_ref[...], HEAD_DIM // 4).astype(BF16)
    for c in range(nv // cw):
        ob_ref[:, na + nr + c * cw:na + nr + (c + 1) * cw] = _dot(n, wv_ref[:, c * cw:(c + 1) * cw]).astype(BF16)


def _inproj(h, mod, norm_w, w, tabs, mla, *, n_lat_tiles, tiles_per_batch, n_batch):
    n_rows, d = h.shape
    wa, wr, wv, wf = w
    cs, sn, ck, sk = tabs
    qn, kn, wq, wqs, wk, wvm, cq_tab, sq_tab = mla
    cw = 2 * LANES
    assert wq.shape[0] == cw and wk.shape[0] == LANES
    nb = wa.shape[1] + wr.shape[1] + wv.shape[1]
    ns5 = wf.shape[1] - 2 * cw
    hq = MLA_HEADS * LANES
    tab_map = lambda i: (jnp.where(i < n_lat_tiles, i % tiles_per_batch, tiles_per_batch), 0)
    row = lambda width: pl.BlockSpec((TM, width), lambda i: (i, 0))
    return pl.pallas_call(
        _inproj_kernel,
        out_shape=(jax.ShapeDtypeStruct((n_rows, nb), BF16), jax.ShapeDtypeStruct((n_rows, ns5), F32))
                  + (jax.ShapeDtypeStruct((n_rows, hq), BF16),) * 3,
        grid=(n_rows // TM,),
        in_specs=[pl.BlockSpec((TM, d), lambda i: (i, 0)),
                  pl.BlockSpec((1, N_MOD, d), lambda i: (_mod_row_map(n_lat_tiles, tiles_per_batch, n_batch)(i), 0, 0)),
                  _const_spec((1, d))]
                 + [_const_spec(a.shape) for a in w]
                 + [pl.BlockSpec((TM, cw), tab_map)] * 4
                 + [_const_spec(a.shape) for a in (qn, kn, wq, wqs, wk, wvm)]
                 + [pl.BlockSpec((TM, cw), tab_map)] * 2,
        out_specs=(row(nb), row(ns5), row(hq), row(hq), row(hq)),
        compiler_params=_cparams(("arbitrary",)),
        name="inproj",
    )(h, mod, norm_w, *w, cs, sn, ck, sk, qn, kn, wq, wqs, wk, wvm, cq_tab, sq_tab)


def _lane_half_masks(rows):
    lane = lax.broadcasted_iota(jnp.int32, (rows, LANES), 1)
    return lane < HEAD_DIM


def _with_ones(v):
    return jnp.concatenate([v, jnp.ones_like(v)], axis=1)


def _na_kernel(q_ref, k_ref, v_ref, kc_ref, vc_ref, bias_ref, o_ref, *, n_blk, rows_n):
    blk = pl.program_id(1)
    nq = NA_QROWS * GRID_W
    nk = NA_KROWS * GRID_W
    ws = jnp.clip(NA_QROWS * blk - NA_WIN_ROWS // 2, 0, rows_n - NA_KROWS)
    r0 = pl.multiple_of(ws * GRID_W, GRID_W)
    lo = _lane_half_masks(nq)
    n_pairs = NA_HEADS // 2

    def scores(j):
        sl = slice(j * LANES, (j + 1) * LANES)
        qp = q_ref[:, sl]
        zero = jnp.zeros_like(qp)
        qs = jnp.concatenate([jnp.where(lo, qp, zero), jnp.where(lo, zero, qp)], axis=0)
        return _dot_t(qs, k_ref[pl.ds(r0, nk), sl]) + bias_ref[0, j], _dot_t(qs, kc_ref[:, sl])

    def finish(j, s1, s2):
        sl = slice(j * LANES, (j + 1) * LANES)
        m = jnp.maximum(jnp.max(s1, axis=-1, keepdims=True), jnp.max(s2, axis=-1, keepdims=True))
        o2 = (_dot(jnp.exp2(s1 - m).astype(BF16), _with_ones(v_ref[pl.ds(r0, nk), sl]))
              + _dot(jnp.exp2(s2 - m).astype(BF16), _with_ones(vc_ref[:, sl])))
        o = o2[:, :LANES] / o2[:, LANES:]
        o_ref[:, sl] = jnp.where(lo, o[:nq], o[nq:]).astype(BF16)

    pending = scores(0)
    for j in range(n_pairs):
        following = scores(j + 1) if j + 1 < n_pairs else None
        finish(j, *pending)
        pending = following


def _na_attention(pb, bias, *, n_batch, seq, ctx, need_ctx):
    rows_n = seq // GRID_W
    n_blk = rows_n // NA_QROWS
    nq = NA_QROWS * GRID_W
    hw = NA_HEADS * HEAD_DIM
    n_q = n_blk + (1 if need_ctx else 0)
    assert ctx == nq
    lat_blocks = n_batch * seq // nq
    qmap = lambda b, i: (jnp.where(i < n_blk, b * n_blk + i, lat_blocks + b), 0)

    def bias_map(b, i):
        t = jnp.where(i == 0, 0, jnp.where(i == 1, 1, jnp.where(i == n_blk - 1, 3, jnp.where(i == n_blk, 4, 2))))
        return (t, 0, 0, 0)

    kern = functools.partial(_na_kernel, n_blk=n_blk, rows_n=rows_n)
    return pl.pallas_call(
        kern,
        out_shape=jax.ShapeDtypeStruct((n_batch * n_q * nq, hw), BF16),
        grid=(n_batch, n_q),
        in_specs=[pl.BlockSpec((nq, hw), qmap),
                  pl.BlockSpec((seq, hw), lambda b, i: (b, 1)),
                  pl.BlockSpec((seq, hw), lambda b, i: (b, 2)),
                  pl.BlockSpec((ctx, hw), lambda b, i: (n_batch * seq // ctx + b, 1)),
                  pl.BlockSpec((ctx, hw), lambda b, i: (n_batch * seq // ctx + b, 2)),
                  pl.BlockSpec((1, NA_HEADS // 2, 2 * nq, NA_KROWS * GRID_W), bias_map)],
        out_specs=pl.BlockSpec((nq, hw), qmap),
        compiler_params=_cparams(("arbitrary", "arbitrary")),
        name="na_attn",
    )(pb, pb, pb, pb, pb, bias)


def _na_bias_table(rpb, rows_n):
    n_blk = rows_n // NA_QROWS
    kr_n = min(NA_WIN_ROWS, rows_n)
    n_heads = rpb.shape[0]
    col = np.arange(GRID_W)
    c0 = np.clip(col - NA_WIN_COLS // 2, 0, GRID_W - NA_WIN_COLS)
    col_ok = (col[None, :] >= c0[:, None]) & (col[None, :] < c0[:, None] + NA_WIN_COLS)
    n_dc = rpb.shape[2]
    skew = 2 * GRID_W
    vec = jnp.concatenate([rpb[:, :, NA_WIN_COLS - 1:],
                           jnp.zeros(rpb.shape[:2] + (skew - n_dc,), F32),
                           rpb[:, :, :NA_WIN_COLS - 1]], axis=2) * LOG2E
    tiled = jnp.tile(vec, (1, 1, GRID_W))[:, :, :GRID_W * (skew - 1)]
    toep = tiled.reshape(rpb.shape[:2] + (GRID_W, skew - 1))[:, :, :, :GRID_W]
    toep = jnp.where(col_ok[None, None], toep, NEG)
    masked = jnp.full((n_heads, 1, GRID_W, GRID_W), NEG, F32)
    ext = jnp.concatenate([masked, toep, masked], axis=1)
    pair2 = jnp.concatenate([ext[:, :-1], ext[:, 1:]], axis=-1)
    n_dr = 2 * NA_WIN_ROWS - 1
    plans = []
    for blk in (0, 1, 2, n_blk - 1):
        ws = int(np.clip(NA_QROWS * blk - NA_WIN_ROWS // 2, 0, rows_n - NA_KROWS))
        plan = []
        for qr in range(NA_QROWS):
            r = NA_QROWS * blk + qr
            r0 = int(np.clip(r - kr_n // 2, 0, rows_n - kr_n))
            row = []
            for kk in range(0, NA_KROWS, 2):
                oks = tuple(r0 <= ws + kk + i < r0 + kr_n and kk + i < NA_KROWS for i in range(2))
                row.append((ws + kk - r + NA_WIN_ROWS, oks))
            plan.append(tuple(row))
        plans.append(tuple(plan))
    plans.append(None)
    nq = NA_QROWS * GRID_W
    nk = NA_KROWS * GRID_W
    kern = functools.partial(_na_bias_kernel, plans=tuple(plans))
    return pl.pallas_call(
        kern,
        out_shape=jax.ShapeDtypeStruct((len(plans), n_heads // 2, 2 * nq, nk), F32),
        grid=(n_heads // 2,),
        in_specs=[pl.BlockSpec((2, n_dr + 1, GRID_W, 2 * GRID_W), lambda j: (j, 0, 0, 0))],
        out_specs=pl.BlockSpec((len(plans), 1, 2 * nq, nk), lambda j: (0, j, 0, 0)),
        compiler_params=_cparams(("arbitrary",)),
        name="na_bias",
    )(pair2)


def _na_bias_kernel(p2_ref, o_ref, *, plans):
    nq = NA_QROWS * GRID_W
    lo = lax.broadcasted_iota(jnp.int32, (GRID_W, 2 * GRID_W), 1) < GRID_W
    neg = jnp.full((GRID_W, 2 * GRID_W), NEG, F32)
    for t, plan in enumerate(plans):
        if plan is None:
            o_ref[t, 0] = jnp.full(o_ref.shape[2:], NEG, F32)
            continue
        for half in range(2):
            for qr, row in enumerate(plan):
                rs = slice(half * nq + qr * GRID_W, half * nq + (qr + 1) * GRID_W)
                for kp, (e, (ok_a, ok_b)) in enumerate(row):
                    width = min(2 * GRID_W, o_ref.shape[3] - kp * 2 * GRID_W)
                    if ok_a or ok_b:
                        tile = p2_ref[half, e]
                        if not ok_a:
                            tile = jnp.where(lo, neg, tile)
                        if not ok_b:
                            tile = jnp.where(lo, tile, neg)
                    else:
                        tile = neg
                    o_ref[t, 0, rs, kp * 2 * GRID_W:kp * 2 * GRID_W + width] = tile[:, :width]


def _swa_kernel(sink_ref, q_ref, k_ref, v_ref, qc_ref, kc_ref, vc_ref, o_ref, oc_ref, *, seq, need_ctx):
    n_blk = seq // SWA_BLK
    band = 3 * SWA_BLK
    group = SWA_HEADS // SWA_KV_HEADS
    pairs = group // 2

    def group_scores(q_slabs, kv, kb, kc, mask_bias):
        r = q_slabs[0].shape[0]
        lo = _lane_half_masks(r)
        parts = []
        for qp in q_slabs:
            zero = jnp.zeros_like(qp)
            parts += [jnp.where(lo, qp, zero), jnp.where(lo, zero, qp)]
        qs = jnp.concatenate(parts, axis=0)
        s1 = None if kb is None else _dot_t(qs, kb) + mask_bias
        return s1, _dot_t(qs, kc)

    def group_finish(kv, s1, s2, vb, vc):
        r = s2.shape[0] // group
        lo = _lane_half_masks(r)
        row = lax.broadcasted_iota(jnp.int32, (group * r, 1), 0)
        sink = jnp.full((group * r, 1), sink_ref[kv * group + group - 1], F32)
        for g in range(group - 2, -1, -1):
            sink = jnp.where(row < (g + 1) * r, sink_ref[kv * group + g], sink)
        if s1 is None:
            m = jnp.maximum(jnp.max(s2, axis=-1, keepdims=True), sink)
            o2 = _dot(jnp.exp2(s2 - m).astype(BF16), _with_ones(vc))
        else:
            m = jnp.maximum(jnp.maximum(jnp.max(s1, axis=-1, keepdims=True), jnp.max(s2, axis=-1, keepdims=True)), sink)
            o2 = (_dot(jnp.exp2(s1 - m).astype(BF16), _with_ones(vb))
                  + _dot(jnp.exp2(s2 - m).astype(BF16), _with_ones(vc)))
        o = o2[:, :LANES] / (o2[:, LANES:] + jnp.exp2(sink - m))
        return [jnp.where(lo, o[(2 * i) * r:(2 * i + 1) * r], o[(2 * i + 1) * r:(2 * i + 2) * r]) for i in range(pairs)]

    qi = lax.broadcasted_iota(jnp.int32, (group * SWA_BLK, band), 0) & (SWA_BLK - 1)
    ki = lax.broadcasted_iota(jnp.int32, (group * SWA_BLK, band), 1)

    def blk_body(n2, carry):
        geo = []
        for j in range(SWA_UNROLL):
            n = n2 * SWA_UNROLL + j
            start = pl.multiple_of(jnp.clip((n - 1) * SWA_BLK, 0, seq - band), SWA_BLK)
            q0 = pl.multiple_of(n * SWA_BLK, SWA_BLK)
            delta = (start + ki) - (q0 + qi)
            geo.append((start, q0, jnp.where(jnp.abs(delta) <= SWA_WINDOW, 0.0, NEG).astype(F32)))
        units = [(j, kv) for j in range(SWA_UNROLL) for kv in range(SWA_KV_HEADS)]

        def scores(j, kv):
            start, q0, mask_bias = geo[j]
            ksl = slice(kv * LANES, (kv + 1) * LANES)
            slabs = [q_ref[pl.ds(q0, SWA_BLK), (kv * pairs + i) * LANES:(kv * pairs + i + 1) * LANES]
                     for i in range(pairs)]
            return group_scores(slabs, kv, k_ref[pl.ds(start, band), ksl], kc_ref[:, ksl], mask_bias)

        def finish(j, kv, s1, s2):
            start, q0, _ = geo[j]
            ksl = slice(kv * LANES, (kv + 1) * LANES)
            outs = group_finish(kv, s1, s2, v_ref[pl.ds(start, band), ksl], vc_ref[:, ksl])
            for i in range(pairs):
                o_ref[pl.ds(q0, SWA_BLK), (kv * pairs + i) * LANES:(kv * pairs + i + 1) * LANES] = outs[i].astype(BF16)

        pending = scores(*units[0])
        for idx, u in enumerate(units):
            following = scores(*units[idx + 1]) if idx + 1 < len(units) else None
            finish(*u, *pending)
            pending = following
        return carry

    lax.fori_loop(0, n_blk // SWA_UNROLL, blk_body, 0)

    if need_ctx:
        for kv in range(SWA_KV_HEADS):
            ksl = slice(kv * LANES, (kv + 1) * LANES)
            slabs = [qc_ref[:, (kv * pairs + i) * LANES:(kv * pairs + i + 1) * LANES] for i in range(pairs)]
            _, s2 = group_scores(slabs, kv, None, kc_ref[:, ksl], None)
            outs = group_finish(kv, None, s2, None, vc_ref[:, ksl])
            for i in range(pairs):
                oc_ref[:, (kv * pairs + i) * LANES:(kv * pairs + i + 1) * LANES] = outs[i].astype(BF16)
    else:
        oc_ref[...] = jnp.zeros_like(oc_ref)


def _swa_attention(pb, sink, *, n_batch, seq, ctx, need_ctx, col0):
    hw = SWA_HEADS * HEAD_DIM
    kw = SWA_KV_HEADS * LANES
    qcol = col0 // hw
    kcol = (col0 + hw) // kw
    vcol = kcol + 1
    cblk = n_batch * seq // ctx
    kern = functools.partial(_swa_kernel, seq=seq, need_ctx=need_ctx)
    return pl.pallas_call(
        kern,
        out_shape=(jax.ShapeDtypeStruct((n_batch * seq, hw), BF16),
                   jax.ShapeDtypeStruct((n_batch * ctx, hw), BF16)),
        grid=(n_batch,),
        in_specs=[pl.BlockSpec(memory_space=pltpu.SMEM),
                  pl.BlockSpec((seq, hw), lambda b: (b, qcol)),
                  pl.BlockSpec((seq, kw), lambda b: (b, kcol)),
                  pl.BlockSpec((seq, kw), lambda b: (b, vcol)),
                  pl.BlockSpec((ctx, hw), lambda b: (cblk + b, qcol)),
                  pl.BlockSpec((ctx, kw), lambda b: (cblk + b, kcol)),
                  pl.BlockSpec((ctx, kw), lambda b: (cblk + b, vcol))],
        out_specs=(pl.BlockSpec((seq, hw), lambda b: (b, 0)),
                   pl.BlockSpec((ctx, hw), lambda b: (b, 0))),
        compiler_params=_cparams(("arbitrary",)),
        name="swa_attn",
    )(sink, pb, pb, pb, pb, pb, pb)


def _mla_project(cq, ckv, kr, qn_ref, kn_ref, wq_ref, wqs_ref, wk_ref, wv_ref, cq_ref, sq_ref, q_ref, k_ref, v_ref):
    kr2 = jnp.concatenate([kr, kr], axis=1)
    cqn = _rms(cq, qn_ref[...]).astype(BF16)
    ckvn = _rms(ckv, kn_ref[...]).astype(BF16)
    cw = 2 * LANES
    for c in range(MLA_HEADS * LANES // cw):
        sl = slice(c * cw, (c + 1) * cw)
        q = _dot(cqn, wq_ref[:, sl]) * cq_ref[...] + _dot(cqn, wqs_ref[:, sl]) * sq_ref[...]
        q_ref[:, sl] = q.astype(BF16)
        k_ref[:, sl] = (_dot(ckvn, wk_ref[:, sl]) + kr2).astype(BF16)
    ones_hi = jnp.where(_lane_half_masks(1), 0.0, 1.0).astype(F32)
    ones_hi = jnp.concatenate([ones_hi, ones_hi], axis=1)
    for c in range(MLA_HEADS * LANES // cw):
        sl = slice(c * cw, (c + 1) * cw)
        v_ref[:, sl] = (_dot(ckvn, wv_ref[:, sl]) + ones_hi).astype(BF16)


def _mla_body(q_ref, k_ref, v_ref, kc_ref, vc_ref, o_ref):
    sub = MLA_SUB
    n_sub = q_ref.shape[0] // sub
    lo = _lane_half_masks(sub)
    units = [(r, half) for r in range(n_sub) for half in range(2)]

    def scores(r, half):
        sl = slice(half * LANES, (half + 1) * LANES)
        q = q_ref[r * sub:(r + 1) * sub, sl]
        s1 = None if k_ref is None else _dot_t(q, k_ref[:, sl])
        return s1, _dot_t(q, kc_ref[:, sl])

    def finish(half, s1, s2):
        sl = slice(half * LANES, (half + 1) * LANES)
        if s1 is None:
            m = jnp.max(s2, axis=-1, keepdims=True)
            return _dot(jnp.exp2(s2 - m).astype(BF16), vc_ref[:, sl])
        m = jnp.maximum(jnp.max(s1, axis=-1, keepdims=True), jnp.max(s2, axis=-1, keepdims=True))
        p1 = jnp.exp2(s1 - m)
        p2 = jnp.exp2(s2 - m)
        return _dot(p1.astype(BF16), v_ref[:, sl]) + _dot(p2.astype(BF16), vc_ref[:, sl])

    outs = {}
    ahead = MLA_AHEAD
    sc = {u: scores(*u) for u in units[:ahead]}
    for idx, u in enumerate(units):
        if idx + ahead < len(units):
            sc[units[idx + ahead]] = scores(*units[idx + ahead])
        outs[u] = finish(u[1], *sc.pop(u))
    for r in range(n_sub):
        o0, o1 = outs[(r, 0)], outs[(r, 1)]
        r0 = pltpu.roll(o0, HEAD_DIM, 1)
        r1 = pltpu.roll(o1, HEAD_DIM, 1)
        o_ref[r * sub:(r + 1) * sub, :] = jnp.where(lo, o0 / r0, r1 / o1).astype(BF16)


def _mla_kernel(q_ref, k_ref, v_ref, kc_ref, vc_ref, o_ref):
    _mla_body(q_ref, k_ref, v_ref, kc_ref, vc_ref, o_ref)


def _mla_ctx_kernel(q_ref, kc_ref, vc_ref, o_ref):
    _mla_body(q_ref, None, None, kc_ref, vc_ref, o_ref)


def _mla_attention(qm, km, vm, *, n_batch, seq, ctx, need_ctx):
    n_qt = seq // MLA_TQ
    cblk = n_batch * seq // ctx
    hv = MLA_HEADS * MLA_V
    y_lat = pl.pallas_call(
        _mla_kernel,
        out_shape=jax.ShapeDtypeStruct((n_batch * seq, hv), BF16),
        grid=(n_batch, MLA_HEADS // 2, n_qt),
        in_specs=[pl.BlockSpec((MLA_TQ, 2 * LANES), lambda b, p, i: (b * n_qt + i, p)),
                  pl.BlockSpec((seq, 2 * LANES), lambda b, p, i: (b, p)),
                  pl.BlockSpec((seq, 2 * LANES), lambda b, p, i: (b, p)),
                  pl.BlockSpec((ctx, 2 * LANES), lambda b, p, i: (cblk + b, p)),
                  pl.BlockSpec((ctx, 2 * LANES), lambda b, p, i: (cblk + b, p))],
        out_specs=pl.BlockSpec((MLA_TQ, LANES), lambda b, p, i: (b * n_qt + i, p)),
        compiler_params=_cparams(("arbitrary", "arbitrary", "arbitrary")),
        name="mla_attn",
    )(qm, km, vm, km, vm)
    if not need_ctx:
        return y_lat
    assert ctx % MLA_SUB == 0
    y_ctx = pl.pallas_call(
        _mla_ctx_kernel,
        out_shape=jax.ShapeDtypeStruct((n_batch * ctx, hv), BF16),
        grid=(n_batch, MLA_HEADS // 2),
        in_specs=[pl.BlockSpec((ctx, 2 * LANES), lambda b, p: (cblk + b, p)),
                  pl.BlockSpec((ctx, 2 * LANES), lambda b, p: (cblk + b, p)),
                  pl.BlockSpec((ctx, 2 * LANES), lambda b, p: (cblk + b, p))],
        out_specs=pl.BlockSpec((ctx, LANES), lambda b, p: (b, p)),
        compiler_params=_cparams(("arbitrary", "arbitrary")),
        name="mla_ctx_attn",
    )(qm, km, vm)
    return y_lat, y_ctx


def _s5_scan_kernel(uf0_ref, uf1_ref, uf2_ref, uf3_ref, ur0_ref, ur1_ref, ur2_ref, ur3_ref,
                    bmat_ref, cmat_ref, lam_ref, of_ref, or_ref,
                    h_ref, stage_ref, lhs_ref, bu_ref, st_ref, ysc_ref, ya_ref, yb_ref):
    c = pl.program_id(0)
    n_seq = SUBLANES
    half = n_seq // 2
    steps = S5_STEPS
    pitch = S5_PITCH
    rows = steps * n_seq
    nblk = h_ref.shape[0]
    sw = h_ref.shape[2] // 2
    ufs = (uf0_ref, uf1_ref, uf2_ref, uf3_ref)
    urs = (ur0_ref, ur1_ref, ur2_ref, ur3_ref)
    is_fwd = (lax.broadcasted_iota(jnp.int32, (rows, 1), 0) % n_seq) < half
    sub_fwd = lax.broadcasted_iota(jnp.int32, (n_seq, LANES), 0) < half

    @pl.when(c == 0)
    def _():
        h_ref[...] = jnp.zeros_like(h_ref)

    def project_in(k):
        ks = slice(k * LANES, (k + 1) * LANES)
        z = k % 2
        for b in range(half):
            stage_ref[k, b * pitch:b * pitch + steps, :] = ufs[b][:, ks]
            stage_ref[k, (half + b) * pitch:(half + b) * pitch + steps, :] = urs[b][:, ks]
        for t in range(steps):
            ga = stage_ref[k, pl.ds(t, n_seq, stride=pitch), :]
            gb = stage_ref[k, pl.ds(steps - 1 - t, n_seq, stride=pitch), :]
            lhs_ref[z, t * n_seq:(t + 1) * n_seq, 0:LANES] = jnp.where(sub_fwd, ga, 0.0)
            lhs_ref[z, t * n_seq:(t + 1) * n_seq, LANES:2 * LANES] = jnp.where(sub_fwd, 0.0, gb)
        bu_ref[k] = _dot(lhs_ref[z].astype(BF16), bmat_ref[k])

    def recur(k):
        z = k % 2
        lre = lam_ref[k, :, 0:sw]
        lim = lam_ref[k, :, sw:2 * sw]
        hr = h_ref[k, :, 0:sw]
        hi = h_ref[k, :, sw:2 * sw]
        for t in range(steps):
            rs = slice(t * n_seq, (t + 1) * n_seq)
            hr, hi = (lre * hr - lim * hi + bu_ref[k, rs, 0:sw],
                      lre * hi + lim * hr + bu_ref[k, rs, sw:2 * sw])
            st_ref[z, rs, 0:sw] = hr
            st_ref[z, rs, sw:2 * sw] = hi
        h_ref[k, :, 0:sw] = hr
        h_ref[k, :, sw:2 * sw] = hi

    def read_out(k):
        ks = slice(k * LANES, (k + 1) * LANES)
        z = k % 2
        hr_rows = rows // 2
        for part in range(2):
            rs = slice(part * hr_rows, (part + 1) * hr_rows)
            y2 = _dot(st_ref[z, rs, :].astype(BF16), cmat_ref[k])
            ysc_ref[z, rs, :] = jnp.where(is_fwd[rs], y2[:, 0:LANES], y2[:, LANES:2 * LANES])
        for t in range(steps):
            g = ysc_ref[z, t * n_seq:(t + 1) * n_seq, :]
            ya_ref[k, pl.ds(t, n_seq, stride=pitch), :] = g
            yb_ref[k, pl.ds(steps - 1 - t, n_seq, stride=pitch), :] = g
        for b in range(half):
            of_ref[b, :, ks] = ya_ref[k, b * pitch:b * pitch + steps, :]
            or_ref[b, :, ks] = yb_ref[k, (half + b) * pitch:(half + b) * pitch + steps, :]

    for k in range(nblk):
        project_in(k)
    recur(0)
    for k in range(1, nblk):
        recur(k)
        read_out(k - 1)
    read_out(nblk - 1)


def _s5_scan(pf, bmat, cmat, lam, *, n_batch, seq, ctx, width):
    steps = S5_STEPS
    nblk = width // LANES
    rows = steps * SUBLANES
    sw2 = bmat.shape[-1]
    ncc = ctx // steps
    nlc = seq // steps
    ctx0 = n_batch * nlc

    def fwd_map(b):
        return lambda c: (jnp.where(c < ncc, ctx0 + b * ncc + c, b * nlc + (c - ncc)), 0)

    def rev_map(b):
        return lambda c: (jnp.where(c < ncc, ctx0 + b * ncc + (ncc - 1 - c), b * nlc + (nlc - 1 - (c - ncc))), 0)

    of_map = lambda c: (0, jnp.where(c < ncc, nlc + c, c - ncc), 0)
    or_map = lambda c: (0, jnp.where(c < ncc, nlc + (ncc - 1 - c), nlc - 1 - (c - ncc)), 0)
    u_specs = ([pl.BlockSpec((steps, width), fwd_map(b)) for b in range(n_batch)]
               + [pl.BlockSpec((steps, width), rev_map(b)) for b in range(n_batch)])
    out_sds = jax.ShapeDtypeStruct((n_batch, seq + ctx, width), F32)
    stage = pltpu.VMEM((nblk, SUBLANES * S5_PITCH, LANES), F32)
    return pl.pallas_call(
        _s5_scan_kernel,
        out_shape=(out_sds, out_sds),
        grid=(ncc + nlc,),
        in_specs=u_specs + [_const_spec(bmat.shape), _const_spec(cmat.shape), _const_spec(lam.shape)],
        out_specs=(pl.BlockSpec((n_batch, steps, width), of_map),
                   pl.BlockSpec((n_batch, steps, width), or_map)),
        scratch_shapes=[pltpu.VMEM((nblk, SUBLANES, sw2), F32),
                        stage, pltpu.VMEM((2, rows, 2 * LANES), F32),
                        pltpu.VMEM((nblk, rows, sw2), F32), pltpu.VMEM((2, rows, sw2), F32),
                        pltpu.VMEM((2, rows, LANES), F32), stage, stage],
        compiler_params=_cparams(("arbitrary",)),
        name="s5_scan",
    )(*([pf] * (2 * n_batch)), bmat, cmat, lam)


def _s5_post_kernel(pf_ref, yf_ref, yr_ref, d_ref, w_ref, b_ref, o_ref, *, width):
    u = pf_ref[:, 0:width]
    y = d_ref[...] * u + yf_ref[0] + yr_ref[0]
    k0 = math.sqrt(2.0 / math.pi)
    g = 0.5 * y * (1.0 + jnp.tanh(k0 * (y + 0.044715 * (y * y * y))))
    z = _dot(g.astype(BF16), w_ref[...]) + b_ref[...]
    o_ref[...] = (g * jax.nn.sigmoid(z)).astype(BF16)


def _s5_post(pf, yf, yr, d_skip, glu_w, glu_b, *, n_batch, seq, ctx, need_ctx):
    width = yf.shape[2]
    kern = functools.partial(_s5_post_kernel, width=width)
    consts = [_const_spec((1, width)), _const_spec(glu_w.shape), _const_spec((1, width))]

    def call(tp, n_tiles, row_block0, seq_block0, n_out, name):
        return pl.pallas_call(
            kern,
            out_shape=jax.ShapeDtypeStruct((n_out, width), BF16),
            grid=(n_batch, n_tiles),
            in_specs=[pl.BlockSpec((tp, width), lambda b, j: (row_block0 + b * n_tiles + j, 0)),
                      pl.BlockSpec((1, tp, width), lambda b, j: (b, seq_block0 + j, 0)),
                      pl.BlockSpec((1, tp, width), lambda b, j: (b, seq_block0 + j, 0))] + consts,
            out_specs=pl.BlockSpec((tp, width), lambda b, j: (b * n_tiles + j, 0)),
            compiler_params=_cparams(("arbitrary", "arbitrary")),
            name=name,
        )(pf, yf, yr, d_skip, glu_w, glu_b)

    tp = min(S5_POST_ROWS, seq)
    y_lat = call(tp, seq // tp, 0, 0, n_batch * seq, "s5_post")
    if not need_ctx:
        return y_lat
    return y_lat, call(ctx, 1, n_batch * seq // ctx, seq // ctx, n_batch * ctx, "s5_post_ctx")


def _merge_kernel(x_ref, mod_ref, nw_ref, *rest, has_tail, n_first):
    y_refs = rest[:N_BRANCH + sum(has_tail)]
    wg_ref, wb_ref, wo_ref, o_ref, acc_ref = rest[len(y_refs):]
    x = x_ref[...]
    d = x.shape[1]
    m = mod_ref[0]
    n = _rms_mod(x, nw_ref[...], m[3:4], m[4:5]).astype(BF16)
    ys = []
    pos = 0
    for b in range(N_BRANCH):
        y = y_refs[pos][...]
        pos += 1
        if has_tail[b]:
            y = jnp.where(pl.program_id(0) < n_first, y, y_refs[pos][...])
            pos += 1
        ys.append(y)
    for b in range(N_BRANCH):
        gate = jax.nn.sigmoid(_dot(n, wg_ref[:, b * d:(b + 1) * d]))
        contrib = gate * _dot(ys[b], wb_ref[b])
        if b == 0:
            acc_ref[...] = contrib
        else:
            acc_ref[...] += contrib
    o_ref[...] = x + m[5:6] * _dot(acc_ref[...].astype(BF16), wo_ref[...])


def _merge(h, n_rows, mod, norm_w, ys, wg, wb, wo, *, n_lat_tiles, tiles_per_batch, n_batch):
    d = h.shape[1]
    bw = wb.shape[1]
    n_first = n_lat_tiles
    has_tail = tuple(isinstance(y, tuple) for y in ys)
    y_specs, y_ops = [], []
    for y in ys:
        if isinstance(y, tuple):
            y_specs += [pl.BlockSpec((TM, bw), lambda i: (jnp.minimum(i, n_first - 1), 0)),
                        pl.BlockSpec((TM, bw), lambda i: (jnp.maximum(i - n_first, 0), 0))]
            y_ops += list(y)
        else:
            y_specs.append(pl.BlockSpec((TM, bw), lambda i: (i, 0)))
            y_ops.append(y)
    kern = functools.partial(_merge_kernel, has_tail=has_tail, n_first=n_first)
    return pl.pallas_call(
        kern,
        out_shape=jax.ShapeDtypeStruct((n_rows, d), F32),
        grid=(n_rows // TM,),
        in_specs=[pl.BlockSpec((TM, d), lambda i: (i, 0)),
                  pl.BlockSpec((1, N_MOD, d), lambda i: (_mod_row_map(n_lat_tiles, tiles_per_batch, n_batch)(i), 0, 0)),
                  _const_spec((1, d))]
                 + y_specs
                 + [_const_spec(wg.shape), _const_spec(wb.shape), _const_spec(wo.shape)],
        out_specs=pl.BlockSpec((TM, d), lambda i: (i, 0)),
        scratch_shapes=[pltpu.VMEM((TM, d), F32)],
        compiler_params=_cparams(("arbitrary",)),
        name="merge",
    )(h, mod, norm_w, *y_ops, wg, wb, wo)


def _swap_rot_pairs(w, nf):
    lead = w.shape[:-1]
    n = w.shape[-1]
    return w.reshape(lead + (n // (2 * nf), 2, nf))[..., ::-1, :].reshape(lead + (n,))


def _rope_tables(seq, dim, lane_off, width, period):
    nf = dim // 4
    pos = jnp.arange(seq)
    rows = (pos // GRID_W).astype(F32)
    cols = (pos % GRID_W).astype(F32)
    inv_freq = ROPE_BASE ** (-jnp.arange(nf, dtype=F32) / nf)
    ang_r = rows[:, None] * inv_freq[None, :]
    ang_c = cols[:, None] * inv_freq[None, :]
    cos = jnp.concatenate([jnp.cos(ang_r)] * 2 + [jnp.cos(ang_c)] * 2, axis=1)
    sin = jnp.concatenate([-jnp.sin(ang_r), jnp.sin(ang_r), -jnp.sin(ang_c), jnp.sin(ang_c)], axis=1)
    c_per = jnp.ones((seq, period), F32).at[:, lane_off:lane_off + dim].set(cos)
    s_per = jnp.zeros((seq, period), F32).at[:, lane_off:lane_off + dim].set(sin)
    reps = width // period
    c_tab = jnp.concatenate([jnp.tile(c_per, (1, reps)), jnp.ones((TM, width), F32)], axis=0)
    s_tab = jnp.concatenate([jnp.tile(s_per, (1, reps)), jnp.zeros((TM, width), F32)], axis=0)
    return c_tab, s_tab


def _inproj_weights(w_in):
    d = w_in.shape[0]
    hw = NA_HEADS * HEAD_DIM
    o_sq = 3 * hw
    o_sk = o_sq + SWA_HEADS * HEAD_DIM
    o_sv = o_sk + SWA_KV_HEADS * HEAD_DIM
    o_s5 = o_sv + SWA_KV_HEADS * HEAD_DIM
    s5w = 512
    o_cq = o_s5 + s5w
    o_ckv = o_cq + 256
    o_kr = o_ckv + 128
    o_g = o_kr + MLA_ROPE
    qscale = LOG2E * HEAD_DIM ** -0.5
    na = jnp.concatenate([w_in[:, :hw] * qscale, w_in[:, hw:3 * hw]], axis=1)
    sq = w_in[:, o_sq:o_sk] * qscale
    sk = w_in[:, o_sk:o_sv]
    sv = w_in[:, o_sv:o_s5]
    def dup_heads(a):
        return jnp.concatenate([a[:, kv * HEAD_DIM:(kv + 1) * HEAD_DIM]
                                for kv in range(SWA_KV_HEADS) for _ in range(2)], axis=1)

    sk_dup = dup_heads(sk)
    sv_dup = dup_heads(sv)
    wr = jnp.concatenate([sq, sk_dup], axis=1)
    kr = w_in[:, o_kr:o_g]
    lpad = jnp.zeros((d, MLA_NOPE), F32)
    rpad = jnp.zeros((d, LANES - MLA_NOPE - MLA_ROPE), F32)
    wf = jnp.concatenate([w_in[:, o_s5:o_kr], lpad, kr, rpad], axis=1)
    wg = w_in[:, o_g:]
    proj = tuple(a.astype(BF16) for a in (na, wr, sv_dup, wf))
    return proj, wg.astype(BF16)


def _mla_weights(w_uq, w_ukv):
    ql = w_uq.shape[0]
    kvl = w_ukv.shape[0]
    dq = MLA_NOPE + MLA_ROPE
    wq3 = w_uq.reshape(ql, MLA_HEADS, dq)
    pad = jnp.zeros((ql, MLA_HEADS, LANES - dq), F32)
    qscale = math.log2(math.e) * dq ** -0.5
    wq = jnp.concatenate([wq3 * qscale, pad], axis=2).reshape(ql, MLA_HEADS * LANES)
    wkv3 = w_ukv.reshape(kvl, MLA_HEADS, MLA_NOPE + MLA_V)
    wk = jnp.concatenate([wkv3[:, :, :MLA_NOPE], jnp.zeros((kvl, MLA_HEADS, LANES - MLA_NOPE), F32)], axis=2)
    wk = wk.reshape(kvl, MLA_HEADS * LANES)
    wv = jnp.concatenate([wkv3[:, :, MLA_NOPE:], jnp.zeros((kvl, MLA_HEADS, LANES - MLA_V), F32)], axis=2)
    wv = wv.reshape(kvl, MLA_HEADS * LANES)
    rope_sw = _swap_rot_pairs(wq3[:, :, MLA_NOPE:], MLA_ROPE // 4) * qscale
    wqs = jnp.concatenate([jnp.zeros((ql, MLA_HEADS, MLA_NOPE), F32), rope_sw, pad], axis=2)
    wqs = wqs.reshape(ql, MLA_HEADS * LANES)
    return tuple(a.astype(BF16) for a in (wq, wqs, wk, wv))


def _s5_params(lam_re, lam_im, log_dt, b_re, b_im, c_re, c_im):
    a = lam_re.astype(F32)
    w = lam_im.astype(F32)
    dt = jnp.exp(log_dt.astype(F32))[..., None]
    mag = jnp.exp(a * dt)
    lb_re = mag * jnp.cos(w * dt)
    lb_im = mag * jnp.sin(w * dt)
    den = a * a + w * w
    cf_re = ((lb_re - 1.0) * a + lb_im * w) / den
    cf_im = (lb_im * a - (lb_re - 1.0) * w) / den
    bb_re = cf_re[..., None] * b_re - cf_im[..., None] * b_im
    bb_im = cf_re[..., None] * b_im + cf_im[..., None] * b_re
    n_dir, g, p, cg = b_re.shape
    gpb = LANES // cg
    nblk = g // gpb
    eye = jnp.eye(gpb, dtype=F32)

    def in_map(x):
        x5 = jnp.swapaxes(x.reshape(n_dir, nblk, gpb, p, cg), 3, 4)
        full = x5[:, :, :, :, None, :] * eye[None, None, :, None, :, None]
        return full.reshape(n_dir, nblk, gpb * cg, gpb * p)

    def out_map(x):
        x5 = jnp.swapaxes(x.reshape(n_dir, nblk, gpb, cg, p), 3, 4)
        full = x5[:, :, :, :, None, :] * eye[None, None, :, None, :, None]
        return full.reshape(n_dir, nblk, gpb * p, gpb * cg)

    b_in = jnp.concatenate([in_map(bb_re), in_map(bb_im)], axis=3)
    bmat = jnp.concatenate([b_in[0], b_in[1]], axis=1).astype(BF16)
    c_out = jnp.concatenate([out_map(c_re.astype(F32)), out_map(-c_im.astype(F32))], axis=2)
    cmat = jnp.concatenate([c_out[0], c_out[1]], axis=2).astype(BF16)
    half = SUBLANES // 2
    lam2 = jnp.concatenate([lb_re.reshape(n_dir, nblk, gpb * p), lb_im.reshape(n_dir, nblk, gpb * p)], axis=2)
    lam = jnp.concatenate([jnp.broadcast_to(lam2[0][:, None, :], (nblk, half, 2 * gpb * p)),
                           jnp.broadcast_to(lam2[1][:, None, :], (nblk, half, 2 * gpb * p))], axis=1)
    return bmat, cmat, lam


def _ffn_weights(wg, wu, wd):
    d, ff = wg.shape
    wg3 = wg.astype(BF16)
    wu3 = wu.astype(BF16)
    wd3 = wd.astype(BF16)
    return wg3, wu3, wd3


def kernel(x, c, ctx, c_ctx, ada_w, ada_b, ffn1_norm, ffn1_w_gate, ffn1_w_up, ffn1_w_down, mix_norm, w_in, na_rpb, swa_sink, s5_lambda_re, s5_lambda_im, s5_log_dt, s5_b_re, s5_b_im, s5_c_re, s5_c_im, s5_d, s5_glu_w, s5_glu_b, mla_q_norm, mla_w_uq, mla_kv_norm, mla_w_ukv, w_branch, w_out, ffn2_norm, ffn2_w_gate, ffn2_w_up, ffn2_w_down, final_norm):
    n_batch, seq, d = x.shape
    n_ctx = ctx.shape[1]
    depth = ada_w.shape[0]
    assert 2 * n_batch == SUBLANES and seq % TM == 0 and (n_batch * n_ctx) % TM == 0
    n_lat = n_batch * seq
    n_all = n_lat + n_batch * n_ctx
    tiles_per_batch = seq // TM
    n_lat_tiles = n_lat // TM
    geo = dict(n_lat_tiles=n_lat_tiles, tiles_per_batch=tiles_per_batch, n_batch=n_batch)

    h = x.reshape(n_lat, d)
    h_ctx = ctx.reshape(n_batch * n_ctx, d)
    cc = jnp.concatenate([c, c_ctx[None, :], jnp.zeros((SUBLANES - n_batch - 1, d), F32)], axis=0)
    mod = _ada_mod(cc, ada_w, ada_b)

    cs_sw, sn_sw = _rope_tables(seq, HEAD_DIM, 0, 2 * LANES, HEAD_DIM)
    ck_kr, sk_kr = _rope_tables(seq, MLA_ROPE, LANES + MLA_NOPE, 2 * LANES, 2 * LANES)
    cq_ml, sq_ml = _rope_tables(seq, MLA_ROPE, MLA_NOPE, 2 * LANES, LANES)
    rows_n = seq // GRID_W
    s5w = s5_d.shape[1]
    sw_col0 = 3 * NA_HEADS * HEAD_DIM

    w_ffn1 = _ffn_weights(ffn1_w_gate[0], ffn1_w_up[0], ffn1_w_down[0])
    for l in range(depth):
        need_ctx = l < depth - 1
        last = l == depth - 1
        ml = mod[l]
        ones = jnp.ones((1, d), F32)
        h, *w_ffn2 = _ffn(h, n_all, ml, ffn1_norm[l][None, :], *w_ffn1, ones, base=0, final=False,
                          h_tail=h_ctx if l == 0 else None,
                          cast_next=(l, ffn2_w_gate, ffn2_w_up, ffn2_w_down), **geo)
        proj_w, gate_w = _inproj_weights(w_in[l])
        mla_w = _mla_weights(mla_w_uq[l], mla_w_ukv[l])
        pb, pf, qm, km, vm = _inproj(h, ml, mix_norm[l][None, :], proj_w, (cs_sw, sn_sw, ck_kr, sk_kr),
                                     (mla_q_norm[l][None, :], mla_kv_norm[l][None, :], *mla_w, cq_ml, sq_ml), **geo)
        bias = _na_bias_table(na_rpb[l].astype(F32), rows_n)
        y_na = _na_attention(pb, bias, n_batch=n_batch, seq=seq, ctx=n_ctx, need_ctx=need_ctx)
        y_sw_l, y_sw_c = _swa_attention(pb, swa_sink[l].astype(F32) * LOG2E, n_batch=n_batch, seq=seq, ctx=n_ctx,
                                        need_ctx=need_ctx, col0=sw_col0)
        y_sw = (y_sw_l, y_sw_c) if need_ctx else y_sw_l
        y_mla = _mla_attention(qm, km, vm, n_batch=n_batch, seq=seq, ctx=n_ctx, need_ctx=need_ctx)
        s5p = _s5_params(s5_lambda_re[l], s5_lambda_im[l], s5_log_dt[l], s5_b_re[l], s5_b_im[l],
                         s5_c_re[l], s5_c_im[l])
        yf, yr = _s5_scan(pf, *s5p, n_batch=n_batch, seq=seq, ctx=n_ctx, width=s5w)
        y_s5 = _s5_post(pf, yf, yr, s5_d[l][None, :].astype(F32), s5_glu_w[l].astype(BF16),
                        s5_glu_b[l][None, :].astype(F32), n_batch=n_batch, seq=seq, ctx=n_ctx, need_ctx=need_ctx)
        n_rows = n_all if need_ctx else n_lat
        h = _merge(h, n_rows, ml, mix_norm[l][None, :], (y_na, y_sw, y_s5, y_mla), gate_w,
                   w_branch[l].astype(BF16), w_out[l].astype(BF16), **geo)
        if last:
            h = _ffn(h, n_rows, ml, ffn2_norm[l][None, :], *w_ffn2, final_norm[None, :], base=6, final=True, **geo)
        else:
            h, *w_ffn1 = _ffn(h, n_rows, ml, ffn2_norm[l][None, :], *w_ffn2, final_norm[None, :], base=6, final=False,
                              cast_next=(l + 1, ffn1_w_gate, ffn1_w_up, ffn1_w_down), **geo)
    return h.reshape(n_batch, seq, d)
```
